```python
import math
import jax, jax.numpy as jnp
from jax import lax
import numpy as np

D_MODEL = 4096
BATCH = 1
SEQ = 8192
DEPTH = 4

N_MIXERS = 3
BLOCK_Q = 128
LN_EPS = 1e-5
RMS_EPS = 1e-6

N_SB_LAYERS = (DEPTH + N_MIXERS - 1) // N_MIXERS
N_MLA_LAYERS = (DEPTH + N_MIXERS - 2) // N_MIXERS
N_DIL_LAYERS = DEPTH // N_MIXERS

SB_HEADS = 32
SB_HEAD_DIM = 128

MLA_HEADS = 32
MLA_Q_RANK = 1024
MLA_KV_RANK = 512
MLA_NOPE_DIM = 128
MLA_ROPE_DIM = 64
MLA_V_DIM = 128
ROPE_THETA = 10000.0

DIL_WINDOWS = (128, 512, 2048)
DIL_DILATIONS = (1, 4, 16)
DIL_GROUPS = 3
DIL_HEADS = 16
DIL_HEAD_DIM = 128
DIL_KEYS = DIL_WINDOWS[0] // DIL_DILATIONS[0] + 1

REL_BUCKETS = 32
REL_MAX_DIST = 2048

MOE_GROUPS = 4
MOE_EXPERTS_PER_GROUP = 8
MOE_TOP_K = 2
EXPERT_HIDDEN = 256

DN_ALPHA = (2 * DEPTH) ** 0.25
DN_BETA = (8 * DEPTH) ** -0.25

kernel_name = "hybrid_sb_mla_dilated_hmoe_deepnorm"


def layer_norm(x, g, b):
    xf = x.astype(jnp.float32)
    mu = jnp.mean(xf, axis=-1, keepdims=True)
    var = jnp.mean(jnp.square(xf - mu), axis=-1, keepdims=True)
    return ((xf - mu) * lax.rsqrt(var + LN_EPS) * g + b).astype(x.dtype)


def rms_norm(x, g):
    xf = x.astype(jnp.float32)
    return (xf * lax.rsqrt(jnp.mean(xf * xf, axis=-1, keepdims=True) + RMS_EPS) * g).astype(x.dtype)


def map_query_blocks(fn, n_blocks):
    out = lax.map(fn, jnp.arange(n_blocks))
    out = jnp.moveaxis(out, 0, 1)
    return out.reshape(out.shape[0], -1, *out.shape[3:])


def apply_rope(x, pos):
    half = x.shape[-1] // 2
    inv_freq = ROPE_THETA ** (-jnp.arange(half, dtype=jnp.float32) / half)
    ang = (pos.astype(jnp.float32)[..., None] * inv_freq)[:, :, None, :]
    cos, sin = jnp.cos(ang), jnp.sin(ang)
    x1 = x[..., :half].astype(jnp.float32)
    x2 = x[..., half:].astype(jnp.float32)
    return jnp.concatenate([x1 * cos - x2 * sin, x2 * cos + x1 * sin], axis=-1).astype(x.dtype)


def t5_bucket(rel):
    n = jnp.maximum(rel, 0)
    max_exact = REL_BUCKETS // 2
    nf = jnp.maximum(n, 1).astype(jnp.float32)
    large = max_exact + (jnp.log(nf / max_exact) / math.log(REL_MAX_DIST / max_exact)
                         * (REL_BUCKETS - max_exact)).astype(jnp.int32)
    large = jnp.minimum(large, REL_BUCKETS - 1)
    return jnp.where(n < max_exact, n, large)


def stick_breaking_attention(x, w_qkv, w_o):
    B, S, _ = x.shape
    H, Dh = SB_HEADS, SB_HEAD_DIM
    qkv = (x @ w_qkv).reshape(B, S, 3, H, Dh)
    q, k, v = qkv[:, :, 0], qkv[:, :, 1], qkv[:, :, 2]
    scale = Dh ** -0.5
    key_pos = jnp.arange(S)

    def block(i):
        start = i * BLOCK_Q
        q_blk = lax.dynamic_slice_in_dim(q, start, BLOCK_Q, axis=1)
        z = jnp.einsum('bqhd,bkhd->bhqk', q_blk, k).astype(jnp.float32) * scale
        q_pos = start + jnp.arange(BLOCK_Q)
        strict = key_pos[None, :] < q_pos[:, None]
        log_keep = jnp.where(strict, jax.nn.log_sigmoid(-z), 0.0)
        later = lax.cumsum(log_keep, axis=3, reverse=True) - log_keep
        a = jnp.where(strict, jnp.exp(jax.nn.log_sigmoid(z) + later), 0.0)
        return jnp.einsum('bhqk,bkhd->bqhd', a.astype(v.dtype), v)

    o = map_query_blocks(block, S // BLOCK_Q)
    return o.reshape(B, S, H * Dh) @ w_o


def mla_attention(x, pos, w_q_a, q_a_norm, w_q_b, w_kv_a, kv_a_norm, w_kv_b, w_o):
    B, S, _ = x.shape
    H = MLA_HEADS
    cq = rms_norm(x @ w_q_a, q_a_norm)
    q = (cq @ w_q_b).reshape(B, S, H, MLA_NOPE_DIM + MLA_ROPE_DIM)
    q_nope = q[..., :MLA_NOPE_DIM]
    q_rope = apply_rope(q[..., MLA_NOPE_DIM:], pos)
    kv_a = x @ w_kv_a
    c_kv = rms_norm(kv_a[..., :MLA_KV_RANK], kv_a_norm)
    k_rope = apply_rope(kv_a[..., MLA_KV_RANK:][:, :, None, :], pos)[:, :, 0]
    kv = (c_kv @ w_kv_b).reshape(B, S, H, MLA_NOPE_DIM + MLA_V_DIM)
    k_nope, v = kv[..., :MLA_NOPE_DIM], kv[..., MLA_NOPE_DIM:]
    scale = (MLA_NOPE_DIM + MLA_ROPE_DIM) ** -0.5
    key_pos = jnp.arange(S)

    def block(i):
        start = i * BLOCK_Q
        qn = lax.dynamic_slice_in_dim(q_nope, start, BLOCK_Q, axis=1)
        qr = lax.dynamic_slice_in_dim(q_rope, start, BLOCK_Q, axis=1)
        s = (jnp.einsum('bqhd,bkhd->bhqk', qn, k_nope)
             + jnp.einsum('bqhr,bkr->bhqk', qr, k_rope)).astype(jnp.float32) * scale
        causal = key_pos[None, :] <= (start + jnp.arange(BLOCK_Q))[:, None]
        p = jax.nn.softmax(jnp.where(causal, s, -jnp.inf), axis=-1)
        return jnp.einsum('bhqk,bkhd->bqhd', p.astype(v.dtype), v)

    o = map_query_blocks(block, S // BLOCK_Q)
    return o.reshape(B, S, H * MLA_V_DIM) @ w_o


def dilated_attention(x, pos, rel_bias, w_qkv, w_o):
    B, S, _ = x.shape
    G, H, Dh, K = DIL_GROUPS, DIL_HEADS, DIL_HEAD_DIM, DIL_KEYS
    qkv = (x @ w_qkv).reshape(B, S, 3, G, H, Dh)
    q, k, v = qkv[:, :, 0], qkv[:, :, 1], qkv[:, :, 2]
    dil = jnp.array(DIL_DILATIONS, jnp.int32)
    offsets = dil[:, None] * jnp.arange(K, dtype=jnp.int32)[None, :]
    bias_tab = rel_bias.reshape(REL_BUCKETS, G, H)
    g_ix = jnp.arange(G)[:, None, None]
    g_ix_b = jnp.arange(G)[None, :, None, None]
    scale = Dh ** -0.5

    def block(i):
        start = i * BLOCK_Q
        q_idx = start + jnp.arange(BLOCK_Q)
        key_idx = q_idx[None, :, None] - offsets[:, None, :]
        valid = key_idx >= 0
        kid = jnp.maximum(key_idx, 0)
        k_sel = k[:, kid, g_ix]
        v_sel = v[:, kid, g_ix]
        q_blk = lax.dynamic_slice_in_dim(q, start, BLOCK_Q, axis=1)
        s = jnp.einsum('bqghd,bgqkhd->bghqk', q_blk, k_sel).astype(jnp.float32) * scale
        rel = pos[:, q_idx][:, None, :, None] - pos[:, kid]
        bias = bias_tab[t5_bucket(rel), g_ix_b]
        s = s + jnp.moveaxis(bias, -1, 2).astype(jnp.float32)
        s = jnp.where(valid[None, :, None], s, -jnp.inf)
        lse = jax.nn.logsumexp(s, axis=-1)
        p = jnp.exp(s - lse[..., None])
        o_g = jnp.einsum('bghqk,bgqkhd->bgqhd', p.astype(v.dtype), v_sel)
        w = jax.nn.softmax(lse, axis=1)
        return jnp.einsum('bghq,bgqhd->bqhd', w.astype(v.dtype), o_g)

    o = map_query_blocks(block, S // BLOCK_Q)
    return o.reshape(B, S, H * Dh) @ w_o


def hierarchical_moe(x, w_group_router, b_group_router, w_expert_router, b_expert_router,
                     w_gate, w_up, w_down):
    B, S, _ = x.shape
    G, E = MOE_GROUPS, MOE_EXPERTS_PER_GROUP
    group_logits = (x @ w_group_router + b_group_router).astype(jnp.float32)
    group_prob = jax.nn.softmax(group_logits, axis=-1)
    g_top = jnp.argmax(group_logits, axis=-1)
    g_gate = jnp.take_along_axis(group_prob, g_top[..., None], axis=-1)
    exp_logits = (jnp.einsum('bsd,gde->bsge', x, w_expert_router)
                  + b_expert_router).astype(jnp.float32)
    sel_logits = jnp.take_along_axis(exp_logits, g_top[..., None, None], axis=2)[:, :, 0]
    top_vals, top_idx = lax.top_k(sel_logits, MOE_TOP_K)
    top_w = jax.nn.softmax(top_vals, axis=-1) * g_gate
    within = jnp.sum(jax.nn.one_hot(top_idx, E, dtype=jnp.float32) * top_w[..., None], axis=2)
    combine = jax.nn.one_hot(g_top, G, dtype=jnp.float32)[..., None] * within[:, :, None, :]
    combine = combine.reshape(B, S, G * E).astype(x.dtype)
    h = jax.nn.silu(jnp.einsum('bsd,edf->bsef', x, w_gate)) * jnp.einsum('bsd,edf->bsef', x, w_up)
    return jnp.einsum('bsef,efd->bsd', h * combine[..., None], w_down)


def setup_inputs(seed: int = 0) -> dict:
    key = jax.random.key(seed)
    ks = iter(jax.random.split(key, 32))

    def dense(shape, fan_in, gain=1.0):
        return jax.random.normal(next(ks), shape, jnp.float32) * (gain * fan_in ** -0.5)

    def gain_vec(shape):
        return 1.0 + 0.02 * jax.random.normal(next(ks), shape, jnp.float32)

    D = D_MODEL
    NE = MOE_GROUPS * MOE_EXPERTS_PER_GROUP
    x = jax.random.normal(next(ks), (BATCH, SEQ, D), jnp.float32)
    offset = jax.random.randint(next(ks), (BATCH, 1), 0, 1024, dtype=jnp.int32)
    positions = offset + jnp.arange(SEQ, dtype=jnp.int32)[None, :]
    rel_bias = 0.1 * jax.random.normal(next(ks), (REL_BUCKETS, DIL_GROUPS * DIL_HEADS), jnp.float32)
    sb_w_qkv = dense((N_SB_LAYERS, D, 3 * SB_HEADS * SB_HEAD_DIM), D)
    sb_w_o = dense((N_SB_LAYERS, SB_HEADS * SB_HEAD_DIM, D), SB_HEADS * SB_HEAD_DIM, DN_BETA)
    mla_w_q_a = dense((N_MLA_LAYERS, D, MLA_Q_RANK), D)
    mla_q_a_norm = gain_vec((N_MLA_LAYERS, MLA_Q_RANK))
    mla_w_q_b = dense((N_MLA_LAYERS, MLA_Q_RANK, MLA_HEADS * (MLA_NOPE_DIM + MLA_ROPE_DIM)), MLA_Q_RANK)
    mla_w_kv_a = dense((N_MLA_LAYERS, D, MLA_KV_RANK + MLA_ROPE_DIM), D)
    mla_kv_a_norm = gain_vec((N_MLA_LAYERS, MLA_KV_RANK))
    mla_w_kv_b = dense((N_MLA_LAYERS, MLA_KV_RANK, MLA_HEADS * (MLA_NOPE_DIM + MLA_V_DIM)), MLA_KV_RANK)
    mla_w_o = dense((N_MLA_LAYERS, MLA_HEADS * MLA_V_DIM, D), MLA_HEADS * MLA_V_DIM, DN_BETA)
    dil_w_qkv = dense((N_DIL_LAYERS, D, 3 * DIL_GROUPS * DIL_HEADS * DIL_HEAD_DIM), D)
    dil_w_o = dense((N_DIL_LAYERS, DIL_HEADS * DIL_HEAD_DIM, D), DIL_HEADS * DIL_HEAD_DIM, DN_BETA)
    ln_gain = gain_vec((DEPTH, 2, D))
    ln_bias = 0.02 * jax.random.normal(next(ks), (DEPTH, 2, D), jnp.float32)
    moe_w_group_router = dense((DEPTH, D, MOE_GROUPS), D)
    moe_b_group_router = 0.01 * jax.random.normal(next(ks), (DEPTH, MOE_GROUPS), jnp.float32)
    moe_w_expert_router = dense((DEPTH, MOE_GROUPS, D, MOE_EXPERTS_PER_GROUP), D)
    moe_b_expert_router = 0.01 * jax.random.normal(next(ks), (DEPTH, MOE_GROUPS, MOE_EXPERTS_PER_GROUP), jnp.float32)
    moe_w_gate = dense((DEPTH, NE, D, EXPERT_HIDDEN), D)
    moe_w_up = dense((DEPTH, NE, D, EXPERT_HIDDEN), D)
    moe_w_down = dense((DEPTH, NE, EXPERT_HIDDEN, D), EXPERT_HIDDEN, DN_BETA)
    return {"x": x, "positions": positions, "rel_bias": rel_bias,
            "sb_w_qkv": sb_w_qkv, "sb_w_o": sb_w_o,
            "mla_w_q_a": mla_w_q_a, "mla_q_a_norm": mla_q_a_norm, "mla_w_q_b": mla_w_q_b,
            "mla_w_kv_a": mla_w_kv_a, "mla_kv_a_norm": mla_kv_a_norm, "mla_w_kv_b": mla_w_kv_b,
            "mla_w_o": mla_w_o,
            "dil_w_qkv": dil_w_qkv, "dil_w_o": dil_w_o,
            "ln_gain": ln_gain, "ln_bias": ln_bias,
            "moe_w_group_router": moe_w_group_router, "moe_b_group_router": moe_b_group_router,
            "moe_w_expert_router": moe_w_expert_router, "moe_b_expert_router": moe_b_expert_router,
            "moe_w_gate": moe_w_gate, "moe_w_up": moe_w_up, "moe_w_down": moe_w_down}


def reference(x, positions, rel_bias,
              sb_w_qkv, sb_w_o,
              mla_w_q_a, mla_q_a_norm, mla_w_q_b, mla_w_kv_a, mla_kv_a_norm, mla_w_kv_b, mla_w_o,
              dil_w_qkv, dil_w_o,
              ln_gain, ln_bias,
              moe_w_group_router, moe_b_group_router, moe_w_expert_router, moe_b_expert_router,
              moe_w_gate, moe_w_up, moe_w_down):
    h = x
    for i in range(DEPTH):
        kind, j = i % N_MIXERS, i // N_MIXERS
        if kind == 0:
            mix = stick_breaking_attention(h, sb_w_qkv[j], sb_w_o[j])
        elif kind == 1:
            mix = mla_attention(h, positions, mla_w_q_a[j], mla_q_a_norm[j], mla_w_q_b[j],
                                mla_w_kv_a[j], mla_kv_a_norm[j], mla_w_kv_b[j], mla_w_o[j])
        else:
            mix = dilated_attention(h, positions, rel_bias, dil_w_qkv[j], dil_w_o[j])
        h = layer_norm(DN_ALPHA * h + mix, ln_gain[i, 0], ln_bias[i, 0])
        ffn = hierarchical_moe(h, moe_w_group_router[i], moe_b_group_router[i],
                               moe_w_expert_router[i], moe_b_expert_router[i],
                               moe_w_gate[i], moe_w_up[i], moe_w_down[i])
        h = layer_norm(DN_ALPHA * h + ffn, ln_gain[i, 1], ln_bias[i, 1])
    return h
```

```python
import functools
import math

import jax
import jax.numpy as jnp
from jax import lax
from jax.experimental import pallas as pl
from jax.experimental.pallas import tpu as pltpu

DEPTH = 4
N_MIXERS = 3
LN_EPS = 1e-5
RMS_EPS = 1e-6

SB_HEADS = 32
SB_HEAD_DIM = 128

MLA_HEADS = 32
MLA_Q_RANK = 1024
MLA_KV_RANK = 512
MLA_NOPE_DIM = 128
MLA_ROPE_DIM = 64
MLA_V_DIM = 128
ROPE_THETA = 10000.0

DIL_DILATIONS = (1, 4, 16)
DIL_GROUPS = 3
DIL_HEADS = 16
DIL_HEAD_DIM = 128
DIL_WINDOW_KEYS = 128

REL_BUCKETS = 32
REL_MAX_DIST = 2048

MOE_GROUPS = 4
MOE_EXPERTS_PER_GROUP = 8
EXPERT_HIDDEN = 256

DN_ALPHA = (2 * DEPTH) ** 0.25

LANES = 128
VMEM_LIMIT_BYTES = 56 * 1024 * 1024
MATMUL_VMEM_BUDGET = 40 * 1024 * 1024
NEG_BIG = -1e30

BF16 = jnp.bfloat16
F32 = jnp.float32

_NT = (((1,), (1,)), ((), ()))


def _params(*sem):
    return pltpu.CompilerParams(dimension_semantics=sem, vmem_limit_bytes=VMEM_LIMIT_BYTES)


def _mm_kernel(x_ref, w_ref, o_ref, wbf_ref):
    @pl.when(pl.program_id(1) == 0)
    def _():
        wbf_ref[...] = w_ref[...].astype(BF16)

    o_ref[...] = jnp.dot(x_ref[...], wbf_ref[...], preferred_element_type=F32).astype(o_ref.dtype)


def _mm_tiles(m, k, n, out_bytes):
    for tm, tn in ((1024, 512), (512, 512), (512, 256), (256, 256), (256, 128), (128, 128)):
        if m % tm or n % tn:
            continue
        need = 2 * tm * k * 2 + 2 * k * tn * 4 + k * tn * 2 + 2 * tm * tn * out_bytes
        if need <= MATMUL_VMEM_BUDGET:
            return tm, tn
    raise ValueError(f"no matmul tiling for {(m, k, n)}")


def matmul(x, w, out_dtype, name):
    m, k = x.shape
    n = w.shape[1]
    tm, tn = _mm_tiles(m, k, n, jnp.dtype(out_dtype).itemsize)
    return pl.pallas_call(
        _mm_kernel,
        grid=(n // tn, m // tm),
        in_specs=[pl.BlockSpec((tm, k), lambda j, i: (i, 0)),
                  pl.BlockSpec((k, tn), lambda j, i: (0, j))],
        out_specs=pl.BlockSpec((tm, tn), lambda j, i: (i, j)),
        out_shape=jax.ShapeDtypeStruct((m, n), out_dtype),
        scratch_shapes=[pltpu.VMEM((k, tn), BF16)],
        compiler_params=_params("arbitrary", "arbitrary"),
        name=name,
    )(x, w)


def _ln_kernel(h_ref, mix_ref, g_ref, b_ref, of_ref, ob_ref):
    x = DN_ALPHA * h_ref[...] + mix_ref[...]
    mu = jnp.mean(x, axis=-1, keepdims=True)
    xc = x - mu
    var = jnp.mean(xc * xc, axis=-1, keepdims=True)
    y = xc * lax.rsqrt(var + LN_EPS) * g_ref[...] + b_ref[...]
    of_ref[...] = y
    ob_ref[...] = y.astype(BF16)


def residual_layer_norm(h, mix, g, b, name):
    s, d = h.shape
    tm = 256
    row = pl.BlockSpec((tm, d), lambda i: (i, 0))
    vec = pl.BlockSpec((1, d), lambda i: (0, 0))
    return pl.pallas_call(
        _ln_kernel,
        grid=(s // tm,),
        in_specs=[row, row, vec, vec],
        out_specs=[row, row],
        out_shape=[jax.ShapeDtypeStruct((s, d), F32), jax.ShapeDtypeStruct((s, d), BF16)],
        compiler_params=_params("arbitrary"),
        name=name,
    )(h, mix, g.reshape(1, d), b.reshape(1, d))


def _softplus(z):
    return jnp.maximum(z, 0.0) + jnp.log1p(jnp.exp(-jnp.abs(z)))


def _sb_kernel(q_ref, k_ref, v_ref, o_ref, acc_ref, carry_ref, *, tq, tk, scale):
    qi = pl.program_id(1)
    q = q_ref[...]
    acc_ref[...] = jnp.zeros_like(acc_ref)
    carry_ref[...] = jnp.zeros_like(carry_ref)
    r = lax.broadcasted_iota(jnp.int32, (tk, tk), 0)
    c = lax.broadcasted_iota(jnp.int32, (tk, tk), 1)
    later_mask = jnp.where(r > c, 1.0, 0.0).astype(BF16)

    def step(j, masked):
        ks = pl.multiple_of(j * tk, tk)
        k = k_ref[pl.ds(ks, tk), :]
        v = v_ref[pl.ds(ks, tk), :]
        z = lax.dot_general(q, k, _NT, preferred_element_type=F32) * scale
        sp = _softplus(z)
        log_keep = -sp
        if masked:
            qpos = qi * tq + lax.broadcasted_iota(jnp.int32, (tq, tk), 0)
            kpos = j * tk + lax.broadcasted_iota(jnp.int32, (tq, tk), 1)
            strict = kpos < qpos
            log_keep = jnp.where(strict, log_keep, 0.0)
        hi = log_keep.astype(BF16)
        lo = (log_keep - hi.astype(F32)).astype(BF16)
        later = (jnp.dot(hi, later_mask, preferred_element_type=F32)
                 + jnp.dot(lo, later_mask, preferred_element_type=F32))
        carry = carry_ref[...]
        tot = later + jnp.concatenate([carry] * (tk // LANES), axis=1)
        a = jnp.exp(z - sp + tot)
        if masked:
            a = jnp.where(strict, a, 0.0)
        acc_ref[...] += jnp.dot(a.astype(BF16), v, preferred_element_type=F32)
        carry_ref[...] = carry + jnp.sum(log_keep, axis=1, keepdims=True)

    n_diag = tq // tk
    n_full = qi * n_diag
    for t in range(n_diag):
        step(n_full + (n_diag - 1 - t), True)

    def body(t, _):
        step(n_full - 1 - t, False)
        return 0

    lax.fori_loop(0, n_full, body, 0)
    o_ref[...] = acc_ref[...].astype(o_ref.dtype)


def sb_attention(qkv, heads, name):
    s = qkv.shape[0]
    dh = SB_HEAD_DIM
    tq = min(512, s)
    tk = min(256, s)
    kern = functools.partial(_sb_kernel, tq=tq, tk=tk, scale=dh ** -0.5)
    return pl.pallas_call(
        kern,
        grid=(heads, s // tq),
        in_specs=[pl.BlockSpec((tq, dh), lambda h, i: (i, h)),
                  pl.BlockSpec((s, dh), lambda h, i: (0, heads + h)),
                  pl.BlockSpec((s, dh), lambda h, i: (0, 2 * heads + h))],
        out_specs=pl.BlockSpec((tq, dh), lambda h, i: (i, h)),
        out_shape=jax.ShapeDtypeStruct((s, heads * dh), BF16),
        scratch_shapes=[pltpu.VMEM((tq, dh), F32), pltpu.VMEM((tq, LANES), F32)],
        compiler_params=_params("arbitrary", "arbitrary"),
        name=name,
    )(qkv, qkv, qkv)


def _rope_lanes(x, cos, sin_up, sin_dn):
    half = MLA_ROPE_DIM // 2
    return (x * cos + pltpu.roll(x, half, 1) * sin_up + pltpu.roll(x, LANES - half, 1) * sin_dn)


def _mla_prep_kernel(a_ref, gq_ref, gkv_ref, cos_ref, sup_ref, sdn_ref, cq_ref, ckv_ref, kr_ref):
    a = a_ref[...]
    rq, rkv = MLA_Q_RANK, MLA_KV_RANK
    cq = a[:, :rq]
    cq_ref[...] = (cq * lax.rsqrt(jnp.mean(cq * cq, axis=-1, keepdims=True) + RMS_EPS)
                   * gq_ref[...]).astype(BF16)
    ckv = a[:, rq:rq + rkv]
    ckv_ref[...] = (ckv * lax.rsqrt(jnp.mean(ckv * ckv, axis=-1, keepdims=True) + RMS_EPS)
                    * gkv_ref[...]).astype(BF16)
    kr = a[:, rq + rkv:rq + rkv + LANES]
    kr_ref[...] = _rope_lanes(kr, cos_ref[...], sup_ref[...], sdn_ref[...]).astype(BF16)


def mla_prep(a, gq, gkv, cos, sin_up, sin_dn, name):
    s, n = a.shape
    tm = 512
    row = lambda w: pl.BlockSpec((tm, w), lambda i: (i, 0))
    vec = lambda w: pl.BlockSpec((1, w), lambda i: (0, 0))
    return pl.pallas_call(
        _mla_prep_kernel,
        grid=(s // tm,),
        in_specs=[row(n), vec(MLA_Q_RANK), vec(MLA_KV_RANK), row(LANES), row(LANES), row(LANES)],
        out_specs=[row(MLA_Q_RANK), row(MLA_KV_RANK), row(LANES)],
        out_shape=[jax.ShapeDtypeStruct((s, MLA_Q_RANK), BF16),
                   jax.ShapeDtypeStruct((s, MLA_KV_RANK), BF16),
                   jax.ShapeDtypeStruct((s, LANES), BF16)],
        compiler_params=_params("arbitrary"),
        name=name,
    )(a, gq.reshape(1, -1), gkv.reshape(1, -1), cos, sin_up, sin_dn)


def _mla_kernel(q_ref, kn_ref, kr_ref, v_ref, cos_ref, sup_ref, sdn_ref, o_ref,
                m_ref, l_ref, acc_ref, *, t, scale):
    qi = pl.program_id(1)
    q = q_ref[...]
    q_rope = _rope_lanes(q[:, LANES:], cos_ref[...], sup_ref[...], sdn_ref[...])
    qc = jnp.concatenate([q[:, :LANES].astype(BF16), q_rope.astype(BF16)], axis=1)
    m_ref[...] = jnp.full_like(m_ref, NEG_BIG)
    l_ref[...] = jnp.zeros_like(l_ref)
    acc_ref[...] = jnp.zeros_like(acc_ref)

    def step(j, masked):
        ks = pl.multiple_of(j * t, t)
        kc = jnp.concatenate([kn_ref[pl.ds(ks, t), :], kr_ref[pl.ds(ks, t), :]], axis=1)
        s = lax.dot_general(qc, kc, _NT, preferred_element_type=F32) * scale
        if masked:
            qpos = lax.broadcasted_iota(jnp.int32, (t, t), 0)
            kpos = lax.broadcasted_iota(jnp.int32, (t, t), 1)
            s = jnp.where(kpos <= qpos, s, NEG_BIG)
        m_prev = m_ref[...]
        m_new = jnp.maximum(m_prev, jnp.max(s, axis=1, keepdims=True))
        alpha = jnp.exp(m_prev - m_new)
        p = jnp.exp(s - m_new)
        l_ref[...] = alpha * l_ref[...] + jnp.sum(p, axis=1, keepdims=True)
        acc_ref[...] = alpha * acc_ref[...] + jnp.dot(p.astype(BF16), v_ref[pl.ds(ks, t), :],
                                                      preferred_element_type=F32)
        m_ref[...] = m_new

    def body(j, _):
        step(j, False)
        return 0

    lax.fori_loop(0, qi, body, 0)
    step(qi, True)
    o_ref[...] = (acc_ref[...] / l_ref[...]).astype(o_ref.dtype)


def mla_attention_core(q, kv, k_rope, cos, sin_up, sin_dn, heads, name):
    s = q.shape[0]
    t = min(512, s)
    kern = functools.partial(_mla_kernel, t=t, scale=(MLA_NOPE_DIM + MLA_ROPE_DIM) ** -0.5)
    tab = pl.BlockSpec((t, LANES), lambda h, i: (i, 0))
    return pl.pallas_call(
        kern,
        grid=(heads, s // t),
        in_specs=[pl.BlockSpec((t, 2 * LANES), lambda h, i: (i, h)),
                  pl.BlockSpec((s, LANES), lambda h, i: (0, 2 * h)),
                  pl.BlockSpec((s, LANES), lambda h, i: (0, 0)),
                  pl.BlockSpec((s, LANES), lambda h, i: (0, 2 * h + 1)),
                  tab, tab, tab],
        out_specs=pl.BlockSpec((t, MLA_V_DIM), lambda h, i: (i, h)),
        out_shape=jax.ShapeDtypeStruct((s, heads * MLA_V_DIM), BF16),
        scratch_shapes=[pltpu.VMEM((t, 1), F32), pltpu.VMEM((t, 1), F32), pltpu.VMEM((t, MLA_V_DIM), F32)],
        compiler_params=_params("arbitrary", "arbitrary"),
        name=name,
    )(q, kv, k_rope, kv, cos, sin_up, sin_dn)


def _t5_bucket(rel):
    n = jnp.maximum(rel, 0)
    max_exact = REL_BUCKETS // 2
    nf = jnp.maximum(n, 1).astype(F32)
    large = max_exact + (jnp.log(nf / max_exact) / math.log(REL_MAX_DIST / max_exact)
                         * (REL_BUCKETS - max_exact)).astype(jnp.int32)
    large = jnp.minimum(large, REL_BUCKETS - 1)
    return jnp.where(n < max_exact, n, large)


def _dil_kernel(tab_ref, q_ref, kp_ref, kc_ref, vp_ref, vc_ref, pq_ref, pkp_ref, pkc_ref,
                o_ref, lse_ref, *, heads, scale):
    a = pl.program_id(1)
    w = DIL_WINDOW_KEYS
    dh = DIL_HEAD_DIM
    rel = pq_ref[...] - jnp.concatenate([pkp_ref[...], pkc_ref[...]], axis=1)
    back = (lax.broadcasted_iota(jnp.int32, (w, 2 * w), 0) + w
            - lax.broadcasted_iota(jnp.int32, (w, 2 * w), 1))
    col = lax.broadcasted_iota(jnp.int32, (w, 2 * w), 1)
    first_col = jnp.where(a > 0, 0, w)
    valid = (back >= 0) & (back <= w) & (col >= first_col)
    bucket = _t5_bucket(rel)
    for h in range(heads):
        hs = slice(h * dh, (h + 1) * dh)
        bias = jnp.zeros((w, 2 * w), F32)
        for b in range(REL_BUCKETS):
            bias = jnp.where(bucket == b, tab_ref[b, h], bias)
        k = jnp.concatenate([kp_ref[:, hs], kc_ref[:, hs]], axis=0)
        v = jnp.concatenate([vp_ref[:, hs], vc_ref[:, hs]], axis=0)
        s = lax.dot_general(q_ref[:, hs], k, _NT, preferred_element_type=F32) * scale + bias
        s = jnp.where(valid, s, NEG_BIG)
        m = jnp.max(s, axis=1, keepdims=True)
        p = jnp.exp(s - m)
        l = jnp.sum(p, axis=1, keepdims=True)
        o_ref[:, hs] = jnp.dot(p.astype(BF16), v, preferred_element_type=F32) / l
        lse_ref[:, hs] = jnp.broadcast_to(m + jnp.log(l), (w, dh))


def dilated_group(qkv, pos, tab, g, name):
    s, n = qkv.shape
    d = DIL_DILATIONS[g]
    heads, dh, w = DIL_HEADS, DIL_HEAD_DIM, DIL_WINDOW_KEYS
    hw = heads * dh
    per_row = n // hw
    sub = s // d
    qkv_v = qkv.reshape(sub, d * n)
    pos_col = pos.reshape(sub, d).T.reshape(d, sub, 1)
    pos_row = pos.reshape(sub, d).T.reshape(d, 1, sub)
    G = DIL_GROUPS
    prev = lambda i: jnp.maximum(i - 1, 0)
    kern = functools.partial(_dil_kernel, heads=heads, scale=dh ** -0.5)
    o, lse = pl.pallas_call(
        kern,
        grid=(d, sub // w),
        in_specs=[pl.BlockSpec(memory_space=pltpu.SMEM),
                  pl.BlockSpec((w, hw), lambda r, i: (i, r * per_row + g)),
                  pl.BlockSpec((w, hw), lambda r, i: (prev(i), r * per_row + G + g)),
                  pl.BlockSpec((w, hw), lambda r, i: (i, r * per_row + G + g)),
                  pl.BlockSpec((w, hw), lambda r, i: (prev(i), r * per_row + 2 * G + g)),
                  pl.BlockSpec((w, hw), lambda r, i: (i, r * per_row + 2 * G + g)),
                  pl.BlockSpec((None, w, 1), lambda r, i: (r, i, 0)),
                  pl.BlockSpec((None, 1, w), lambda r, i: (r, 0, prev(i))),
                  pl.BlockSpec((None, 1, w), lambda r, i: (r, 0, i))],
        out_specs=[pl.BlockSpec((w, hw), lambda r, i: (i, r)),
                   pl.BlockSpec((w, hw), lambda r, i: (i, r))],
        out_shape=[jax.ShapeDtypeStruct((sub, d * hw), F32), jax.ShapeDtypeStruct((sub, d * hw), F32)],
        compiler_params=_params("arbitrary", "arbitrary"),
        name=name,
    )(tab, qkv_v, qkv_v, qkv_v, qkv_v, qkv_v, pos_col, pos_row, pos_row)
    return o.reshape(s, hw), lse.reshape(s, hw)


def _dil_mix_kernel(o0, l0, o1, l1, o2, l2, out_ref):
    a, b, c = l0[...], l1[...], l2[...]
    m = jnp.maximum(jnp.maximum(a, b), c)
    ea, eb, ec = jnp.exp(a - m), jnp.exp(b - m), jnp.exp(c - m)
    out = (ea * o0[...] + eb * o1[...] + ec * o2[...]) / (ea + eb + ec)
    out_ref[...] = out.astype(out_ref.dtype)


def dilated_mix(parts, name):
    s, hw = parts[0][0].shape
    tm = 256
    blk = pl.BlockSpec((tm, hw), lambda i: (i, 0))
    flat = [x for pair in parts for x in pair]
    return pl.pallas_call(
        _dil_mix_kernel,
        grid=(s // tm,),
        in_specs=[blk] * 6,
        out_specs=blk,
        out_shape=jax.ShapeDtypeStruct((s, hw), BF16),
        compiler_params=_params("arbitrary"),
        name=name,
    )(*flat)


def _router_kernel(x_ref, w_ref, b_ref, gate_ref):
    ng, ne = MOE_GROUPS, MOE_EXPERTS_PER_GROUP
    logits = jnp.dot(x_ref[...], w_ref[...], preferred_element_type=F32,
                     precision=lax.Precision.HIGHEST) + b_ref[...]
    lane = lax.broadcasted_iota(jnp.int32, logits.shape, 1).astype(F32)
    big = float(1 << 20)
    is_group = (lane >= ng * ne) & (lane < ng * ne + ng)
    g_logit = jnp.where(is_group, logits, NEG_BIG)
    g_max = jnp.max(g_logit, axis=1, keepdims=True)
    g_top = jnp.min(jnp.where(is_group & (g_logit == g_max), lane, big), axis=1, keepdims=True) - ng * ne
    g_gate = 1.0 / jnp.sum(jnp.where(is_group, jnp.exp(g_logit - g_max), 0.0), axis=1, keepdims=True)
    in_group = (lane >= g_top * ne) & (lane < (g_top + 1) * ne)
    e_logit = jnp.where(in_group, logits, NEG_BIG)
    v1 = jnp.max(e_logit, axis=1, keepdims=True)
    i1 = jnp.min(jnp.where(in_group & (e_logit == v1), lane, big), axis=1, keepdims=True)
    rest = in_group & (lane != i1)
    e_rest = jnp.where(rest, logits, NEG_BIG)
    v2 = jnp.max(e_rest, axis=1, keepdims=True)
    i2 = jnp.min(jnp.where(rest & (e_rest == v2), lane, big), axis=1, keepdims=True)
    e21 = jnp.exp(v2 - v1)
    w1 = g_gate / (1.0 + e21)
    w2 = g_gate * e21 / (1.0 + e21)
    gate_ref[...] = jnp.where(lane == i1, w1, jnp.where(lane == i2, w2, 0.0))


def moe_gates(h, w_router, b_router, name):
    s, d = h.shape
    tm = 256
    return pl.pallas_call(
        _router_kernel,
        grid=(s // tm,),
        in_specs=[pl.BlockSpec((tm, d), lambda i: (i, 0)),
                  pl.BlockSpec((d, LANES), lambda i: (0, 0)),
                  pl.BlockSpec((1, LANES), lambda i: (0, 0))],
        out_specs=pl.BlockSpec((tm, LANES), lambda i: (i, 0)),
        out_shape=jax.ShapeDtypeStruct((s, LANES), F32),
        compiler_params=_params("arbitrary"),
        name=name,
    )(h, w_router, b_router)


def _expert_up_kernel(x_ref, wg_ref, wu_ref, gate_ref, o_ref, wg_bf, wu_bf):
    e = pl.program_id(0)

    @pl.when(pl.program_id(1) == 0)
    def _():
        wg_bf[...] = wg_ref[...].astype(BF16)
        wu_bf[...] = wu_ref[...].astype(BF16)

    x = x_ref[...]
    gate = jnp.dot(x, wg_bf[...], preferred_element_type=F32)
    up = jnp.dot(x, wu_bf[...], preferred_element_type=F32)
    lane = lax.broadcasted_iota(jnp.int32, gate_ref.shape, 1)
    comb = jnp.sum(jnp.where(lane == e, gate_ref[...], 0.0), axis=1, keepdims=True)
    hidden = gate * jax.nn.sigmoid(gate) * up
    o_ref[...] = (hidden * comb).astype(o_ref.dtype)


def expert_up(x, w_gate, w_up, gates, name):
    s, d = x.shape
    ne, _, f = w_gate.shape
    tm = 1024
    return pl.pallas_call(
        _expert_up_kernel,
        grid=(ne, s // tm),
        in_specs=[pl.BlockSpec((tm, d), lambda e, i: (i, 0)),
                  pl.BlockSpec((None, d, f), lambda e, i: (e, 0, 0)),
                  pl.BlockSpec((None, d, f), lambda e, i: (e, 0, 0)),
                  pl.BlockSpec((tm, LANES), lambda e, i: (i, 0))],
        out_specs=pl.BlockSpec((tm, f), lambda e, i: (i, e)),
        out_shape=jax.ShapeDtypeStruct((s, ne * f), BF16),
        scratch_shapes=[pltpu.VMEM((d, f), BF16), pltpu.VMEM((d, f), BF16)],
        compiler_params=_params("arbitrary", "arbitrary"),
        name=name,
    )(x, w_gate, w_up, gates)


def hierarchical_moe(h_f32, h_bf16, w_gr, b_gr, w_er, b_er, w_gate, w_up, w_down, tag):
    d = h_f32.shape[1]
    ng, ne = MOE_GROUPS, MOE_EXPERTS_PER_GROUP
    w_router = jnp.concatenate(
        [jnp.transpose(w_er, (1, 0, 2)).reshape(d, ng * ne), w_gr,
         jnp.zeros((d, LANES - ng * ne - ng), F32)], axis=1)
    b_router = jnp.concatenate(
        [b_er.reshape(ng * ne), b_gr, jnp.zeros((LANES - ng * ne - ng,), F32)]).reshape(1, LANES)
    gates = moe_gates(h_f32, w_router, b_router, f"moe_router_{tag}")
    hidden = expert_up(h_bf16, w_gate, w_up, gates, f"moe_up_{tag}")
    return matmul(hidden, w_down.reshape(-1, d), F32, f"moe_down_{tag}")


def stick_breaking_mixer(h_bf16, w_qkv, w_o, tag):
    qkv = matmul(h_bf16, w_qkv, BF16, f"sb_qkv_{tag}")
    o = sb_attention(qkv, SB_HEADS, f"sb_attn_{tag}")
    return matmul(o, w_o, F32, f"sb_out_{tag}")


def _rope_tables(pos):
    half = MLA_ROPE_DIM // 2
    inv_freq = ROPE_THETA ** (-jnp.arange(half, dtype=F32) / half)
    ang = pos.astype(F32)[:, None] * inv_freq
    cos, sin = jnp.cos(ang), jnp.sin(ang)
    zeros = jnp.zeros((pos.shape[0], LANES - 2 * half), F32)
    z_half = jnp.zeros_like(sin)
    cos_t = jnp.concatenate([cos, cos, zeros], axis=1)
    sin_up = jnp.concatenate([z_half, sin, zeros], axis=1)
    sin_dn = jnp.concatenate([-sin, z_half, zeros], axis=1)
    return cos_t, sin_up, sin_dn


def mla_mixer(h_bf16, pos, w_q_a, q_a_norm, w_q_b, w_kv_a, kv_a_norm, w_kv_b, w_o, tag):
    d = h_bf16.shape[1]
    heads = MLA_HEADS
    nope, rope = MLA_NOPE_DIM, MLA_ROPE_DIM
    used = MLA_Q_RANK + MLA_KV_RANK + rope
    width = -(-(MLA_Q_RANK + MLA_KV_RANK + LANES) // 512) * 512
    w_a = jnp.concatenate([w_q_a, w_kv_a, jnp.zeros((d, width - used), F32)], axis=1)
    a = matmul(h_bf16, w_a, F32, f"mla_a_{tag}")
    cos_t, sin_up, sin_dn = _rope_tables(pos)
    cq, ckv, k_rope = mla_prep(a, q_a_norm, kv_a_norm, cos_t, sin_up, sin_dn, f"mla_prep_{tag}")
    w_qb = jnp.pad(w_q_b.reshape(MLA_Q_RANK, heads, nope + rope),
                   ((0, 0), (0, 0), (0, 2 * LANES - nope - rope))).reshape(MLA_Q_RANK, heads * 2 * LANES)
    q = matmul(cq, w_qb, F32, f"mla_qb_{tag}")
    kv = matmul(ckv, w_kv_b, BF16, f"mla_kvb_{tag}")
    o = mla_attention_core(q, kv, k_rope, cos_t, sin_up, sin_dn, heads, f"mla_attn_{tag}")
    return matmul(o, w_o, F32, f"mla_out_{tag}")


def dilated_mixer(h_bf16, pos, rel_bias, w_qkv, w_o, tag):
    qkv = matmul(h_bf16, w_qkv, BF16, f"dil_qkv_{tag}")
    tabs = rel_bias.reshape(REL_BUCKETS, DIL_GROUPS, DIL_HEADS)
    parts = [dilated_group(qkv, pos, tabs[:, g, :], g, f"dil_attn_{tag}_g{g}") for g in range(DIL_GROUPS)]
    o = dilated_mix(parts, f"dil_mix_{tag}")
    return matmul(o, w_o, F32, f"dil_out_{tag}")


def kernel(x, positions, rel_bias, sb_w_qkv, sb_w_o, mla_w_q_a, mla_q_a_norm, mla_w_q_b, mla_w_kv_a,
           mla_kv_a_norm, mla_w_kv_b, mla_w_o, dil_w_qkv, dil_w_o, ln_gain, ln_bias,
           moe_w_group_router, moe_b_group_router, moe_w_expert_router, moe_b_expert_router,
           moe_w_gate, moe_w_up, moe_w_down):
    batch, seq, d = x.shape
    outs = []
    for b in range(batch):
        h = x[b]
        h_bf16 = h.astype(BF16)
        pos = positions[b]
        for i in range(DEPTH):
            kind, j = i % N_MIXERS, i // N_MIXERS
            tag = f"l{i}"
            if kind == 0:
                mix = stick_breaking_mixer(h_bf16, sb_w_qkv[j], sb_w_o[j], tag)
            elif kind == 1:
                mix = mla_mixer(h_bf16, pos, mla_w_q_a[j], mla_q_a_norm[j], mla_w_q_b[j], mla_w_kv_a[j],
                                mla_kv_a_norm[j], mla_w_kv_b[j], mla_w_o[j], tag)
            else:
                mix = dilated_mixer(h_bf16, pos, rel_bias, dil_w_qkv[j], dil_w_o[j], tag)
            h, h_bf16 = residual_layer_norm(h, mix, ln_gain[i, 0], ln_bias[i, 0], f"ln_mix_{tag}")
            ffn = hierarchical_moe(h, h_bf16, moe_w_group_router[i], moe_b_group_router[i],
                                   moe_w_expert_router[i], moe_b_expert_router[i],
                                   moe_w_gate[i], moe_w_up[i], moe_w_down[i], tag)
            h, h_bf16 = residual_layer_norm(h, ffn, ln_gain[i, 1], ln_bias[i, 1], f"ln_ffn_{tag}")
        outs.append(h)
    return jnp.stack(outs)
```

```python
import functools
import math

import jax
import jax.numpy as jnp
from jax import lax
from jax.experimental import pallas as pl
from jax.experimental.pallas import tpu as pltpu

DEPTH = 4
N_MIXERS = 3
LN_EPS = 1e-5
RMS_EPS = 1e-6

SB_HEADS = 32
SB_HEAD_DIM = 128

MLA_HEADS = 32
MLA_Q_RANK = 1024
MLA_KV_RANK = 512
MLA_NOPE_DIM = 128
MLA_ROPE_DIM = 64
MLA_V_DIM = 128
ROPE_THETA = 10000.0

DIL_DILATIONS = (1, 4, 16)
DIL_GROUPS = 3
DIL_HEADS = 16
DIL_HEAD_DIM = 128
DIL_WINDOW_KEYS = 128

REL_BUCKETS = 32
REL_MAX_DIST = 2048

MOE_GROUPS = 4
MOE_EXPERTS_PER_GROUP = 8
EXPERT_HIDDEN = 256

DN_ALPHA = (2 * DEPTH) ** 0.25

LANES = 128
VMEM_LIMIT_BYTES = 56 * 1024 * 1024
MATMUL_VMEM_BUDGET = 40 * 1024 * 1024
NEG_BIG = -1e30
EXP2_UNDERFLOW = -151.0
SB_HEADS_PER_STEP = 2
MLA_HEADS_PER_STEP = 2

BF16 = jnp.bfloat16
F32 = jnp.float32

_NT = (((1,), (1,)), ((), ()))


def _params(*sem):
    return pltpu.CompilerParams(dimension_semantics=sem, vmem_limit_bytes=VMEM_LIMIT_BYTES)


def _mm_kernel(x_ref, w_ref, o_ref, wbf_ref):
    @pl.when(pl.program_id(1) == 0)
    def _():
        wbf_ref[...] = w_ref[...].astype(BF16)

    o_ref[...] = jnp.dot(x_ref[...], wbf_ref[...], preferred_element_type=F32).astype(o_ref.dtype)


def _mm_tiles(m, k, n, out_bytes):
    for tm, tn in ((1024, 512), (512, 512), (512, 256), (256, 256), (256, 128), (128, 128)):
        if m % tm or n % tn:
            continue
        need = 2 * tm * k * 2 + 2 * k * tn * 4 + k * tn * 2 + 2 * tm * tn * out_bytes
        if need <= MATMUL_VMEM_BUDGET:
            return tm, tn
    raise ValueError(f"no matmul tiling for {(m, k, n)}")


def matmul(x, w, out_dtype, name, lead=()):
    m, k = x.shape
    n = w.shape[-1]
    tm, tn = _mm_tiles(m, k, n, jnp.dtype(out_dtype).itemsize)
    return pl.pallas_call(
        _mm_kernel,
        grid=(n // tn, m // tm),
        in_specs=[pl.BlockSpec((tm, k), lambda j, i: (i, 0)),
                  pl.BlockSpec((None,) * len(lead) + (k, tn), lambda j, i: tuple(lead) + (0, j))],
        out_specs=pl.BlockSpec((tm, tn), lambda j, i: (i, j)),
        out_shape=jax.ShapeDtypeStruct((m, n), out_dtype),
        scratch_shapes=[pltpu.VMEM((k, tn), BF16)],
        compiler_params=_params("arbitrary", "arbitrary"),
        name=name,
    )(x, w)


def _ln_kernel(h_ref, mix_ref, g_ref, b_ref, of_ref, ob_ref):
    x = DN_ALPHA * h_ref[...] + mix_ref[...]
    mu = jnp.mean(x, axis=-1, keepdims=True)
    xc = x - mu
    var = jnp.mean(xc * xc, axis=-1, keepdims=True)
    y = xc * lax.rsqrt(var + LN_EPS) * g_ref[...] + b_ref[...]
    of_ref[...] = y
    ob_ref[...] = y.astype(BF16)


def residual_layer_norm(h, mix, g, b, name):
    s, d = h.shape
    tm = 256
    row = pl.BlockSpec((tm, d), lambda i: (i, 0))
    vec = pl.BlockSpec((1, d), lambda i: (0, 0))
    return pl.pallas_call(
        _ln_kernel,
        grid=(s // tm,),
        in_specs=[row, row, vec, vec],
        out_specs=[row, row],
        out_shape=[jax.ShapeDtypeStruct((s, d), F32), jax.ShapeDtypeStruct((s, d), BF16)],
        compiler_params=_params("arbitrary"),
        name=name,
    )(h, mix, g.reshape(1, d), b.reshape(1, d))


def _sb_kernel(q_ref, k_ref, v_ref, o_ref, acc_ref, carry_ref, *, t, hps, c2):
    qi = pl.program_id(1)
    dh = SB_HEAD_DIM
    heads = [slice(c * dh, (c + 1) * dh) for c in range(hps)]
    q_neg = [-q_ref[:, hs] for hs in heads]
    acc_ref[...] = jnp.zeros_like(acc_ref)
    carry_ref[...] = jnp.zeros_like(carry_ref)
    r = lax.broadcasted_iota(jnp.int32, (t, t), 0)
    c = lax.broadcasted_iota(jnp.int32, (t, t), 1)
    strict = c < r
    later_mask = jnp.where(r > c, 1.0, 0.0).astype(BF16)
    later_mask2 = jnp.concatenate([later_mask, later_mask], axis=0)

    def step(j, masked):
        ks = pl.multiple_of(j * t, t)
        for ci, hs in enumerate(heads):
            k = k_ref[pl.ds(ks, t), hs]
            v = v_ref[pl.ds(ks, t), hs]
            nz = lax.dot_general(q_neg[ci], k, _NT, preferred_element_type=F32) * c2
            log_keep = jnp.minimum(nz, 0.0) - jnp.log2(1.0 + jnp.exp2(-jnp.abs(nz)))
            if masked:
                log_keep = jnp.where(strict, log_keep, 0.0)
            hi = pltpu.bitcast(pltpu.bitcast(log_keep, jnp.uint32) & jnp.uint32(0xFFFF0000), F32)
            hi_lo = jnp.concatenate([hi.astype(BF16), (log_keep - hi).astype(BF16)], axis=1)
            later = jnp.dot(hi_lo, later_mask2, preferred_element_type=F32)
            carry = carry_ref[ci]
            tot = later + jnp.concatenate([carry] * (t // LANES), axis=1)
            a = jnp.exp2(log_keep - nz + tot)
            if masked:
                a = jnp.where(strict, a, 0.0)
            acc_ref[ci] += jnp.dot(a.astype(BF16), v, preferred_element_type=F32)
            carry_ref[ci] = carry + jnp.sum(log_keep, axis=1, keepdims=True)

    def largest_carry():
        m = carry_ref[0]
        for ci in range(1, hps):
            m = jnp.maximum(m, carry_ref[ci])
        return jnp.max(m)

    step(qi, True)

    def cond(state):
        j, top = state
        return jnp.logical_and(j >= 0, top >= EXP2_UNDERFLOW)

    def body(state):
        j, _ = state
        step(j, False)
        return j - 1, largest_carry()

    lax.while_loop(cond, body, (qi - 1, largest_carry()))
    for ci, hs in enumerate(heads):
        o_ref[:, hs] = acc_ref[ci].astype(o_ref.dtype)


def sb_attention(qkv, heads, name):
    s = qkv.shape[0]
    dh = SB_HEAD_DIM
    t = min(256, s)
    hps = SB_HEADS_PER_STEP
    groups = heads // hps
    kern = functools.partial(_sb_kernel, t=t, hps=hps, c2=dh ** -0.5 / math.log(2.0))
    return pl.pallas_call(
        kern,
        grid=(groups, s // t),
        in_specs=[pl.BlockSpec((t, hps * dh), lambda h, i: (i, h)),
                  pl.BlockSpec((s, hps * dh), lambda h, i: (0, groups + h)),
                  pl.BlockSpec((s, hps * dh), lambda h, i: (0, 2 * groups + h))],
        out_specs=pl.BlockSpec((t, hps * dh), lambda h, i: (i, h)),
        out_shape=jax.ShapeDtypeStruct((s, heads * dh), BF16),
        scratch_shapes=[pltpu.VMEM((hps, t, dh), F32), pltpu.VMEM((hps, t, LANES), F32)],
        compiler_params=_params("arbitrary", "arbitrary"),
        name=name,
    )(qkv, qkv, qkv)


def _rope_lanes(x, cos, sin_up, sin_dn):
    half = MLA_ROPE_DIM // 2
    return (x * cos + pltpu.roll(x, half, 1) * sin_up + pltpu.roll(x, LANES - half, 1) * sin_dn)


def _mla_prep_kernel(a_ref, gq_ref, gkv_ref, cos_ref, sup_ref, sdn_ref, cq_ref, ckv_ref, kr_ref):
    a = a_ref[...]
    rq, rkv = MLA_Q_RANK, MLA_KV_RANK
    cq = a[:, :rq]
    cq_ref[...] = (cq * lax.rsqrt(jnp.mean(cq * cq, axis=-1, keepdims=True) + RMS_EPS)
                   * gq_ref[...]).astype(BF16)
    ckv = a[:, rq:rq + rkv]
    ckv_ref[...] = (ckv * lax.rsqrt(jnp.mean(ckv * ckv, axis=-1, keepdims=True) + RMS_EPS)
                    * gkv_ref[...]).astype(BF16)
    kr = a[:, rq + rkv:rq + rkv + LANES]
    kr_ref[...] = _rope_lanes(kr, cos_ref[...], sup_ref[...], sdn_ref[...]).astype(BF16)


def mla_prep(a, gq, gkv, cos, sin_up, sin_dn, name):
    s, n = a.shape
    tm = 512
    row = lambda w: pl.BlockSpec((tm, w), lambda i: (i, 0))
    vec = lambda w: pl.BlockSpec((1, w), lambda i: (0, 0))
    return pl.pallas_call(
        _mla_prep_kernel,
        grid=(s // tm,),
        in_specs=[row(n), vec(MLA_Q_RANK), vec(MLA_KV_RANK), row(LANES), row(LANES), row(LANES)],
        out_specs=[row(MLA_Q_RANK), row(MLA_KV_RANK), row(LANES)],
        out_shape=[jax.ShapeDtypeStruct((s, MLA_Q_RANK), BF16),
                   jax.ShapeDtypeStruct((s, MLA_KV_RANK), BF16),
                   jax.ShapeDtypeStruct((s, LANES), BF16)],
        compiler_params=_params("arbitrary"),
        name=name,
    )(a, gq.reshape(1, -1), gkv.reshape(1, -1), cos, sin_up, sin_dn)


def _mla_kernel(q_ref, kv_ref, kr_ref, cos_ref, sup_ref, sdn_ref, o_ref,
                m_ref, l_ref, acc_ref, *, t, hps, c2):
    qi = pl.program_id(1)
    wide = 2 * LANES
    cos, sin_up, sin_dn = cos_ref[...], sup_ref[...], sdn_ref[...]
    qc = []
    for ci in range(hps):
        q = q_ref[:, ci * wide:(ci + 1) * wide]
        q_rope = _rope_lanes(q[:, LANES:], cos, sin_up, sin_dn)
        qc.append(jnp.concatenate([q[:, :LANES].astype(BF16), q_rope.astype(BF16)], axis=1))
    m_ref[...] = jnp.full_like(m_ref, NEG_BIG)
    l_ref[...] = jnp.zeros_like(l_ref)
    acc_ref[...] = jnp.zeros_like(acc_ref)
    causal = (lax.broadcasted_iota(jnp.int32, (t, t), 1) <= lax.broadcasted_iota(jnp.int32, (t, t), 0))

    def step(j, masked):
        ks = pl.multiple_of(j * t, t)
        kr = kr_ref[pl.ds(ks, t), :]
        for ci in range(hps):
            kn = kv_ref[pl.ds(ks, t), ci * wide:ci * wide + LANES]
            v = kv_ref[pl.ds(ks, t), ci * wide + LANES:(ci + 1) * wide]
            s = lax.dot_general(qc[ci], jnp.concatenate([kn, kr], axis=1), _NT,
                                preferred_element_type=F32) * c2
            if masked:
                s = jnp.where(causal, s, NEG_BIG)
            m_prev = m_ref[ci]
            m_new = jnp.maximum(m_prev, jnp.max(s, axis=1, keepdims=True))
            alpha = jnp.exp2(m_prev - m_new)
            p = jnp.exp2(s - jnp.concatenate([m_new] * (t // LANES), axis=1))
            l_ref[ci] = alpha * l_ref[ci] + jnp.sum(p, axis=1, keepdims=True)
            acc_ref[ci] = alpha * acc_ref[ci] + jnp.dot(p.astype(BF16), v, preferred_element_type=F32)
            m_ref[ci] = m_new

    def body(j, _):
        step(j, False)
        return 0

    lax.fori_loop(0, qi, body, 0)
    step(qi, True)
    for ci in range(hps):
        o_ref[:, ci * LANES:(ci + 1) * LANES] = (acc_ref[ci] / l_ref[ci]).astype(o_ref.dtype)


def mla_attention_core(q, kv, k_rope, cos, sin_up, sin_dn, heads, name):
    assert MLA_V_DIM == LANES and MLA_NOPE_DIM == LANES
    s = q.shape[0]
    t = min(512, s)
    hps = MLA_HEADS_PER_STEP
    kern = functools.partial(_mla_kernel, t=t, hps=hps,
                             c2=(MLA_NOPE_DIM + MLA_ROPE_DIM) ** -0.5 / math.log(2.0))
    tab = pl.BlockSpec((t, LANES), lambda h, i: (i, 0))
    return pl.pallas_call(
        kern,
        grid=(heads // hps, s // t),
        in_specs=[pl.BlockSpec((t, hps * 2 * LANES), lambda h, i: (i, h)),
                  pl.BlockSpec((s, hps * 2 * LANES), lambda h, i: (0, h)),
                  pl.BlockSpec((s, LANES), lambda h, i: (0, 0)),
                  tab, tab, tab],
        out_specs=pl.BlockSpec((t, hps * LANES), lambda h, i: (i, h)),
        out_shape=jax.ShapeDtypeStruct((s, heads * MLA_V_DIM), BF16),
        scratch_shapes=[pltpu.VMEM((hps, t, LANES), F32), pltpu.VMEM((hps, t, LANES), F32),
                        pltpu.VMEM((hps, t, LANES), F32)],
        compiler_params=_params("arbitrary", "arbitrary"),
        name=name,
    )(q, kv, k_rope, cos, sin_up, sin_dn)


def _t5_bucket(rel):
    n = jnp.maximum(rel, 0)
    max_exact = REL_BUCKETS // 2
    nf = jnp.maximum(n, 1).astype(F32)
    large = max_exact + (jnp.log(nf / max_exact) / math.log(REL_MAX_DIST / max_exact)
                         * (REL_BUCKETS - max_exact)).astype(jnp.int32)
    large = jnp.minimum(large, REL_BUCKETS - 1)
    return jnp.where(n < max_exact, n, large)


def _dil_kernel(tab_ref, q_ref, kp_ref, kc_ref, vp_ref, vc_ref, pq_ref, pkp_ref, pkc_ref,
                o_ref, lse_ref, *, heads, scale):
    a = pl.program_id(1)
    w = DIL_WINDOW_KEYS
    dh = DIL_HEAD_DIM
    rel = pq_ref[...] - jnp.concatenate([pkp_ref[...], pkc_ref[...]], axis=1)
    back = (lax.broadcasted_iota(jnp.int32, (w, 2 * w), 0) + w
            - lax.broadcasted_iota(jnp.int32, (w, 2 * w), 1))
    col = lax.broadcasted_iota(jnp.int32, (w, 2 * w), 1)
    first_col = jnp.where(a > 0, 0, w)
    valid = (back >= 0) & (back <= w) & (col >= first_col)
    bucket = _t5_bucket(rel)
    for h in range(heads):
        hs = slice(h * dh, (h + 1) * dh)
        bias = jnp.zeros((w, 2 * w), F32)
        for b in range(REL_BUCKETS):
            bias = jnp.where(bucket == b, tab_ref[b, h], bias)
        k = jnp.concatenate([kp_ref[:, hs], kc_ref[:, hs]], axis=0)
        v = jnp.concatenate([vp_ref[:, hs], vc_ref[:, hs]], axis=0)
        s = lax.dot_general(q_ref[:, hs], k, _NT, preferred_element_type=F32) * scale + bias
        s = jnp.where(valid, s, NEG_BIG)
        m = jnp.max(s, axis=1, keepdims=True)
        p = jnp.exp(s - m)
        l = jnp.sum(p, axis=1, keepdims=True)
        o_ref[:, hs] = jnp.dot(p.astype(BF16), v, preferred_element_type=F32) / l
        lse_ref[:, hs] = jnp.broadcast_to(m + jnp.log(l), (w, dh))


def dilated_group(qkv, pos, tab, g, name):
    s, n = qkv.shape
    d = DIL_DILATIONS[g]
    heads, dh, w = DIL_HEADS, DIL_HEAD_DIM, DIL_WINDOW_KEYS
    hw = heads * dh
    per_row = n // hw
    sub = s // d
    qkv_v = qkv.reshape(sub, d * n)
    pos_col = pos.reshape(sub, d).T.reshape(d, sub, 1)
    pos_row = pos.reshape(sub, d).T.reshape(d, 1, sub)
    G = DIL_GROUPS
    prev = lambda i: jnp.maximum(i - 1, 0)
    kern = functools.partial(_dil_kernel, heads=heads, scale=dh ** -0.5)
    o, lse = pl.pallas_call(
        kern,
        grid=(d, sub // w),
        in_specs=[pl.BlockSpec(memory_space=pltpu.SMEM),
                  pl.BlockSpec((w, hw), lambda r, i: (i, r * per_row + g)),
                  pl.BlockSpec((w, hw), lambda r, i: (prev(i), r * per_row + G + g)),
                  pl.BlockSpec((w, hw), lambda r, i: (i, r * per_row + G + g)),
                  pl.BlockSpec((w, hw), lambda r, i: (prev(i), r * per_row + 2 * G + g)),
                  pl.BlockSpec((w, hw), lambda r, i: (i, r * per_row + 2 * G + g)),
                  pl.BlockSpec((None, w, 1), lambda r, i: (r, i, 0)),
                  pl.BlockSpec((None, 1, w), lambda r, i: (r, 0, prev(i))),
                  pl.BlockSpec((None, 1, w), lambda r, i: (r, 0, i))],
        out_specs=[pl.BlockSpec((w, hw), lambda r, i: (i, r)),
                   pl.BlockSpec((w, hw), lambda r, i: (i, r))],
        out_shape=[jax.ShapeDtypeStruct((sub, d * hw), F32), jax.ShapeDtypeStruct((sub, d * hw), F32)],
        compiler_params=_params("arbitrary", "arbitrary"),
        name=name,
    )(tab, qkv_v, qkv_v, qkv_v, qkv_v, qkv_v, pos_col, pos_row, pos_row)
    return o.reshape(s, hw), lse.reshape(s, hw)


def _dil_mix_kernel(o0, l0, o1, l1, o2, l2, out_ref):
    a, b, c = l0[...], l1[...], l2[...]
    m = jnp.maximum(jnp.maximum(a, b), c)
    ea, eb, ec = jnp.exp(a - m), jnp.exp(b - m), jnp.exp(c - m)
    out = (ea * o0[...] + eb * o1[...] + ec * o2[...]) / (ea + eb + ec)
    out_ref[...] = out.astype(out_ref.dtype)


def dilated_mix(parts, name):
    s, hw = parts[0][0].shape
    tm = 256
    blk = pl.BlockSpec((tm, hw), lambda i: (i, 0))
    flat = [x for pair in parts for x in pair]
    return pl.pallas_call(
        _dil_mix_kernel,
        grid=(s // tm,),
        in_specs=[blk] * 6,
        out_specs=blk,
        out_shape=jax.ShapeDtypeStruct((s, hw), BF16),
        compiler_params=_params("arbitrary"),
        name=name,
    )(*flat)


def _router_kernel(x_ref, w_ref, b_ref, gate_ref):
    ng, ne = MOE_GROUPS, MOE_EXPERTS_PER_GROUP
    logits = jnp.dot(x_ref[...], w_ref[...], preferred_element_type=F32,
                     precision=lax.Precision.HIGHEST) + b_ref[...]
    lane = lax.broadcasted_iota(jnp.int32, logits.shape, 1).astype(F32)
    big = float(1 << 20)
    is_group = (lane >= ng * ne) & (lane < ng * ne + ng)
    g_logit = jnp.where(is_group, logits, NEG_BIG)
    g_max = jnp.max(g_logit, axis=1, keepdims=True)
    g_top = jnp.min(jnp.where(is_group & (g_logit == g_max), lane, big), axis=1, keepdims=True) - ng * ne
    g_gate = 1.0 / jnp.sum(jnp.where(is_group, jnp.exp(g_logit - g_max), 0.0), axis=1, keepdims=True)
    in_group = (lane >= g_top * ne) & (lane < (g_top + 1) * ne)
    e_logit = jnp.where(in_group, logits, NEG_BIG)
    v1 = jnp.max(e_logit, axis=1, keepdims=True)
    i1 = jnp.min(jnp.where(in_group & (e_logit == v1), lane, big), axis=1, keepdims=True)
    rest = in_group & (lane != i1)
    e_rest = jnp.where(rest, logits, NEG_BIG)
    v2 = jnp.max(e_rest, axis=1, keepdims=True)
    i2 = jnp.min(jnp.where(rest & (e_rest == v2), lane, big), axis=1, keepdims=True)
    e21 = jnp.exp(v2 - v1)
    w1 = g_gate / (1.0 + e21)
    w2 = g_gate * e21 / (1.0 + e21)
    gate_ref[...] = jnp.where(lane == i1, w1, jnp.where(lane == i2, w2, 0.0))


def moe_gates(h, w_router, b_router, name):
    s, d = h.shape
    tm = 256
    return pl.pallas_call(
        _router_kernel,
        grid=(s // tm,),
        in_specs=[pl.BlockSpec((tm, d), lambda i: (i, 0)),
                  pl.BlockSpec((d, LANES), lambda i: (0, 0)),
                  pl.BlockSpec((1, LANES), lambda i: (0, 0))],
        out_specs=pl.BlockSpec((tm, LANES), lambda i: (i, 0)),
        out_shape=jax.ShapeDtypeStruct((s, LANES), F32),
        compiler_params=_params("arbitrary"),
        name=name,
    )(h, w_router, b_router)


def _expert_up_kernel(x_ref, wg_ref, wu_ref, gate_ref, o_ref, wg_bf, wu_bf):
    e = pl.program_id(0)

    @pl.when(pl.program_id(1) == 0)
    def _():
        wg_bf[...] = wg_ref[...].astype(BF16)
        wu_bf[...] = wu_ref[...].astype(BF16)

    x = x_ref[...]
    gate = jnp.dot(x, wg_bf[...], preferred_element_type=F32)
    up = jnp.dot(x, wu_bf[...], preferred_element_type=F32)
    lane = lax.broadcasted_iota(jnp.int32, gate_ref.shape, 1)
    comb = jnp.sum(jnp.where(lane == e, gate_ref[...], 0.0), axis=1, keepdims=True)
    hidden = gate * jax.nn.sigmoid(gate) * up
    o_ref[...] = (hidden * comb).astype(o_ref.dtype)


def expert_up(x, w_gate, w_up, gates, layer, name):
    s, d = x.shape
    _, ne, _, f = w_gate.shape
    tm = 1024
    return pl.pallas_call(
        _expert_up_kernel,
        grid=(ne, s // tm),
        in_specs=[pl.BlockSpec((tm, d), lambda e, i: (i, 0)),
                  pl.BlockSpec((None, None, d, f), lambda e, i: (layer, e, 0, 0)),
                  pl.BlockSpec((None, None, d, f), lambda e, i: (layer, e, 0, 0)),
                  pl.BlockSpec((tm, LANES), lambda e, i: (i, 0))],
        out_specs=pl.BlockSpec((tm, f), lambda e, i: (i, e)),
        out_shape=jax.ShapeDtypeStruct((s, ne * f), BF16),
        scratch_shapes=[pltpu.VMEM((d, f), BF16), pltpu.VMEM((d, f), BF16)],
        compiler_params=_params("arbitrary", "arbitrary"),
        name=name,
    )(x, w_gate, w_up, gates)


def hierarchical_moe(h_f32, h_bf16, w_gr, b_gr, w_er, b_er, w_gate, w_up, w_down, layer, tag):
    d = h_f32.shape[1]
    ng, ne = MOE_GROUPS, MOE_EXPERTS_PER_GROUP
    w_router = jnp.concatenate(
        [jnp.transpose(w_er, (1, 0, 2)).reshape(d, ng * ne), w_gr,
         jnp.zeros((d, LANES - ng * ne - ng), F32)], axis=1)
    b_router = jnp.concatenate(
        [b_er.reshape(ng * ne), b_gr, jnp.zeros((LANES - ng * ne - ng,), F32)]).reshape(1, LANES)
    gates = moe_gates(h_f32, w_router, b_router, f"moe_router_{tag}")
    hidden = expert_up(h_bf16, w_gate, w_up, gates, layer, f"moe_up_{tag}")
    return matmul(hidden, w_down.reshape(w_down.shape[0], -1, d), F32, f"moe_down_{tag}", lead=(layer,))


def stick_breaking_mixer(h_bf16, w_qkv, w_o, j, tag):
    qkv = matmul(h_bf16, w_qkv, BF16, f"sb_qkv_{tag}", lead=(j,))
    o = sb_attention(qkv, SB_HEADS, f"sb_attn_{tag}")
    return matmul(o, w_o, F32, f"sb_out_{tag}", lead=(j,))


def _rope_tables(pos):
    half = MLA_ROPE_DIM // 2
    inv_freq = ROPE_THETA ** (-jnp.arange(half, dtype=F32) / half)
    ang = pos.astype(F32)[:, None] * inv_freq
    cos, sin = jnp.cos(ang), jnp.sin(ang)
    zeros = jnp.zeros((pos.shape[0], LANES - 2 * half), F32)
    z_half = jnp.zeros_like(sin)
    cos_t = jnp.concatenate([cos, cos, zeros], axis=1)
    sin_up = jnp.concatenate([z_half, sin, zeros], axis=1)
    sin_dn = jnp.concatenate([-sin, z_half, zeros], axis=1)
    return cos_t, sin_up, sin_dn


def mla_mixer(h_bf16, pos, w_q_a, q_a_norm, w_q_b, w_kv_a, kv_a_norm, w_kv_b, w_o, j, tag):
    d = h_bf16.shape[1]
    heads = MLA_HEADS
    nope, rope = MLA_NOPE_DIM, MLA_ROPE_DIM
    used = MLA_Q_RANK + MLA_KV_RANK + rope
    width = -(-(MLA_Q_RANK + MLA_KV_RANK + LANES) // 512) * 512
    w_a = jnp.concatenate([w_q_a, w_kv_a, jnp.zeros((d, width - used), F32)], axis=1)
    a = matmul(h_bf16, w_a, F32, f"mla_a_{tag}")
    cos_t, sin_up, sin_dn = _rope_tables(pos)
    cq, ckv, k_rope = mla_prep(a, q_a_norm, kv_a_norm, cos_t, sin_up, sin_dn, f"mla_prep_{tag}")
    w_qb = jnp.pad(w_q_b.reshape(MLA_Q_RANK, heads, nope + rope),
                   ((0, 0), (0, 0), (0, 2 * LANES - nope - rope))).reshape(MLA_Q_RANK, heads * 2 * LANES)
    q = matmul(cq, w_qb, F32, f"mla_qb_{tag}")
    kv = matmul(ckv, w_kv_b, BF16, f"mla_kvb_{tag}", lead=(j,))
    o = mla_attention_core(q, kv, k_rope, cos_t, sin_up, sin_dn, heads, f"mla_attn_{tag}")
    return matmul(o, w_o, F32, f"mla_out_{tag}", lead=(j,))


def dilated_mixer(h_bf16, pos, rel_bias, w_qkv, w_o, j, tag):
    qkv = matmul(h_bf16, w_qkv, BF16, f"dil_qkv_{tag}", lead=(j,))
    tabs = rel_bias.reshape(REL_BUCKETS, DIL_GROUPS, DIL_HEADS)
    parts = [dilated_group(qkv, pos, tabs[:, g, :], g, f"dil_attn_{tag}_g{g}") for g in range(DIL_GROUPS)]
    o = dilated_mix(parts, f"dil_mix_{tag}")
    return matmul(o, w_o, F32, f"dil_out_{tag}", lead=(j,))


def kernel(x, positions, rel_bias, sb_w_qkv, sb_w_o, mla_w_q_a, mla_q_a_norm, mla_w_q_b, mla_w_kv_a,
           mla_kv_a_norm, mla_w_kv_b, mla_w_o, dil_w_qkv, dil_w_o, ln_gain, ln_bias,
           moe_w_group_router, moe_b_group_router, moe_w_expert_router, moe_b_expert_router,
           moe_w_gate, moe_w_up, moe_w_down):
    batch, seq, d = x.shape
    outs = []
    for b in range(batch):
        h = x[b]
        h_bf16 = h.astype(BF16)
        pos = positions[b]
        for i in range(DEPTH):
            kind, j = i % N_MIXERS, i // N_MIXERS
            tag = f"l{i}"
            if kind == 0:
                mix = stick_breaking_mixer(h_bf16, sb_w_qkv, sb_w_o, j, tag)
            elif kind == 1:
                mix = mla_mixer(h_bf16, pos, mla_w_q_a[j], mla_q_a_norm[j], mla_w_q_b[j], mla_w_kv_a[j],
                                mla_kv_a_norm[j], mla_w_kv_b, mla_w_o, j, tag)
            else:
                mix = dilated_mixer(h_bf16, pos, rel_bias, dil_w_qkv, dil_w_o, j, tag)
            h, h_bf16 = residual_layer_norm(h, mix, ln_gain[i, 0], ln_bias[i, 0], f"ln_mix_{tag}")
            ffn = hierarchical_moe(h, h_bf16, moe_w_group_router[i], moe_b_group_router[i],
                                   moe_w_expert_router[i], moe_b_expert_router[i],
                                   moe_w_gate, moe_w_up, moe_w_down, i, tag)
            h, h_bf16 = residual_layer_norm(h, ffn, ln_gain[i, 1], ln_bias[i, 1], f"ln_ffn_{tag}")
        outs.append(h)
    return jnp.stack(outs)
```

```python
import functools
import math

import jax
import jax.numpy as jnp
from jax import lax
from jax.experimental import pallas as pl
from jax.experimental.pallas import tpu as pltpu

DEPTH = 4
N_MIXERS = 3
LN_EPS = 1e-5
RMS_EPS = 1e-6

SB_HEADS = 32
SB_HEAD_DIM = 128

MLA_HEADS = 32
MLA_Q_RANK = 1024
MLA_KV_RANK = 512
MLA_NOPE_DIM = 128
MLA_ROPE_DIM = 64
MLA_V_DIM = 128
ROPE_THETA = 10000.0

DIL_DILATIONS = (1, 4, 16)
DIL_GROUPS = 3
DIL_HEADS = 16
DIL_HEAD_DIM = 128
DIL_WINDOW_KEYS = 128

REL_BUCKETS = 32
REL_MAX_DIST = 2048

MOE_GROUPS = 4
MOE_EXPERTS_PER_GROUP = 8
EXPERT_HIDDEN = 256

DN_ALPHA = (2 * DEPTH) ** 0.25

LANES = 128
VMEM_LIMIT_BYTES = 56 * 1024 * 1024
MATMUL_VMEM_BUDGET = 40 * 1024 * 1024
NEG_BIG = -1e30
EXP2_UNDERFLOW = -151.0
SB_HEADS_PER_STEP = 2
MLA_HEADS_PER_STEP = 2
MOE_TILE = 256
ROUTE_RANK_LANE = MOE_GROUPS * MOE_EXPERTS_PER_GROUP
ROUTE_GROUP_LANE = ROUTE_RANK_LANE + 1

BF16 = jnp.bfloat16
F32 = jnp.float32

_NT = (((1,), (1,)), ((), ()))


def _params(*sem):
    return pltpu.CompilerParams(dimension_semantics=sem, vmem_limit_bytes=VMEM_LIMIT_BYTES)


def _mm_kernel(x_ref, w_ref, o_ref, wbf_ref):
    @pl.when(pl.program_id(1) == 0)
    def _():
        wbf_ref[...] = w_ref[...].astype(BF16)

    o_ref[...] = jnp.dot(x_ref[...], wbf_ref[...], preferred_element_type=F32).astype(o_ref.dtype)


def _mm_tiles(m, k, n, out_bytes):
    for tm, tn in ((1024, 512), (512, 512), (512, 256), (256, 256), (256, 128), (128, 128)):
        if m % tm or n % tn:
            continue
        need = 2 * tm * k * 2 + 2 * k * tn * 4 + k * tn * 2 + 2 * tm * tn * out_bytes
        if need <= MATMUL_VMEM_BUDGET:
            return tm, tn
    raise ValueError(f"no matmul tiling for {(m, k, n)}")


def matmul(x, w, out_dtype, name, lead=()):
    m, k = x.shape
    n = w.shape[-1]
    tm, tn = _mm_tiles(m, k, n, jnp.dtype(out_dtype).itemsize)
    return pl.pallas_call(
        _mm_kernel,
        grid=(n // tn, m // tm),
        in_specs=[pl.BlockSpec((tm, k), lambda j, i: (i, 0)),
                  pl.BlockSpec((None,) * len(lead) + (k, tn), lambda j, i: tuple(lead) + (0, j))],
        out_specs=pl.BlockSpec((tm, tn), lambda j, i: (i, j)),
        out_shape=jax.ShapeDtypeStruct((m, n), out_dtype),
        scratch_shapes=[pltpu.VMEM((k, tn), BF16)],
        compiler_params=_params("arbitrary", "arbitrary"),
        name=name,
    )(x, w)


def _ln_kernel(h_ref, mix_ref, g_ref, b_ref, of_ref, ob_ref):
    x = DN_ALPHA * h_ref[...] + mix_ref[...]
    mu = jnp.mean(x, axis=-1, keepdims=True)
    xc = x - mu
    var = jnp.mean(xc * xc, axis=-1, keepdims=True)
    y = xc * lax.rsqrt(var + LN_EPS) * g_ref[...] + b_ref[...]
    of_ref[...] = y
    ob_ref[...] = y.astype(BF16)


def residual_layer_norm(h, mix, g, b, name):
    s, d = h.shape
    tm = 256
    row = pl.BlockSpec((tm, d), lambda i: (i, 0))
    vec = pl.BlockSpec((1, d), lambda i: (0, 0))
    return pl.pallas_call(
        _ln_kernel,
        grid=(s // tm,),
        in_specs=[row, row, vec, vec],
        out_specs=[row, row],
        out_shape=[jax.ShapeDtypeStruct((s, d), F32), jax.ShapeDtypeStruct((s, d), BF16)],
        compiler_params=_params("arbitrary"),
        name=name,
    )(h, mix, g.reshape(1, d), b.reshape(1, d))


def _sb_kernel(q_ref, k_ref, v_ref, o_ref, acc_ref, carry_ref, *, t, hps, c2):
    qi = pl.program_id(1)
    dh = SB_HEAD_DIM
    heads = [slice(c * dh, (c + 1) * dh) for c in range(hps)]
    q_neg = [-q_ref[:, hs] for hs in heads]
    acc_ref[...] = jnp.zeros_like(acc_ref)
    carry_ref[...] = jnp.zeros_like(carry_ref)
    r = lax.broadcasted_iota(jnp.int32, (t, t), 0)
    c = lax.broadcasted_iota(jnp.int32, (t, t), 1)
    strict = c < r
    later_mask = jnp.where(r > c, 1.0, 0.0).astype(BF16)
    later_mask2 = jnp.concatenate([later_mask, later_mask], axis=0)

    def step(j, masked):
        ks = pl.multiple_of(j * t, t)
        for ci, hs in enumerate(heads):
            k = k_ref[pl.ds(ks, t), hs]
            v = v_ref[pl.ds(ks, t), hs]
            nz = lax.dot_general(q_neg[ci], k, _NT, preferred_element_type=F32) * c2
            log_keep = jnp.minimum(nz, 0.0) - jnp.log2(1.0 + jnp.exp2(-jnp.abs(nz)))
            if masked:
                log_keep = jnp.where(strict, log_keep, 0.0)
            hi = pltpu.bitcast(pltpu.bitcast(log_keep, jnp.uint32) & jnp.uint32(0xFFFF0000), F32)
            hi_lo = jnp.concatenate([hi.astype(BF16), (log_keep - hi).astype(BF16)], axis=1)
            later = jnp.dot(hi_lo, later_mask2, preferred_element_type=F32)
            carry = carry_ref[ci]
            tot = later + jnp.concatenate([carry] * (t // LANES), axis=1)
            a = jnp.exp2(log_keep - nz + tot)
            if masked:
                a = jnp.where(strict, a, 0.0)
            acc_ref[ci] += jnp.dot(a.astype(BF16), v, preferred_element_type=F32)
            carry_ref[ci] = carry + jnp.sum(log_keep, axis=1, keepdims=True)

    def largest_carry():
        m = carry_ref[0]
        for ci in range(1, hps):
            m = jnp.maximum(m, carry_ref[ci])
        return jnp.max(m)

    step(qi, True)

    def cond(state):
        j, top = state
        return jnp.logical_and(j >= 0, top >= EXP2_UNDERFLOW)

    def body(state):
        j, _ = state
        step(j, False)
        return j - 1, largest_carry()

    lax.while_loop(cond, body, (qi - 1, largest_carry()))
    for ci, hs in enumerate(heads):
        o_ref[:, hs] = acc_ref[ci].astype(o_ref.dtype)


def sb_attention(qkv, heads, name):
    s = qkv.shape[0]
    dh = SB_HEAD_DIM
    t = min(256, s)
    hps = SB_HEADS_PER_STEP
    groups = heads // hps
    kern = functools.partial(_sb_kernel, t=t, hps=hps, c2=dh ** -0.5 / math.log(2.0))
    return pl.pallas_call(
        kern,
        grid=(groups, s // t),
        in_specs=[pl.BlockSpec((t, hps * dh), lambda h, i: (i, h)),
                  pl.BlockSpec((s, hps * dh), lambda h, i: (0, groups + h)),
                  pl.BlockSpec((s, hps * dh), lambda h, i: (0, 2 * groups + h))],
        out_specs=pl.BlockSpec((t, hps * dh), lambda h, i: (i, h)),
        out_shape=jax.ShapeDtypeStruct((s, heads * dh), BF16),
        scratch_shapes=[pltpu.VMEM((hps, t, dh), F32), pltpu.VMEM((hps, t, LANES), F32)],
        compiler_params=_params("arbitrary", "arbitrary"),
        name=name,
    )(qkv, qkv, qkv)


def _rope_lanes(x, cos, sin_up, sin_dn):
    half = MLA_ROPE_DIM // 2
    return (x * cos + pltpu.roll(x, half, 1) * sin_up + pltpu.roll(x, LANES - half, 1) * sin_dn)


def _mla_prep_kernel(a_ref, gq_ref, gkv_ref, cos_ref, sup_ref, sdn_ref, cq_ref, ckv_ref, kr_ref):
    a = a_ref[...]
    rq, rkv = MLA_Q_RANK, MLA_KV_RANK
    cq = a[:, :rq]
    cq_ref[...] = (cq * lax.rsqrt(jnp.mean(cq * cq, axis=-1, keepdims=True) + RMS_EPS)
                   * gq_ref[...]).astype(BF16)
    ckv = a[:, rq:rq + rkv]
    ckv_ref[...] = (ckv * lax.rsqrt(jnp.mean(ckv * ckv, axis=-1, keepdims=True) + RMS_EPS)
                    * gkv_ref[...]).astype(BF16)
    kr = a[:, rq + rkv:rq + rkv + LANES]
    kr_ref[...] = _rope_lanes(kr, cos_ref[...], sup_ref[...], sdn_ref[...]).astype(BF16)


def mla_prep(a, gq, gkv, cos, sin_up, sin_dn, name):
    s, n = a.shape
    tm = 512
    row = lambda w: pl.BlockSpec((tm, w), lambda i: (i, 0))
    vec = lambda w: pl.BlockSpec((1, w), lambda i: (0, 0))
    return pl.pallas_call(
        _mla_prep_kernel,
        grid=(s // tm,),
        in_specs=[row(n), vec(MLA_Q_RANK), vec(MLA_KV_RANK), row(LANES), row(LANES), row(LANES)],
        out_specs=[row(MLA_Q_RANK), row(MLA_KV_RANK), row(LANES)],
        out_shape=[jax.ShapeDtypeStruct((s, MLA_Q_RANK), BF16),
                   jax.ShapeDtypeStruct((s, MLA_KV_RANK), BF16),
                   jax.ShapeDtypeStruct((s, LANES), BF16)],
        compiler_params=_params("arbitrary"),
        name=name,
    )(a, gq.reshape(1, -1), gkv.reshape(1, -1), cos, sin_up, sin_dn)


def _mla_kernel(q_ref, kv_ref, kr_ref, cos_ref, sup_ref, sdn_ref, o_ref,
                m_ref, l_ref, acc_ref, *, t, hps, c2):
    qi = pl.program_id(1)
    wide = 2 * LANES
    cos, sin_up, sin_dn = cos_ref[...], sup_ref[...], sdn_ref[...]
    qc = []
    for ci in range(hps):
        q = q_ref[:, ci * wide:(ci + 1) * wide]
        q_rope = _rope_lanes(q[:, LANES:], cos, sin_up, sin_dn)
        qc.append(jnp.concatenate([q[:, :LANES].astype(BF16), q_rope.astype(BF16)], axis=1))
    m_ref[...] = jnp.full_like(m_ref, NEG_BIG)
    l_ref[...] = jnp.zeros_like(l_ref)
    acc_ref[...] = jnp.zeros_like(acc_ref)
    causal = (lax.broadcasted_iota(jnp.int32, (t, t), 1) <= lax.broadcasted_iota(jnp.int32, (t, t), 0))

    def step(j, masked):
        ks = pl.multiple_of(j * t, t)
        kr = kr_ref[pl.ds(ks, t), :]
        for ci in range(hps):
            kn = kv_ref[pl.ds(ks, t), ci * wide:ci * wide + LANES]
            v = kv_ref[pl.ds(ks, t), ci * wide + LANES:(ci + 1) * wide]
            s = lax.dot_general(qc[ci], jnp.concatenate([kn, kr], axis=1), _NT,
                                preferred_element_type=F32) * c2
            if masked:
                s = jnp.where(causal, s, NEG_BIG)
            m_prev = m_ref[ci]
            m_new = jnp.maximum(m_prev, jnp.max(s, axis=1, keepdims=True))
            alpha = jnp.exp2(m_prev - m_new)
            p = jnp.exp2(s - jnp.concatenate([m_new] * (t // LANES), axis=1))
            l_ref[ci] = alpha * l_ref[ci] + jnp.sum(p, axis=1, keepdims=True)
            acc_ref[ci] = alpha * acc_ref[ci] + jnp.dot(p.astype(BF16), v, preferred_element_type=F32)
            m_ref[ci] = m_new

    def body(j, _):
        step(j, False)
        return 0

    lax.fori_loop(0, qi, body, 0)
    step(qi, True)
    for ci in range(hps):
        o_ref[:, ci * LANES:(ci + 1) * LANES] = (acc_ref[ci] / l_ref[ci]).astype(o_ref.dtype)


def mla_attention_core(q, kv, k_rope, cos, sin_up, sin_dn, heads, name):
    assert MLA_V_DIM == LANES and MLA_NOPE_DIM == LANES
    s = q.shape[0]
    t = min(512, s)
    hps = MLA_HEADS_PER_STEP
    kern = functools.partial(_mla_kernel, t=t, hps=hps,
                             c2=(MLA_NOPE_DIM + MLA_ROPE_DIM) ** -0.5 / math.log(2.0))
    tab = pl.BlockSpec((t, LANES), lambda h, i: (i, 0))
    return pl.pallas_call(
        kern,
        grid=(heads // hps, s // t),
        in_specs=[pl.BlockSpec((t, hps * 2 * LANES), lambda h, i: (i, h)),
                  pl.BlockSpec((s, hps * 2 * LANES), lambda h, i: (0, h)),
                  pl.BlockSpec((s, LANES), lambda h, i: (0, 0)),
                  tab, tab, tab],
        out_specs=pl.BlockSpec((t, hps * LANES), lambda h, i: (i, h)),
        out_shape=jax.ShapeDtypeStruct((s, heads * MLA_V_DIM), BF16),
        scratch_shapes=[pltpu.VMEM((hps, t, LANES), F32), pltpu.VMEM((hps, t, LANES), F32),
                        pltpu.VMEM((hps, t, LANES), F32)],
        compiler_params=_params("arbitrary", "arbitrary"),
        name=name,
    )(q, kv, k_rope, cos, sin_up, sin_dn)


def _t5_bucket(rel):
    n = jnp.maximum(rel, 0)
    max_exact = REL_BUCKETS // 2
    nf = jnp.maximum(n, 1).astype(F32)
    large = max_exact + (jnp.log(nf / max_exact) / math.log(REL_MAX_DIST / max_exact)
                         * (REL_BUCKETS - max_exact)).astype(jnp.int32)
    large = jnp.minimum(large, REL_BUCKETS - 1)
    return jnp.where(n < max_exact, n, large)


def _dil_kernel(tab_ref, q_ref, kp_ref, kc_ref, vp_ref, vc_ref, pq_ref, pkp_ref, pkc_ref,
                o_ref, lse_ref, *, heads, scale):
    a = pl.program_id(1)
    w = DIL_WINDOW_KEYS
    dh = DIL_HEAD_DIM
    rel = pq_ref[...] - jnp.concatenate([pkp_ref[...], pkc_ref[...]], axis=1)
    back = (lax.broadcasted_iota(jnp.int32, (w, 2 * w), 0) + w
            - lax.broadcasted_iota(jnp.int32, (w, 2 * w), 1))
    col = lax.broadcasted_iota(jnp.int32, (w, 2 * w), 1)
    first_col = jnp.where(a > 0, 0, w)
    valid = (back >= 0) & (back <= w) & (col >= first_col)
    bucket = _t5_bucket(rel)
    for h in range(heads):
        hs = slice(h * dh, (h + 1) * dh)
        bias = jnp.zeros((w, 2 * w), F32)
        for b in range(REL_BUCKETS):
            bias = jnp.where(bucket == b, tab_ref[b, h], bias)
        k = jnp.concatenate([kp_ref[:, hs], kc_ref[:, hs]], axis=0)
        v = jnp.concatenate([vp_ref[:, hs], vc_ref[:, hs]], axis=0)
        s = lax.dot_general(q_ref[:, hs], k, _NT, preferred_element_type=F32) * scale + bias
        s = jnp.where(valid, s, NEG_BIG)
        m = jnp.max(s, axis=1, keepdims=True)
        p = jnp.exp(s - m)
        l = jnp.sum(p, axis=1, keepdims=True)
        o_ref[:, hs] = jnp.dot(p.astype(BF16), v, preferred_element_type=F32) / l
        lse_ref[:, hs] = jnp.broadcast_to(m + jnp.log(l), (w, dh))


def dilated_group(qkv, pos, tab, g, name):
    s, n = qkv.shape
    d = DIL_DILATIONS[g]
    heads, dh, w = DIL_HEADS, DIL_HEAD_DIM, DIL_WINDOW_KEYS
    hw = heads * dh
    per_row = n // hw
    sub = s // d
    qkv_v = qkv.reshape(sub, d * n)
    pos_col = pos.reshape(sub, d).T.reshape(d, sub, 1)
    pos_row = pos.reshape(sub, d).T.reshape(d, 1, sub)
    G = DIL_GROUPS
    prev = lambda i: jnp.maximum(i - 1, 0)
    kern = functools.partial(_dil_kernel, heads=heads, scale=dh ** -0.5)
    o, lse = pl.pallas_call(
        kern,
        grid=(d, sub // w),
        in_specs=[pl.BlockSpec(memory_space=pltpu.SMEM),
                  pl.BlockSpec((w, hw), lambda r, i: (i, r * per_row + g)),
                  pl.BlockSpec((w, hw), lambda r, i: (prev(i), r * per_row + G + g)),
                  pl.BlockSpec((w, hw), lambda r, i: (i, r * per_row + G + g)),
                  pl.BlockSpec((w, hw), lambda r, i: (prev(i), r * per_row + 2 * G + g)),
                  pl.BlockSpec((w, hw), lambda r, i: (i, r * per_row + 2 * G + g)),
                  pl.BlockSpec((None, w, 1), lambda r, i: (r, i, 0)),
                  pl.BlockSpec((None, 1, w), lambda r, i: (r, 0, prev(i))),
                  pl.BlockSpec((None, 1, w), lambda r, i: (r, 0, i))],
        out_specs=[pl.BlockSpec((w, hw), lambda r, i: (i, r)),
                   pl.BlockSpec((w, hw), lambda r, i: (i, r))],
        out_shape=[jax.ShapeDtypeStruct((sub, d * hw), F32), jax.ShapeDtypeStruct((sub, d * hw), F32)],
        compiler_params=_params("arbitrary", "arbitrary"),
        name=name,
    )(tab, qkv_v, qkv_v, qkv_v, qkv_v, qkv_v, pos_col, pos_row, pos_row)
    return o.reshape(s, hw), lse.reshape(s, hw)


def _dil_mix_kernel(o0, l0, o1, l1, o2, l2, out_ref):
    a, b, c = l0[...], l1[...], l2[...]
    m = jnp.maximum(jnp.maximum(a, b), c)
    ea, eb, ec = jnp.exp(a - m), jnp.exp(b - m), jnp.exp(c - m)
    out = (ea * o0[...] + eb * o1[...] + ec * o2[...]) / (ea + eb + ec)
    out_ref[...] = out.astype(out_ref.dtype)


def dilated_mix(parts, name):
    s, hw = parts[0][0].shape
    tm = 256
    blk = pl.BlockSpec((tm, hw), lambda i: (i, 0))
    flat = [x for pair in parts for x in pair]
    return pl.pallas_call(
        _dil_mix_kernel,
        grid=(s // tm,),
        in_specs=[blk] * 6,
        out_specs=blk,
        out_shape=jax.ShapeDtypeStruct((s, hw), BF16),
        compiler_params=_params("arbitrary"),
        name=name,
    )(*flat)


def _router_kernel(x_ref, w_ref, b_ref, route_ref, count_ref, run_ref):
    ng, ne = MOE_GROUPS, MOE_EXPERTS_PER_GROUP
    logits = jnp.dot(x_ref[...], w_ref[...], preferred_element_type=F32,
                     precision=lax.Precision.HIGHEST) + b_ref[...]
    lane = lax.broadcasted_iota(jnp.int32, logits.shape, 1).astype(F32)
    big = float(1 << 20)
    is_group = (lane >= ng * ne) & (lane < ng * ne + ng)
    g_logit = jnp.where(is_group, logits, NEG_BIG)
    g_max = jnp.max(g_logit, axis=1, keepdims=True)
    g_top = jnp.min(jnp.where(is_group & (g_logit == g_max), lane, big), axis=1, keepdims=True) - ng * ne
    g_gate = 1.0 / jnp.sum(jnp.where(is_group, jnp.exp(g_logit - g_max), 0.0), axis=1, keepdims=True)
    in_group = (lane >= g_top * ne) & (lane < (g_top + 1) * ne)
    e_logit = jnp.where(in_group, logits, NEG_BIG)
    v1 = jnp.max(e_logit, axis=1, keepdims=True)
    i1 = jnp.min(jnp.where(in_group & (e_logit == v1), lane, big), axis=1, keepdims=True)
    rest = in_group & (lane != i1)
    e_rest = jnp.where(rest, logits, NEG_BIG)
    v2 = jnp.max(e_rest, axis=1, keepdims=True)
    i2 = jnp.min(jnp.where(rest & (e_rest == v2), lane, big), axis=1, keepdims=True)
    e21 = jnp.exp(v2 - v1)
    w1 = g_gate / (1.0 + e21)
    w2 = g_gate * e21 / (1.0 + e21)
    gates = jnp.where(lane == i1, w1, jnp.where(lane == i2, w2, 0.0))

    @pl.when(pl.program_id(0) == 0)
    def _():
        run_ref[...] = jnp.zeros_like(run_ref)

    tm = x_ref.shape[0]
    member = jnp.where(lane == g_top, 1.0, 0.0)
    earlier = (lax.broadcasted_iota(jnp.int32, (tm, tm), 1)
               < lax.broadcasted_iota(jnp.int32, (tm, tm), 0))
    before = jnp.dot(jnp.where(earlier, 1.0, 0.0).astype(BF16), member.astype(BF16),
                     preferred_element_type=F32)
    run = run_ref[...]
    rank = jnp.sum(jnp.where(lane == g_top, before + run, 0.0), axis=1, keepdims=True)
    run_ref[...] = run + jnp.sum(member, axis=0, keepdims=True)
    count_ref[...] = jnp.broadcast_to(run_ref[...], count_ref.shape)
    route_ref[...] = jnp.where(lane == ROUTE_RANK_LANE, rank,
                               jnp.where(lane == ROUTE_GROUP_LANE, g_top, gates))


def moe_route(h, w_router, b_router, name):
    s, d = h.shape
    tm = MOE_TILE
    return pl.pallas_call(
        _router_kernel,
        grid=(s // tm,),
        in_specs=[pl.BlockSpec((tm, d), lambda i: (i, 0)),
                  pl.BlockSpec((d, LANES), lambda i: (0, 0)),
                  pl.BlockSpec((1, LANES), lambda i: (0, 0))],
        out_specs=[pl.BlockSpec((tm, LANES), lambda i: (i, 0)),
                   pl.BlockSpec((8, LANES), lambda i: (0, 0))],
        out_shape=[jax.ShapeDtypeStruct((s, LANES), F32), jax.ShapeDtypeStruct((8, LANES), F32)],
        scratch_shapes=[pltpu.VMEM((1, LANES), F32)],
        compiler_params=_params("arbitrary"),
        name=name,
    )(h, w_router, b_router)


def _dispatch_kernel(pos_ref, h_ref, route_ref, xs_in, rs_in, xs_ref, rs_ref, sem):
    del xs_in, rs_in
    tm = h_ref.shape[0]
    base = pl.program_id(0) * tm

    def row_copies(r):
        dst = pos_ref[base + r]
        return (pltpu.make_async_copy(h_ref.at[pl.ds(r, 1), :], xs_ref.at[pl.ds(dst, 1), :], sem.at[0]),
                pltpu.make_async_copy(route_ref.at[pl.ds(r, 1), :], rs_ref.at[pl.ds(dst, 1), :], sem.at[1]))

    def start(r, carry):
        for cp in row_copies(r):
            cp.start()
        return carry

    def wait(r, carry):
        for cp in row_copies(r):
            cp.wait()
        return carry

    lax.fori_loop(0, tm, start, 0)
    lax.fori_loop(0, tm, wait, 0)


def moe_dispatch(h, route, pos, rows, name):
    s, d = h.shape
    tm = MOE_TILE
    grid_spec = pltpu.PrefetchScalarGridSpec(
        num_scalar_prefetch=1,
        grid=(s // tm,),
        in_specs=[pl.BlockSpec((tm, d), lambda i, pos: (i, 0)),
                  pl.BlockSpec((tm, LANES), lambda i, pos: (i, 0)),
                  pl.BlockSpec(memory_space=pl.ANY),
                  pl.BlockSpec(memory_space=pl.ANY)],
        out_specs=[pl.BlockSpec(memory_space=pl.ANY), pl.BlockSpec(memory_space=pl.ANY)],
        scratch_shapes=[pltpu.SemaphoreType.DMA((2,))],
    )
    return pl.pallas_call(
        _dispatch_kernel,
        grid_spec=grid_spec,
        out_shape=[jax.ShapeDtypeStruct((rows, d), F32), jax.ShapeDtypeStruct((rows, LANES), F32)],
        input_output_aliases={3: 0, 4: 1},
        compiler_params=_params("arbitrary"),
        name=name,
    )(pos, h, route, jnp.zeros((rows, d), F32), jnp.zeros((rows, LANES), F32))


def _weights_changed(tg_ref, t):
    return jnp.logical_or(t == 0, tg_ref[t] != tg_ref[jnp.maximum(t - 1, 0)])


def _group_up_kernel(tg_ref, x_ref, route_ref, wg_ref, wu_ref, o_ref, wg_bf, wu_bf):
    e = pl.program_id(0)
    t = pl.program_id(1)

    @pl.when(_weights_changed(tg_ref, t))
    def _():
        wg_bf[...] = wg_ref[...].astype(BF16)
        wu_bf[...] = wu_ref[...].astype(BF16)

    x = x_ref[...].astype(BF16)
    gate = jnp.dot(x, wg_bf[...], preferred_element_type=F32)
    up = jnp.dot(x, wu_bf[...], preferred_element_type=F32)
    lane = lax.broadcasted_iota(jnp.int32, route_ref.shape, 1)
    flat_expert = tg_ref[t] * MOE_EXPERTS_PER_GROUP + e
    comb = jnp.sum(jnp.where(lane == flat_expert, route_ref[...], 0.0), axis=1, keepdims=True)
    hidden = gate * jax.nn.sigmoid(gate) * up
    o_ref[...] = (hidden * comb).astype(o_ref.dtype)


def moe_group_up(xs, rs, tile_group, w_gate, w_up, layer, name):
    rows, d = xs.shape
    ne = MOE_EXPERTS_PER_GROUP
    f = w_gate.shape[-1]
    tm = MOE_TILE
    wspec = pl.BlockSpec((None, None, d, f), lambda e, t, tg: (layer, tg[t] * ne + e, 0, 0))
    grid_spec = pltpu.PrefetchScalarGridSpec(
        num_scalar_prefetch=1,
        grid=(ne, rows // tm),
        in_specs=[pl.BlockSpec((tm, d), lambda e, t, tg: (t, 0)),
                  pl.BlockSpec((tm, LANES), lambda e, t, tg: (t, 0)),
                  wspec, wspec],
        out_specs=pl.BlockSpec((tm, f), lambda e, t, tg: (t, e)),
        scratch_shapes=[pltpu.VMEM((d, f), BF16), pltpu.VMEM((d, f), BF16)],
    )
    return pl.pallas_call(
        _group_up_kernel,
        grid_spec=grid_spec,
        out_shape=jax.ShapeDtypeStruct((rows, ne * f), BF16),
        compiler_params=_params("arbitrary", "arbitrary"),
        name=name,
    )(tile_group, xs, rs, w_gate, w_up)


def _group_down_kernel(tg_ref, x_ref, w_ref, o_ref, w_bf):
    @pl.when(_weights_changed(tg_ref, pl.program_id(1)))
    def _():
        w_bf[...] = w_ref[...].astype(BF16)

    o_ref[...] = jnp.dot(x_ref[...], w_bf[...], preferred_element_type=F32)


def moe_group_down(hidden, tile_group, w_down, layer, name):
    rows, k = hidden.shape
    d = w_down.shape[-1]
    tm = MOE_TILE
    tn = min(1024, d)
    w4 = w_down.reshape(w_down.shape[0], MOE_GROUPS, k, d)
    grid_spec = pltpu.PrefetchScalarGridSpec(
        num_scalar_prefetch=1,
        grid=(d // tn, rows // tm),
        in_specs=[pl.BlockSpec((tm, k), lambda j, t, tg: (t, 0)),
                  pl.BlockSpec((None, None, k, tn), lambda j, t, tg: (layer, tg[t], 0, j))],
        out_specs=pl.BlockSpec((tm, tn), lambda j, t, tg: (t, j)),
        scratch_shapes=[pltpu.VMEM((k, tn), BF16)],
    )
    return pl.pallas_call(
        _group_down_kernel,
        grid_spec=grid_spec,
        out_shape=jax.ShapeDtypeStruct((rows, d), F32),
        compiler_params=_params("arbitrary", "arbitrary"),
        name=name,
    )(tile_group, hidden, w4)


def _ln_gather_kernel(pos_ref, h_ref, ys_ref, g_ref, b_ref, of_ref, ob_ref, y_buf, sem):
    tm = h_ref.shape[0]
    base = pl.program_id(0) * tm

    def row_copy(r):
        return pltpu.make_async_copy(ys_ref.at[pl.ds(pos_ref[base + r], 1), :],
                                     y_buf.at[pl.ds(r, 1), :], sem.at[0])

    def start(r, carry):
        row_copy(r).start()
        return carry

    def wait(r, carry):
        row_copy(r).wait()
        return carry

    lax.fori_loop(0, tm, start, 0)
    lax.fori_loop(0, tm, wait, 0)
    _ln_kernel(h_ref, y_buf, g_ref, b_ref, of_ref, ob_ref)


def residual_layer_norm_gathered(h, ys, pos, g, b, name):
    s, d = h.shape
    tm = MOE_TILE
    row = pl.BlockSpec((tm, d), lambda i, pos: (i, 0))
    vec = pl.BlockSpec((1, d), lambda i, pos: (0, 0))
    grid_spec = pltpu.PrefetchScalarGridSpec(
        num_scalar_prefetch=1,
        grid=(s // tm,),
        in_specs=[row, pl.BlockSpec(memory_space=pl.ANY), vec, vec],
        out_specs=[row, row],
        scratch_shapes=[pltpu.VMEM((tm, d), F32), pltpu.SemaphoreType.DMA((1,))],
    )
    return pl.pallas_call(
        _ln_gather_kernel,
        grid_spec=grid_spec,
        out_shape=[jax.ShapeDtypeStruct((s, d), F32), jax.ShapeDtypeStruct((s, d), BF16)],
        compiler_params=_params("arbitrary"),
        name=name,
    )(pos, h, ys, g.reshape(1, d), b.reshape(1, d))


def hierarchical_moe(h_f32, w_gr, b_gr, w_er, b_er, w_gate, w_up, w_down, layer, tag):
    s, d = h_f32.shape
    ng, ne = MOE_GROUPS, MOE_EXPERTS_PER_GROUP
    tm = MOE_TILE
    w_router = jnp.concatenate(
        [jnp.transpose(w_er, (1, 0, 2)).reshape(d, ng * ne), w_gr,
         jnp.zeros((d, LANES - ng * ne - ng), F32)], axis=1)
    b_router = jnp.concatenate(
        [b_er.reshape(ng * ne), b_gr, jnp.zeros((LANES - ng * ne - ng,), F32)]).reshape(1, LANES)
    route, counts = moe_route(h_f32, w_router, b_router, f"moe_router_{tag}")
    counts = counts[0, :ng].astype(jnp.int32)
    padded = (counts + tm - 1) // tm * tm
    ends = jnp.cumsum(padded)
    rank = route[:, ROUTE_RANK_LANE].astype(jnp.int32)
    group = route[:, ROUTE_GROUP_LANE].astype(jnp.int32)
    pos = (ends - padded)[group] + rank
    rows = s + ng * tm
    tile_start = jnp.arange(rows // tm, dtype=jnp.int32) * tm
    tile_group = jnp.minimum(jnp.sum(tile_start[:, None] >= ends[None, :], axis=1), ng - 1).astype(jnp.int32)
    xs, rs = moe_dispatch(h_f32, route, pos, rows, f"moe_dispatch_{tag}")
    hidden = moe_group_up(xs, rs, tile_group, w_gate, w_up, layer, f"moe_up_{tag}")
    ys = moe_group_down(hidden, tile_group, w_down, layer, f"moe_down_{tag}")
    return ys, pos


def stick_breaking_mixer(h_bf16, w_qkv, w_o, j, tag):
    qkv = matmul(h_bf16, w_qkv, BF16, f"sb_qkv_{tag}", lead=(j,))
    o = sb_attention(qkv, SB_HEADS, f"sb_attn_{tag}")
    return matmul(o, w_o, F32, f"sb_out_{tag}", lead=(j,))


def _rope_tables(pos):
    half = MLA_ROPE_DIM // 2
    inv_freq = ROPE_THETA ** (-jnp.arange(half, dtype=F32) / half)
    ang = pos.astype(F32)[:, None] * inv_freq
    cos, sin = jnp.cos(ang), jnp.sin(ang)
    zeros = jnp.zeros((pos.shape[0], LANES - 2 * half), F32)
    z_half = jnp.zeros_like(sin)
    cos_t = jnp.concatenate([cos, cos, zeros], axis=1)
    sin_up = jnp.concatenate([z_half, sin, zeros], axis=1)
    sin_dn = jnp.concatenate([-sin, z_half, zeros], axis=1)
    return cos_t, sin_up, sin_dn


def mla_mixer(h_bf16, pos, w_q_a, q_a_norm, w_q_b, w_kv_a, kv_a_norm, w_kv_b, w_o, j, tag):
    d = h_bf16.shape[1]
    heads = MLA_HEADS
    nope, rope = MLA_NOPE_DIM, MLA_ROPE_DIM
    used = MLA_Q_RANK + MLA_KV_RANK + rope
    width = -(-(MLA_Q_RANK + MLA_KV_RANK + LANES) // 512) * 512
    w_a = jnp.concatenate([w_q_a, w_kv_a, jnp.zeros((d, width - used), F32)], axis=1)
    a = matmul(h_bf16, w_a, F32, f"mla_a_{tag}")
    cos_t, sin_up, sin_dn = _rope_tables(pos)
    cq, ckv, k_rope = mla_prep(a, q_a_norm, kv_a_norm, cos_t, sin_up, sin_dn, f"mla_prep_{tag}")
    w_qb = jnp.pad(w_q_b.reshape(MLA_Q_RANK, heads, nope + rope),
                   ((0, 0), (0, 0), (0, 2 * LANES - nope - rope))).reshape(MLA_Q_RANK, heads * 2 * LANES)
    q = matmul(cq, w_qb, F32, f"mla_qb_{tag}")
    kv = matmul(ckv, w_kv_b, BF16, f"mla_kvb_{tag}", lead=(j,))
    o = mla_attention_core(q, kv, k_rope, cos_t, sin_up, sin_dn, heads, f"mla_attn_{tag}")
    return matmul(o, w_o, F32, f"mla_out_{tag}", lead=(j,))


def dilated_mixer(h_bf16, pos, rel_bias, w_qkv, w_o, j, tag):
    qkv = matmul(h_bf16, w_qkv, BF16, f"dil_qkv_{tag}", lead=(j,))
    tabs = rel_bias.reshape(REL_BUCKETS, DIL_GROUPS, DIL_HEADS)
    parts = [dilated_group(qkv, pos, tabs[:, g, :], g, f"dil_attn_{tag}_g{g}") for g in range(DIL_GROUPS)]
    o = dilated_mix(parts, f"dil_mix_{tag}")
    return matmul(o, w_o, F32, f"dil_out_{tag}", lead=(j,))


def kernel(x, positions, rel_bias, sb_w_qkv, sb_w_o, mla_w_q_a, mla_q_a_norm, mla_w_q_b, mla_w_kv_a,
           mla_kv_a_norm, mla_w_kv_b, mla_w_o, dil_w_qkv, dil_w_o, ln_gain, ln_bias,
           moe_w_group_router, moe_b_group_router, moe_w_expert_router, moe_b_expert_router,
           moe_w_gate, moe_w_up, moe_w_down):
    batch, seq, d = x.shape
    outs = []
    for b in range(batch):
        h = x[b]
        h_bf16 = h.astype(BF16)
        pos = positions[b]
        for i in range(DEPTH):
            kind, j = i % N_MIXERS, i // N_MIXERS
            tag = f"l{i}"
            if kind == 0:
                mix = stick_breaking_mixer(h_bf16, sb_w_qkv, sb_w_o, j, tag)
            elif kind == 1:
                mix = mla_mixer(h_bf16, pos, mla_w_q_a[j], mla_q_a_norm[j], mla_w_q_b[j], mla_w_kv_a[j],
                                mla_kv_a_norm[j], mla_w_kv_b, mla_w_o, j, tag)
            else:
                mix = dilated_mixer(h_bf16, pos, rel_bias, dil_w_qkv, dil_w_o, j, tag)
            h, h_bf16 = residual_layer_norm(h, mix, ln_gain[i, 0], ln_bias[i, 0], f"ln_mix_{tag}")
            ys, slot = hierarchical_moe(h, moe_w_group_router[i], moe_b_group_router[i],
                                        moe_w_expert_router[i], moe_b_expert_router[i],
                                        moe_w_gate, moe_w_up, moe_w_down, i, tag)
            h, h_bf16 = residual_layer_norm_gathered(h, ys, slot, ln_gain[i, 1], ln_bias[i, 1],
                                                     f"ln_ffn_{tag}")
        outs.append(h)
    return jnp.stack(outs)
```

```python
import functools
import math

import jax
import jax.numpy as jnp
from jax import lax
from jax.experimental import pallas as pl
from jax.experimental.pallas import tpu as pltpu

DEPTH = 4
N_MIXERS = 3
LN_EPS = 1e-5
RMS_EPS = 1e-6

SB_HEADS = 32
SB_HEAD_DIM = 128

MLA_HEADS = 32
MLA_Q_RANK = 1024
MLA_KV_RANK = 512
MLA_NOPE_DIM = 128
MLA_ROPE_DIM = 64
MLA_V_DIM = 128
ROPE_THETA = 10000.0

DIL_DILATIONS = (1, 4, 16)
DIL_GROUPS = 3
DIL_HEADS = 16
DIL_HEAD_DIM = 128
DIL_WINDOW_KEYS = 128

REL_BUCKETS = 32
REL_MAX_DIST = 2048

MOE_GROUPS = 4
MOE_EXPERTS_PER_GROUP = 8
EXPERT_HIDDEN = 256

DN_ALPHA = (2 * DEPTH) ** 0.25

LANES = 128
VMEM_LIMIT_BYTES = 56 * 1024 * 1024
MATMUL_VMEM_BUDGET = 40 * 1024 * 1024
NEG_BIG = -1e30
EXP2_UNDERFLOW = -151.0
SB_HEADS_PER_STEP = 2
MLA_HEADS_PER_STEP = 2
MOE_TILE = 256
ROUTE_RANK_LANE = MOE_GROUPS * MOE_EXPERTS_PER_GROUP
ROUTE_GROUP_LANE = ROUTE_RANK_LANE + 1

BF16 = jnp.bfloat16
F32 = jnp.float32

_NT = (((1,), (1,)), ((), ()))


def _params(*sem):
    return pltpu.CompilerParams(dimension_semantics=sem, vmem_limit_bytes=VMEM_LIMIT_BYTES)


def _mm_kernel(x_ref, w_ref, o_ref, wbf_ref):
    @pl.when(pl.program_id(1) == 0)
    def _():
        wbf_ref[...] = w_ref[...].astype(BF16)

    o_ref[...] = jnp.dot(x_ref[...], wbf_ref[...], preferred_element_type=F32).astype(o_ref.dtype)


def _mm_tiles(m, k, n, out_bytes):
    for tm, tn in ((1024, 512), (512, 512), (512, 256), (256, 256), (256, 128), (128, 128)):
        if m % tm or n % tn:
            continue
        need = 2 * tm * k * 2 + 2 * k * tn * 4 + k * tn * 2 + 2 * tm * tn * out_bytes
        if need <= MATMUL_VMEM_BUDGET:
            return tm, tn
    raise ValueError(f"no matmul tiling for {(m, k, n)}")


def matmul(x, w, out_dtype, name, lead=(), n_out=None, col_block=None):
    m, k = x.shape
    n = w.shape[-1] if n_out is None else n_out
    tm, tn = _mm_tiles(m, k, n, jnp.dtype(out_dtype).itemsize)
    wcol = (lambda j: j) if col_block is None else (lambda j: col_block(j, tn))
    return pl.pallas_call(
        _mm_kernel,
        grid=(n // tn, m // tm),
        in_specs=[pl.BlockSpec((tm, k), lambda j, i: (i, 0)),
                  pl.BlockSpec((None,) * len(lead) + (k, tn), lambda j, i: tuple(lead) + (0, wcol(j)))],
        out_specs=pl.BlockSpec((tm, tn), lambda j, i: (i, j)),
        out_shape=jax.ShapeDtypeStruct((m, n), out_dtype),
        scratch_shapes=[pltpu.VMEM((k, tn), BF16)],
        compiler_params=_params("arbitrary", "arbitrary"),
        name=name,
    )(x, w)


def _ln_kernel(h_ref, mix_ref, g_ref, b_ref, of_ref, ob_ref, *strided_refs, dilations=(), stage_ref=None):
    x = DN_ALPHA * h_ref[...] + mix_ref[...]
    mu = jnp.mean(x, axis=-1, keepdims=True)
    xc = x - mu
    var = jnp.mean(xc * xc, axis=-1, keepdims=True)
    y = xc * lax.rsqrt(var + LN_EPS) * g_ref[...] + b_ref[...]
    of_ref[...] = y
    ob_ref[...] = y.astype(BF16)
    if not dilations:
        return
    rows, width = y.shape
    slabs = [slice(c * LANES, (c + 1) * LANES) for c in range(width // LANES)]
    for c, lanes in enumerate(slabs):
        stage_ref[c] = y[:, lanes]
    for ref, d in zip(strided_refs, dilations):
        for r in range(d):
            for c, lanes in enumerate(slabs):
                ref[r, :, lanes] = stage_ref.at[c][pl.ds(r, rows // d, stride=d), :].astype(BF16)


def residual_layer_norm(h, mix, g, b, name):
    s, d = h.shape
    tm = 256
    row = pl.BlockSpec((tm, d), lambda i: (i, 0))
    vec = pl.BlockSpec((1, d), lambda i: (0, 0))
    return pl.pallas_call(
        _ln_kernel,
        grid=(s // tm,),
        in_specs=[row, row, vec, vec],
        out_specs=[row, row],
        out_shape=[jax.ShapeDtypeStruct((s, d), F32), jax.ShapeDtypeStruct((s, d), BF16)],
        compiler_params=_params("arbitrary"),
        name=name,
    )(h, mix, g.reshape(1, d), b.reshape(1, d))


def _sb_kernel(q_ref, k_ref, v_ref, o_ref, acc_ref, carry_ref, *, t, hps, c2):
    qi = pl.program_id(1)
    dh = SB_HEAD_DIM
    heads = [slice(c * dh, (c + 1) * dh) for c in range(hps)]
    q_neg = [-q_ref[:, hs] for hs in heads]
    acc_ref[...] = jnp.zeros_like(acc_ref)
    carry_ref[...] = jnp.zeros_like(carry_ref)
    r = lax.broadcasted_iota(jnp.int32, (t, t), 0)
    c = lax.broadcasted_iota(jnp.int32, (t, t), 1)
    strict = c < r
    later_mask = jnp.where(r > c, 1.0, 0.0).astype(BF16)
    later_mask2 = jnp.concatenate([later_mask, later_mask], axis=0)

    def step(j, masked):
        ks = pl.multiple_of(j * t, t)
        for ci, hs in enumerate(heads):
            k = k_ref[pl.ds(ks, t), hs]
            v = v_ref[pl.ds(ks, t), hs]
            nz = lax.dot_general(q_neg[ci], k, _NT, preferred_element_type=F32) * c2
            log_keep = jnp.minimum(nz, 0.0) - jnp.log2(1.0 + jnp.exp2(-jnp.abs(nz)))
            if masked:
                log_keep = jnp.where(strict, log_keep, 0.0)
            hi = pltpu.bitcast(pltpu.bitcast(log_keep, jnp.uint32) & jnp.uint32(0xFFFF0000), F32)
            hi_lo = jnp.concatenate([hi.astype(BF16), (log_keep - hi).astype(BF16)], axis=1)
            later = jnp.dot(hi_lo, later_mask2, preferred_element_type=F32)
            carry = carry_ref[ci]
            tot = later + jnp.concatenate([carry] * (t // LANES), axis=1)
            a = jnp.exp2(log_keep - nz + tot)
            if masked:
                a = jnp.where(strict, a, 0.0)
            acc_ref[ci] += jnp.dot(a.astype(BF16), v, preferred_element_type=F32)
            carry_ref[ci] = carry + jnp.sum(log_keep, axis=1, keepdims=True)

    def largest_carry():
        m = carry_ref[0]
        for ci in range(1, hps):
            m = jnp.maximum(m, carry_ref[ci])
        return jnp.max(m)

    step(qi, True)

    def cond(state):
        j, top = state
        return jnp.logical_and(j >= 0, top >= EXP2_UNDERFLOW)

    def body(state):
        j, _ = state
        step(j, False)
        return j - 1, largest_carry()

    lax.while_loop(cond, body, (qi - 1, largest_carry()))
    for ci, hs in enumerate(heads):
        o_ref[:, hs] = acc_ref[ci].astype(o_ref.dtype)


def sb_attention(qkv, heads, name):
    s = qkv.shape[0]
    dh = SB_HEAD_DIM
    t = min(256, s)
    hps = SB_HEADS_PER_STEP
    groups = heads // hps
    kern = functools.partial(_sb_kernel, t=t, hps=hps, c2=dh ** -0.5 / math.log(2.0))
    return pl.pallas_call(
        kern,
        grid=(groups, s // t),
        in_specs=[pl.BlockSpec((t, hps * dh), lambda h, i: (i, h)),
                  pl.BlockSpec((s, hps * dh), lambda h, i: (0, groups + h)),
                  pl.BlockSpec((s, hps * dh), lambda h, i: (0, 2 * groups + h))],
        out_specs=pl.BlockSpec((t, hps * dh), lambda h, i: (i, h)),
        out_shape=jax.ShapeDtypeStruct((s, heads * dh), BF16),
        scratch_shapes=[pltpu.VMEM((hps, t, dh), F32), pltpu.VMEM((hps, t, LANES), F32)],
        compiler_params=_params("arbitrary", "arbitrary"),
        name=name,
    )(qkv, qkv, qkv)


def _rope_lanes(x, cos, sin_up, sin_dn):
    half = MLA_ROPE_DIM // 2
    return (x * cos + pltpu.roll(x, half, 1) * sin_up + pltpu.roll(x, LANES - half, 1) * sin_dn)


def _mla_prep_kernel(a_ref, gq_ref, gkv_ref, cos_ref, sup_ref, sdn_ref, cq_ref, ckv_ref, kr_ref):
    a = a_ref[...]
    rq, rkv = MLA_Q_RANK, MLA_KV_RANK
    cq = a[:, :rq]
    cq_ref[...] = (cq * lax.rsqrt(jnp.mean(cq * cq, axis=-1, keepdims=True) + RMS_EPS)
                   * gq_ref[...]).astype(BF16)
    ckv = a[:, rq:rq + rkv]
    ckv_ref[...] = (ckv * lax.rsqrt(jnp.mean(ckv * ckv, axis=-1, keepdims=True) + RMS_EPS)
                    * gkv_ref[...]).astype(BF16)
    kr = a[:, rq + rkv:rq + rkv + LANES]
    kr_ref[...] = _rope_lanes(kr, cos_ref[...], sup_ref[...], sdn_ref[...]).astype(BF16)


def mla_prep(a, gq, gkv, cos, sin_up, sin_dn, name):
    s, n = a.shape
    tm = 512
    row = lambda w: pl.BlockSpec((tm, w), lambda i: (i, 0))
    vec = lambda w: pl.BlockSpec((1, w), lambda i: (0, 0))
    return pl.pallas_call(
        _mla_prep_kernel,
        grid=(s // tm,),
        in_specs=[row(n), vec(MLA_Q_RANK), vec(MLA_KV_RANK), row(LANES), row(LANES), row(LANES)],
        out_specs=[row(MLA_Q_RANK), row(MLA_KV_RANK), row(LANES)],
        out_shape=[jax.ShapeDtypeStruct((s, MLA_Q_RANK), BF16),
                   jax.ShapeDtypeStruct((s, MLA_KV_RANK), BF16),
                   jax.ShapeDtypeStruct((s, LANES), BF16)],
        compiler_params=_params("arbitrary"),
        name=name,
    )(a, gq.reshape(1, -1), gkv.reshape(1, -1), cos, sin_up, sin_dn)


def _mla_kernel(q_ref, kv_ref, kr_ref, cos_ref, sup_ref, sdn_ref, o_ref,
                m_ref, l_ref, acc_ref, *, t, hps, c2):
    qi = pl.program_id(1)
    wide = 2 * LANES
    cos, sin_up, sin_dn = cos_ref[...], sup_ref[...], sdn_ref[...]
    qc = []
    for ci in range(hps):
        q = q_ref[:, ci * wide:(ci + 1) * wide]
        q_rope = _rope_lanes(q[:, LANES:], cos, sin_up, sin_dn)
        qc.append(jnp.concatenate([q[:, :LANES].astype(BF16), q_rope.astype(BF16)], axis=1))
    m_ref[...] = jnp.full_like(m_ref, NEG_BIG)
    l_ref[...] = jnp.zeros_like(l_ref)
    acc_ref[...] = jnp.zeros_like(acc_ref)
    causal = (lax.broadcasted_iota(jnp.int32, (t, t), 1) <= lax.broadcasted_iota(jnp.int32, (t, t), 0))

    def step(j, masked):
        ks = pl.multiple_of(j * t, t)
        kr = kr_ref[pl.ds(ks, t), :]
        for ci in range(hps):
            kn = kv_ref[pl.ds(ks, t), ci * wide:ci * wide + LANES]
            v = kv_ref[pl.ds(ks, t), ci * wide + LANES:(ci + 1) * wide]
            s = lax.dot_general(qc[ci], jnp.concatenate([kn, kr], axis=1), _NT,
                                preferred_element_type=F32) * c2
            if masked:
                s = jnp.where(causal, s, NEG_BIG)
            m_prev = m_ref[ci]
            m_new = jnp.maximum(m_prev, jnp.max(s, axis=1, keepdims=True))
            alpha = jnp.exp2(m_prev - m_new)
            p = jnp.exp2(s - jnp.concatenate([m_new] * (t // LANES), axis=1))
            l_ref[ci] = alpha * l_ref[ci] + jnp.sum(p, axis=1, keepdims=True)
            acc_ref[ci] = alpha * acc_ref[ci] + jnp.dot(p.astype(BF16), v, preferred_element_type=F32)
            m_ref[ci] = m_new

    def body(j, _):
        step(j, False)
        return 0

    lax.fori_loop(0, qi, body, 0)
    step(qi, True)
    for ci in range(hps):
        o_ref[:, ci * LANES:(ci + 1) * LANES] = (acc_ref[ci] / l_ref[ci]).astype(o_ref.dtype)


def mla_attention_core(q, kv, k_rope, cos, sin_up, sin_dn, heads, name):
    assert MLA_V_DIM == LANES and MLA_NOPE_DIM == LANES
    s = q.shape[0]
    t = min(512, s)
    hps = MLA_HEADS_PER_STEP
    kern = functools.partial(_mla_kernel, t=t, hps=hps,
                             c2=(MLA_NOPE_DIM + MLA_ROPE_DIM) ** -0.5 / math.log(2.0))
    tab = pl.BlockSpec((t, LANES), lambda h, i: (i, 0))
    return pl.pallas_call(
        kern,
        grid=(heads // hps, s // t),
        in_specs=[pl.BlockSpec((t, hps * 2 * LANES), lambda h, i: (i, h)),
                  pl.BlockSpec((s, hps * 2 * LANES), lambda h, i: (0, h)),
                  pl.BlockSpec((s, LANES), lambda h, i: (0, 0)),
                  tab, tab, tab],
        out_specs=pl.BlockSpec((t, hps * LANES), lambda h, i: (i, h)),
        out_shape=jax.ShapeDtypeStruct((s, heads * MLA_V_DIM), BF16),
        scratch_shapes=[pltpu.VMEM((hps, t, LANES), F32), pltpu.VMEM((hps, t, LANES), F32),
                        pltpu.VMEM((hps, t, LANES), F32)],
        compiler_params=_params("arbitrary", "arbitrary"),
        name=name,
    )(q, kv, k_rope, cos, sin_up, sin_dn)


def _t5_bucket(rel):
    n = jnp.maximum(rel, 0)
    max_exact = REL_BUCKETS // 2
    nf = jnp.maximum(n, 1).astype(F32)
    large = max_exact + (jnp.log(nf / max_exact) / math.log(REL_MAX_DIST / max_exact)
                         * (REL_BUCKETS - max_exact)).astype(jnp.int32)
    large = jnp.minimum(large, REL_BUCKETS - 1)
    return jnp.where(n < max_exact, n, large)


def _dil_kernel(tab_ref, q_ref, kp_ref, kc_ref, vp_ref, vc_ref, pq_ref, pkp_ref, pkc_ref,
                o_ref, lse_ref, *, heads, scale):
    a = pl.program_id(1)
    w = DIL_WINDOW_KEYS
    dh = DIL_HEAD_DIM
    rel = pq_ref[...] - jnp.concatenate([pkp_ref[...], pkc_ref[...]], axis=1)
    back = (lax.broadcasted_iota(jnp.int32, (w, 2 * w), 0) + w
            - lax.broadcasted_iota(jnp.int32, (w, 2 * w), 1))
    col = lax.broadcasted_iota(jnp.int32, (w, 2 * w), 1)
    first_col = jnp.where(a > 0, 0, w)
    valid = (back >= 0) & (back <= w) & (col >= first_col)
    bucket = _t5_bucket(rel)
    for h in range(heads):
        hs = slice(h * dh, (h + 1) * dh)
        bias = jnp.zeros((w, 2 * w), F32)
        for b in range(REL_BUCKETS):
            bias = jnp.where(bucket == b, tab_ref[b, h], bias)
        k = jnp.concatenate([kp_ref[:, hs], kc_ref[:, hs]], axis=0)
        v = jnp.concatenate([vp_ref[:, hs], vc_ref[:, hs]], axis=0)
        s = lax.dot_general(q_ref[:, hs], k, _NT, preferred_element_type=F32) * scale + bias
        s = jnp.where(valid, s, NEG_BIG)
        m = jnp.max(s, axis=1, keepdims=True)
        p = jnp.exp(s - m)
        l = jnp.sum(p, axis=1, keepdims=True)
        o_ref[:, hs] = jnp.dot(p.astype(BF16), v, preferred_element_type=F32) / l
        lse_ref[:, hs] = jnp.broadcast_to(m + jnp.log(l), (w, dh))


def dilated_group(qkv, pos, tab, g, name):
    s, n = qkv.shape
    d = DIL_DILATIONS[g]
    heads, dh, w = DIL_HEADS, DIL_HEAD_DIM, DIL_WINDOW_KEYS
    hw = heads * dh
    sub = s // d
    nb = sub // w
    pos_col = pos.reshape(sub, d).T.reshape(d, sub, 1)
    pos_row = pos.reshape(sub, d).T.reshape(d, 1, sub)
    prev = lambda i: jnp.maximum(i - 1, 0)
    kern = functools.partial(_dil_kernel, heads=heads, scale=dh ** -0.5)
    return pl.pallas_call(
        kern,
        grid=(d, nb),
        in_specs=[pl.BlockSpec(memory_space=pltpu.SMEM),
                  pl.BlockSpec((w, hw), lambda r, i: (r * nb + i, 0)),
                  pl.BlockSpec((w, hw), lambda r, i: (r * nb + prev(i), 1)),
                  pl.BlockSpec((w, hw), lambda r, i: (r * nb + i, 1)),
                  pl.BlockSpec((w, hw), lambda r, i: (r * nb + prev(i), 2)),
                  pl.BlockSpec((w, hw), lambda r, i: (r * nb + i, 2)),
                  pl.BlockSpec((None, w, 1), lambda r, i: (r, i, 0)),
                  pl.BlockSpec((None, 1, w), lambda r, i: (r, 0, prev(i))),
                  pl.BlockSpec((None, 1, w), lambda r, i: (r, 0, i))],
        out_specs=[pl.BlockSpec((w, hw), lambda r, i: (r * nb + i, 0)),
                   pl.BlockSpec((w, hw), lambda r, i: (r * nb + i, 0))],
        out_shape=[jax.ShapeDtypeStruct((s, hw), F32), jax.ShapeDtypeStruct((s, hw), F32)],
        compiler_params=_params("arbitrary", "arbitrary"),
        name=name,
    )(tab, qkv, qkv, qkv, qkv, qkv, pos_col, pos_row, pos_row)


def _dil_mix_kernel(*refs, dilations):
    n = len(dilations)
    out_ref = refs[2 * n]
    buffers = list(refs[2 * n + 1:])
    outs, lses = [], []
    for g, d in enumerate(dilations):
        o_ref, l_ref = refs[2 * g], refs[2 * g + 1]
        if d == 1:
            outs.append(o_ref[0])
            lses.append(l_ref[0])
            continue
        o_buf, l_buf = buffers.pop(0), buffers.pop(0)
        slabs, rows = o_buf.shape[0], o_buf.shape[1] // d
        for r in range(d):
            for c in range(slabs):
                lanes = slice(c * LANES, (c + 1) * LANES)
                o_buf.at[c][pl.ds(r, rows, stride=d), :] = o_ref[r, :, lanes]
                l_buf.at[c][pl.ds(r, rows, stride=d), :] = l_ref[r, :, lanes]
        outs.append(jnp.concatenate([o_buf[c] for c in range(slabs)], axis=1))
        lses.append(jnp.concatenate([l_buf[c] for c in range(slabs)], axis=1))
    m = functools.reduce(jnp.maximum, lses)
    es = [jnp.exp(l - m) for l in lses]
    num = functools.reduce(lambda a, b: a + b, [e * o for e, o in zip(es, outs)])
    out_ref[...] = (num / functools.reduce(lambda a, b: a + b, es)).astype(out_ref.dtype)


def dilated_mix(parts, name):
    s, hw = parts[0][0].shape
    tm = 256
    flat, specs, scratch = [], [], []
    for (o, lse), d in zip(parts, DIL_DILATIONS):
        for x in (o, lse):
            flat.append(x.reshape(d, s // d, hw))
            specs.append(pl.BlockSpec((d, tm // d, hw), lambda i: (0, i, 0)))
        if d > 1:
            scratch += [pltpu.VMEM((hw // LANES, tm, LANES), F32)] * 2
    return pl.pallas_call(
        functools.partial(_dil_mix_kernel, dilations=DIL_DILATIONS),
        grid=(s // tm,),
        in_specs=specs,
        out_specs=pl.BlockSpec((tm, hw), lambda i: (i, 0)),
        out_shape=jax.ShapeDtypeStruct((s, hw), BF16),
        scratch_shapes=scratch,
        compiler_params=_params("arbitrary"),
        name=name,
    )(*flat)


def _router_kernel(x_ref, w_ref, b_ref, route_ref, count_ref, run_ref):
    ng, ne = MOE_GROUPS, MOE_EXPERTS_PER_GROUP
    logits = jnp.dot(x_ref[...], w_ref[...], preferred_element_type=F32,
                     precision=lax.Precision.HIGHEST) + b_ref[...]
    lane = lax.broadcasted_iota(jnp.int32, logits.shape, 1).astype(F32)
    big = float(1 << 20)
    is_group = (lane >= ng * ne) & (lane < ng * ne + ng)
    g_logit = jnp.where(is_group, logits, NEG_BIG)
    g_max = jnp.max(g_logit, axis=1, keepdims=True)
    g_top = jnp.min(jnp.where(is_group & (g_logit == g_max), lane, big), axis=1, keepdims=True) - ng * ne
    g_gate = 1.0 / jnp.sum(jnp.where(is_group, jnp.exp(g_logit - g_max), 0.0), axis=1, keepdims=True)
    in_group = (lane >= g_top * ne) & (lane < (g_top + 1) * ne)
    e_logit = jnp.where(in_group, logits, NEG_BIG)
    v1 = jnp.max(e_logit, axis=1, keepdims=True)
    i1 = jnp.min(jnp.where(in_group & (e_logit == v1), lane, big), axis=1, keepdims=True)
    rest = in_group & (lane != i1)
    e_rest = jnp.where(rest, logits, NEG_BIG)
    v2 = jnp.max(e_rest, axis=1, keepdims=True)
    i2 = jnp.min(jnp.where(rest & (e_rest == v2), lane, big), axis=1, keepdims=True)
    e21 = jnp.exp(v2 - v1)
    w1 = g_gate / (1.0 + e21)
    w2 = g_gate * e21 / (1.0 + e21)
    gates = jnp.where(lane == i1, w1, jnp.where(lane == i2, w2, 0.0))

    @pl.when(pl.program_id(0) == 0)
    def _():
        run_ref[...] = jnp.zeros_like(run_ref)

    tm = x_ref.shape[0]
    member = jnp.where(lane == g_top, 1.0, 0.0)
    earlier = (lax.broadcasted_iota(jnp.int32, (tm, tm), 1)
               < lax.broadcasted_iota(jnp.int32, (tm, tm), 0))
    before = jnp.dot(jnp.where(earlier, 1.0, 0.0).astype(BF16), member.astype(BF16),
                     preferred_element_type=F32)
    run = run_ref[...]
    rank = jnp.sum(jnp.where(lane == g_top, before + run, 0.0), axis=1, keepdims=True)
    run_ref[...] = run + jnp.sum(member, axis=0, keepdims=True)
    count_ref[...] = jnp.broadcast_to(run_ref[...], count_ref.shape)
    route_ref[...] = jnp.where(lane == ROUTE_RANK_LANE, rank,
                               jnp.where(lane == ROUTE_GROUP_LANE, g_top, gates))


def moe_route(h, w_router, b_router, name):
    s, d = h.shape
    tm = MOE_TILE
    return pl.pallas_call(
        _router_kernel,
        grid=(s // tm,),
        in_specs=[pl.BlockSpec((tm, d), lambda i: (i, 0)),
                  pl.BlockSpec((d, LANES), lambda i: (0, 0)),
                  pl.BlockSpec((1, LANES), lambda i: (0, 0))],
        out_specs=[pl.BlockSpec((tm, LANES), lambda i: (i, 0)),
                   pl.BlockSpec((8, LANES), lambda i: (0, 0))],
        out_shape=[jax.ShapeDtypeStruct((s, LANES), F32), jax.ShapeDtypeStruct((8, LANES), F32)],
        scratch_shapes=[pltpu.VMEM((1, LANES), F32)],
        compiler_params=_params("arbitrary"),
        name=name,
    )(h, w_router, b_router)


def _dispatch_kernel(pos_ref, h_ref, route_ref, xs_in, rs_in, xs_ref, rs_ref, sem):
    del xs_in, rs_in
    tm = h_ref.shape[0]
    base = pl.program_id(0) * tm

    def row_copies(r):
        dst = pos_ref[base + r]
        return (pltpu.make_async_copy(h_ref.at[pl.ds(r, 1), :], xs_ref.at[pl.ds(dst, 1), :], sem.at[0]),
                pltpu.make_async_copy(route_ref.at[pl.ds(r, 1), :], rs_ref.at[pl.ds(dst, 1), :], sem.at[1]))

    def start(r, carry):
        for cp in row_copies(r):
            cp.start()
        return carry

    def wait(r, carry):
        for cp in row_copies(r):
            cp.wait()
        return carry

    lax.fori_loop(0, tm, start, 0)
    lax.fori_loop(0, tm, wait, 0)


def moe_dispatch(h, route, pos, rows, name):
    s, d = h.shape
    tm = MOE_TILE
    grid_spec = pltpu.PrefetchScalarGridSpec(
        num_scalar_prefetch=1,
        grid=(s // tm,),
        in_specs=[pl.BlockSpec((tm, d), lambda i, pos: (i, 0)),
                  pl.BlockSpec((tm, LANES), lambda i, pos: (i, 0)),
                  pl.BlockSpec(memory_space=pl.ANY),
                  pl.BlockSpec(memory_space=pl.ANY)],
        out_specs=[pl.BlockSpec(memory_space=pl.ANY), pl.BlockSpec(memory_space=pl.ANY)],
        scratch_shapes=[pltpu.SemaphoreType.DMA((2,))],
    )
    return pl.pallas_call(
        _dispatch_kernel,
        grid_spec=grid_spec,
        out_shape=[jax.ShapeDtypeStruct((rows, d), F32), jax.ShapeDtypeStruct((rows, LANES), F32)],
        input_output_aliases={3: 0, 4: 1},
        compiler_params=_params("arbitrary"),
        name=name,
    )(pos, h, route, jnp.zeros((rows, d), F32), jnp.zeros((rows, LANES), F32))


def _weights_changed(tg_ref, t):
    return jnp.logical_or(t == 0, tg_ref[t] != tg_ref[jnp.maximum(t - 1, 0)])


def _group_up_kernel(tg_ref, x_ref, route_ref, wg_ref, wu_ref, o_ref, wg_bf, wu_bf):
    e = pl.program_id(0)
    t = pl.program_id(1)

    @pl.when(_weights_changed(tg_ref, t))
    def _():
        wg_bf[...] = wg_ref[...].astype(BF16)
        wu_bf[...] = wu_ref[...].astype(BF16)

    x = x_ref[...].astype(BF16)
    gate = jnp.dot(x, wg_bf[...], preferred_element_type=F32)
    up = jnp.dot(x, wu_bf[...], preferred_element_type=F32)
    lane = lax.broadcasted_iota(jnp.int32, route_ref.shape, 1)
    flat_expert = tg_ref[t] * MOE_EXPERTS_PER_GROUP + e
    comb = jnp.sum(jnp.where(lane == flat_expert, route_ref[...], 0.0), axis=1, keepdims=True)
    hidden = gate * jax.nn.sigmoid(gate) * up
    o_ref[...] = (hidden * comb).astype(o_ref.dtype)


def moe_group_up(xs, rs, tile_group, w_gate, w_up, layer, name):
    rows, d = xs.shape
    ne = MOE_EXPERTS_PER_GROUP
    f = w_gate.shape[-1]
    tm = MOE_TILE
    wspec = pl.BlockSpec((None, None, d, f), lambda e, t, tg: (layer, tg[t] * ne + e, 0, 0))
    grid_spec = pltpu.PrefetchScalarGridSpec(
        num_scalar_prefetch=1,
        grid=(ne, rows // tm),
        in_specs=[pl.BlockSpec((tm, d), lambda e, t, tg: (t, 0)),
                  pl.BlockSpec((tm, LANES), lambda e, t, tg: (t, 0)),
                  wspec, wspec],
        out_specs=pl.BlockSpec((tm, f), lambda e, t, tg: (t, e)),
        scratch_shapes=[pltpu.VMEM((d, f), BF16), pltpu.VMEM((d, f), BF16)],
    )
    return pl.pallas_call(
        _group_up_kernel,
        grid_spec=grid_spec,
        out_shape=jax.ShapeDtypeStruct((rows, ne * f), BF16),
        compiler_params=_params("arbitrary", "arbitrary"),
        name=name,
    )(tile_group, xs, rs, w_gate, w_up)


def _group_down_kernel(tg_ref, x_ref, w_ref, o_ref, w_bf):
    @pl.when(_weights_changed(tg_ref, pl.program_id(1)))
    def _():
        w_bf[...] = w_ref[...].astype(BF16)

    o_ref[...] = jnp.dot(x_ref[...], w_bf[...], preferred_element_type=F32)


def moe_group_down(hidden, tile_group, w_down, layer, name):
    rows, k = hidden.shape
    d = w_down.shape[-1]
    tm = MOE_TILE
    tn = min(1024, d)
    w4 = w_down.reshape(w_down.shape[0], MOE_GROUPS, k, d)
    grid_spec = pltpu.PrefetchScalarGridSpec(
        num_scalar_prefetch=1,
        grid=(d // tn, rows // tm),
        in_specs=[pl.BlockSpec((tm, k), lambda j, t, tg: (t, 0)),
                  pl.BlockSpec((None, None, k, tn), lambda j, t, tg: (layer, tg[t], 0, j))],
        out_specs=pl.BlockSpec((tm, tn), lambda j, t, tg: (t, j)),
        scratch_shapes=[pltpu.VMEM((k, tn), BF16)],
    )
    return pl.pallas_call(
        _group_down_kernel,
        grid_spec=grid_spec,
        out_shape=jax.ShapeDtypeStruct((rows, d), F32),
        compiler_params=_params("arbitrary", "arbitrary"),
        name=name,
    )(tile_group, hidden, w4)


def _ln_gather_kernel(pos_ref, h_ref, ys_ref, g_ref, b_ref, of_ref, ob_ref, *rest, dilations):
    strided_refs, (y_buf, sem, *stage) = rest[:len(dilations)], rest[len(dilations):]
    tm = h_ref.shape[0]
    base = pl.program_id(0) * tm

    def row_copy(r):
        return pltpu.make_async_copy(ys_ref.at[pl.ds(pos_ref[base + r], 1), :],
                                     y_buf.at[pl.ds(r, 1), :], sem.at[0])

    def start(r, carry):
        row_copy(r).start()
        return carry

    def wait(r, carry):
        row_copy(r).wait()
        return carry

    lax.fori_loop(0, tm, start, 0)
    lax.fori_loop(0, tm, wait, 0)
    _ln_kernel(h_ref, y_buf, g_ref, b_ref, of_ref, ob_ref, *strided_refs, dilations=dilations,
               stage_ref=stage[0] if stage else None)


def residual_layer_norm_gathered(h, ys, pos, g, b, name, dilations=()):
    s, d = h.shape
    tm = MOE_TILE
    row = pl.BlockSpec((tm, d), lambda i, pos: (i, 0))
    vec = pl.BlockSpec((1, d), lambda i, pos: (0, 0))
    grid_spec = pltpu.PrefetchScalarGridSpec(
        num_scalar_prefetch=1,
        grid=(s // tm,),
        in_specs=[row, pl.BlockSpec(memory_space=pl.ANY), vec, vec],
        out_specs=[row, row] + [pl.BlockSpec((dil, tm // dil, d), lambda i, pos: (0, i, 0))
                                for dil in dilations],
        scratch_shapes=[pltpu.VMEM((tm, d), F32), pltpu.SemaphoreType.DMA((1,))]
        + ([pltpu.VMEM((d // LANES, tm, LANES), F32)] if dilations else []),
    )
    return pl.pallas_call(
        functools.partial(_ln_gather_kernel, dilations=tuple(dilations)),
        grid_spec=grid_spec,
        out_shape=[jax.ShapeDtypeStruct((s, d), F32), jax.ShapeDtypeStruct((s, d), BF16)]
        + [jax.ShapeDtypeStruct((dil, s // dil, d), BF16) for dil in dilations],
        compiler_params=_params("arbitrary"),
        name=name,
    )(pos, h, ys, g.reshape(1, d), b.reshape(1, d))


def hierarchical_moe(h_f32, w_gr, b_gr, w_er, b_er, w_gate, w_up, w_down, layer, tag):
    s, d = h_f32.shape
    ng, ne = MOE_GROUPS, MOE_EXPERTS_PER_GROUP
    tm = MOE_TILE
    w_router = jnp.concatenate(
        [jnp.transpose(w_er, (1, 0, 2)).reshape(d, ng * ne), w_gr,
         jnp.zeros((d, LANES - ng * ne - ng), F32)], axis=1)
    b_router = jnp.concatenate(
        [b_er.reshape(ng * ne), b_gr, jnp.zeros((LANES - ng * ne - ng,), F32)]).reshape(1, LANES)
    route, counts = moe_route(h_f32, w_router, b_router, f"moe_router_{tag}")
    counts = counts[0, :ng].astype(jnp.int32)
    padded = (counts + tm - 1) // tm * tm
    ends = jnp.cumsum(padded)
    rank = route[:, ROUTE_RANK_LANE].astype(jnp.int32)
    group = route[:, ROUTE_GROUP_LANE].astype(jnp.int32)
    pos = (ends - padded)[group] + rank
    rows = s + ng * tm
    tile_start = jnp.arange(rows // tm, dtype=jnp.int32) * tm
    tile_group = jnp.minimum(jnp.sum(tile_start[:, None] >= ends[None, :], axis=1), ng - 1).astype(jnp.int32)
    xs, rs = moe_dispatch(h_f32, route, pos, rows, f"moe_dispatch_{tag}")
    hidden = moe_group_up(xs, rs, tile_group, w_gate, w_up, layer, f"moe_up_{tag}")
    ys = moe_group_down(hidden, tile_group, w_down, layer, f"moe_down_{tag}")
    return ys, pos


def stick_breaking_mixer(h_bf16, w_qkv, w_o, j, tag):
    qkv = matmul(h_bf16, w_qkv, BF16, f"sb_qkv_{tag}", lead=(j,))
    o = sb_attention(qkv, SB_HEADS, f"sb_attn_{tag}")
    return matmul(o, w_o, F32, f"sb_out_{tag}", lead=(j,))


def _rope_tables(pos):
    half = MLA_ROPE_DIM // 2
    inv_freq = ROPE_THETA ** (-jnp.arange(half, dtype=F32) / half)
    ang = pos.astype(F32)[:, None] * inv_freq
    cos, sin = jnp.cos(ang), jnp.sin(ang)
    zeros = jnp.zeros((pos.shape[0], LANES - 2 * half), F32)
    z_half = jnp.zeros_like(sin)
    cos_t = jnp.concatenate([cos, cos, zeros], axis=1)
    sin_up = jnp.concatenate([z_half, sin, zeros], axis=1)
    sin_dn = jnp.concatenate([-sin, z_half, zeros], axis=1)
    return cos_t, sin_up, sin_dn


def mla_mixer(h_bf16, pos, w_q_a, q_a_norm, w_q_b, w_kv_a, kv_a_norm, w_kv_b, w_o, j, tag):
    d = h_bf16.shape[1]
    heads = MLA_HEADS
    nope, rope = MLA_NOPE_DIM, MLA_ROPE_DIM
    used = MLA_Q_RANK + MLA_KV_RANK + rope
    width = -(-(MLA_Q_RANK + MLA_KV_RANK + LANES) // 512) * 512
    w_a = jnp.concatenate([w_q_a, w_kv_a, jnp.zeros((d, width - used), F32)], axis=1)
    a = matmul(h_bf16, w_a, F32, f"mla_a_{tag}")
    cos_t, sin_up, sin_dn = _rope_tables(pos)
    cq, ckv, k_rope = mla_prep(a, q_a_norm, kv_a_norm, cos_t, sin_up, sin_dn, f"mla_prep_{tag}")
    w_qb = jnp.pad(w_q_b.reshape(MLA_Q_RANK, heads, nope + rope),
                   ((0, 0), (0, 0), (0, 2 * LANES - nope - rope))).reshape(MLA_Q_RANK, heads * 2 * LANES)
    q = matmul(cq, w_qb, F32, f"mla_qb_{tag}")
    kv = matmul(ckv, w_kv_b, BF16, f"mla_kvb_{tag}", lead=(j,))
    o = mla_attention_core(q, kv, k_rope, cos_t, sin_up, sin_dn, heads, f"mla_attn_{tag}")
    return matmul(o, w_o, F32, f"mla_out_{tag}", lead=(j,))


def dilated_mixer(h_by_dilation, pos, rel_bias, w_qkv, w_o, j, tag):
    groups = DIL_GROUPS
    hw = DIL_HEADS * DIL_HEAD_DIM
    tabs = rel_bias.reshape(REL_BUCKETS, groups, DIL_HEADS)
    parts = []
    for g, d in enumerate(DIL_DILATIONS):
        def col_block(jt, tn, g=g):
            per = hw // tn
            return ((jt // per) * groups + g) * per + jt % per
        qkv = matmul(h_by_dilation[d], w_qkv, BF16, f"dil_qkv_{tag}_g{g}", lead=(j,),
                     n_out=3 * hw, col_block=col_block)
        parts.append(dilated_group(qkv, pos, tabs[:, g, :], g, f"dil_attn_{tag}_g{g}"))
    o = dilated_mix(parts, f"dil_mix_{tag}")
    return matmul(o, w_o, F32, f"dil_out_{tag}", lead=(j,))


def kernel(x, positions, rel_bias, sb_w_qkv, sb_w_o, mla_w_q_a, mla_q_a_norm, mla_w_q_b, mla_w_kv_a,
           mla_kv_a_norm, mla_w_kv_b, mla_w_o, dil_w_qkv, dil_w_o, ln_gain, ln_bias,
           moe_w_group_router, moe_b_group_router, moe_w_expert_router, moe_b_expert_router,
           moe_w_gate, moe_w_up, moe_w_down):
    batch, seq, d = x.shape
    outs = []
    for b in range(batch):
        h = x[b]
        h_bf16 = h.astype(BF16)
        pos = positions[b]
        strided = ()
        assert N_MIXERS > 2 and DEPTH > 0
        for i in range(DEPTH):
            kind, j = i % N_MIXERS, i // N_MIXERS
            tag = f"l{i}"
            if kind == 0:
                mix = stick_breaking_mixer(h_bf16, sb_w_qkv, sb_w_o, j, tag)
            elif kind == 1:
                mix = mla_mixer(h_bf16, pos, mla_w_q_a[j], mla_q_a_norm[j], mla_w_q_b[j], mla_w_kv_a[j],
                                mla_kv_a_norm[j], mla_w_kv_b, mla_w_o, j, tag)
            else:
                h_by_dilation = {1: h_bf16}
                h_by_dilation.update({dil: hs.reshape(seq, d) for dil, hs in zip(next_dilations, strided)})
                mix = dilated_mixer(h_by_dilation, pos, rel_bias, dil_w_qkv, dil_w_o, j, tag)
            h, h_bf16 = residual_layer_norm(h, mix, ln_gain[i, 0], ln_bias[i, 0], f"ln_mix_{tag}")
            ys, slot = hierarchical_moe(h, moe_w_group_router[i], moe_b_group_router[i],
                                        moe_w_expert_router[i], moe_b_expert_router[i],
                                        moe_w_gate, moe_w_up, moe_w_down, i, tag)
            feeds_dilated = i + 1 < DEPTH and (i + 1) % N_MIXERS == 2
            next_dilations = tuple(dil for dil in DIL_DILATIONS if dil > 1) if feeds_dilated else ()
            h, h_bf16, *strided = residual_layer_norm_gathered(h, ys, slot, ln_gain[i, 1], ln_bias[i, 1],
                                                               f"ln_ffn_{tag}", dilations=next_dilations)
        outs.append(h)
    return jnp.stack(outs)
```

```python
import functools
import math

import jax
import jax.numpy as jnp
from jax import lax
from jax.experimental import pallas as pl
from jax.experimental.pallas import tpu as pltpu

DEPTH = 4
N_MIXERS = 3
LN_EPS = 1e-5
RMS_EPS = 1e-6

SB_HEADS = 32
SB_HEAD_DIM = 128

MLA_HEADS = 32
MLA_Q_RANK = 1024
MLA_KV_RANK = 512
MLA_NOPE_DIM = 128
MLA_ROPE_DIM = 64
MLA_V_DIM = 128
ROPE_THETA = 10000.0

DIL_DILATIONS = (1, 4, 16)
DIL_GROUPS = 3
DIL_HEADS = 16
DIL_HEAD_DIM = 128
DIL_WINDOW_KEYS = 128

REL_BUCKETS = 32
REL_MAX_DIST = 2048

MOE_GROUPS = 4
MOE_EXPERTS_PER_GROUP = 8
EXPERT_HIDDEN = 256

DN_ALPHA = (2 * DEPTH) ** 0.25

LANES = 128
VMEM_LIMIT_BYTES = 56 * 1024 * 1024
MATMUL_VMEM_BUDGET = 40 * 1024 * 1024
NEG_BIG = -1e30
EXP2_UNDERFLOW = -151.0
SB_HEADS_PER_STEP = 2
MLA_HEADS_PER_STEP = 2
MOE_TILE = 256
ROUTE_RANK_LANE = MOE_GROUPS * MOE_EXPERTS_PER_GROUP
ROUTE_GROUP_LANE = ROUTE_RANK_LANE + 1

BF16 = jnp.bfloat16
F32 = jnp.float32

_NT = (((1,), (1,)), ((), ()))


def _params(*sem):
    return pltpu.CompilerParams(dimension_semantics=sem, vmem_limit_bytes=VMEM_LIMIT_BYTES)


def _mm_kernel(x_ref, w_ref, o_ref, wbf_ref):
    @pl.when(pl.program_id(1) == 0)
    def _():
        wbf_ref[...] = w_ref[...].astype(BF16)

    o_ref[...] = jnp.dot(x_ref[...], wbf_ref[...], preferred_element_type=F32).astype(o_ref.dtype)


def _mm_tiles(m, k, n, out_bytes):
    for tm, tn in ((1024, 512), (512, 512), (512, 256), (256, 256), (256, 128), (128, 128)):
        if m % tm or n % tn:
            continue
        need = 2 * tm * k * 2 + 2 * k * tn * 4 + k * tn * 2 + 2 * tm * tn * out_bytes
        if need <= MATMUL_VMEM_BUDGET:
            return tm, tn
    raise ValueError(f"no matmul tiling for {(m, k, n)}")


def matmul(x, w, out_dtype, name, lead=(), n_out=None, col_block=None):
    m, k = x.shape
    n = w.shape[-1] if n_out is None else n_out
    tm, tn = _mm_tiles(m, k, n, jnp.dtype(out_dtype).itemsize)
    wcol = (lambda j: j) if col_block is None else (lambda j: col_block(j, tn))
    return pl.pallas_call(
        _mm_kernel,
        grid=(n // tn, m // tm),
        in_specs=[pl.BlockSpec((tm, k), lambda j, i: (i, 0)),
                  pl.BlockSpec((None,) * len(lead) + (k, tn), lambda j, i: tuple(lead) + (0, wcol(j)))],
        out_specs=pl.BlockSpec((tm, tn), lambda j, i: (i, j)),
        out_shape=jax.ShapeDtypeStruct((m, n), out_dtype),
        scratch_shapes=[pltpu.VMEM((k, tn), BF16)],
        compiler_params=_params("arbitrary", "arbitrary"),
        name=name,
    )(x, w)


def _ln_kernel(h_ref, mix_ref, g_ref, b_ref, of_ref, ob_ref, *strided_refs, dilations=(), stage_ref=None):
    x = DN_ALPHA * h_ref[...] + mix_ref[...]
    mu = jnp.mean(x, axis=-1, keepdims=True)
    xc = x - mu
    var = jnp.mean(xc * xc, axis=-1, keepdims=True)
    y = xc * lax.rsqrt(var + LN_EPS) * g_ref[...] + b_ref[...]
    of_ref[...] = y
    ob_ref[...] = y.astype(BF16)
    if not dilations:
        return
    rows, width = y.shape
    slabs = [slice(c * LANES, (c + 1) * LANES) for c in range(width // LANES)]
    for c, lanes in enumerate(slabs):
        stage_ref[c] = y[:, lanes]
    for ref, d in zip(strided_refs, dilations):
        for r in range(d):
            for c, lanes in enumerate(slabs):
                ref[r, :, lanes] = stage_ref.at[c][pl.ds(r, rows // d, stride=d), :].astype(BF16)


def residual_layer_norm(h, mix, g, b, name):
    s, d = h.shape
    tm = 256
    row = pl.BlockSpec((tm, d), lambda i: (i, 0))
    vec = pl.BlockSpec((1, d), lambda i: (0, 0))
    return pl.pallas_call(
        _ln_kernel,
        grid=(s // tm,),
        in_specs=[row, row, vec, vec],
        out_specs=[row, row],
        out_shape=[jax.ShapeDtypeStruct((s, d), F32), jax.ShapeDtypeStruct((s, d), BF16)],
        compiler_params=_params("arbitrary"),
        name=name,
    )(h, mix, g.reshape(1, d), b.reshape(1, d))


def _sb_kernel(q_ref, k_ref, v_ref, o_ref, acc_ref, carry_ref, *, t, hps, c2):
    qi = pl.program_id(1)
    dh = SB_HEAD_DIM
    heads = [slice(c * dh, (c + 1) * dh) for c in range(hps)]
    q_neg = [-q_ref[:, hs] for hs in heads]
    acc_ref[...] = jnp.zeros_like(acc_ref)
    carry_ref[...] = jnp.zeros_like(carry_ref)
    r = lax.broadcasted_iota(jnp.int32, (t, t), 0)
    c = lax.broadcasted_iota(jnp.int32, (t, t), 1)
    strict = c < r
    later_mask = jnp.where(r > c, 1.0, 0.0).astype(BF16)
    later_mask2 = jnp.concatenate([later_mask, later_mask], axis=0)

    def step(j, masked):
        ks = pl.multiple_of(j * t, t)
        for ci, hs in enumerate(heads):
            k = k_ref[pl.ds(ks, t), hs]
            v = v_ref[pl.ds(ks, t), hs]
            nz = lax.dot_general(q_neg[ci], k, _NT, preferred_element_type=F32) * c2
            log_keep = jnp.minimum(nz, 0.0) - jnp.log2(1.0 + jnp.exp2(-jnp.abs(nz)))
            if masked:
                log_keep = jnp.where(strict, log_keep, 0.0)
            hi = pltpu.bitcast(pltpu.bitcast(log_keep, jnp.uint32) & jnp.uint32(0xFFFF0000), F32)
            hi_lo = jnp.concatenate([hi.astype(BF16), (log_keep - hi).astype(BF16)], axis=1)
            later = jnp.dot(hi_lo, later_mask2, preferred_element_type=F32)
            carry = carry_ref[ci]
            tot = later + jnp.concatenate([carry] * (t // LANES), axis=1)
            a = jnp.exp2(log_keep - nz + tot)
            if masked:
                a = jnp.where(strict, a, 0.0)
            acc_ref[ci] += jnp.dot(a.astype(BF16), v, preferred_element_type=F32)
            carry_ref[ci] = carry + jnp.sum(log_keep, axis=1, keepdims=True)

    def largest_carry():
        m = carry_ref[0]
        for ci in range(1, hps):
            m = jnp.maximum(m, carry_ref[ci])
        return jnp.max(m)

    step(qi, True)

    def cond(state):
        j, top = state
        return jnp.logical_and(j >= 0, top >= EXP2_UNDERFLOW)

    def body(state):
        j, _ = state
        step(j, False)
        return j - 1, largest_carry()

    lax.while_loop(cond, body, (qi - 1, largest_carry()))
    for ci, hs in enumerate(heads):
        o_ref[:, hs] = acc_ref[ci].astype(o_ref.dtype)


def sb_attention(qkv, heads, name):
    s = qkv.shape[0]
    dh = SB_HEAD_DIM
    t = min(256, s)
    hps = SB_HEADS_PER_STEP
    groups = heads // hps
    kern = functools.partial(_sb_kernel, t=t, hps=hps, c2=dh ** -0.5 / math.log(2.0))
    return pl.pallas_call(
        kern,
        grid=(groups, s // t),
        in_specs=[pl.BlockSpec((t, hps * dh), lambda h, i: (i, h)),
                  pl.BlockSpec((s, hps * dh), lambda h, i: (0, groups + h)),
                  pl.BlockSpec((s, hps * dh), lambda h, i: (0, 2 * groups + h))],
        out_specs=pl.BlockSpec((t, hps * dh), lambda h, i: (i, h)),
        out_shape=jax.ShapeDtypeStruct((s, heads * dh), BF16),
        scratch_shapes=[pltpu.VMEM((hps, t, dh), F32), pltpu.VMEM((hps, t, LANES), F32)],
        compiler_params=_params("arbitrary", "arbitrary"),
        name=name,
    )(qkv, qkv, qkv)


def _rope_lanes(x, cos, sin_up, sin_dn):
    half = MLA_ROPE_DIM // 2
    return (x * cos + pltpu.roll(x, half, 1) * sin_up + pltpu.roll(x, LANES - half, 1) * sin_dn)


def _mla_prep_kernel(a_ref, gq_ref, gkv_ref, cos_ref, sup_ref, sdn_ref, cq_ref, ckv_ref, kr_ref):
    a = a_ref[...]
    rq, rkv = MLA_Q_RANK, MLA_KV_RANK
    cq = a[:, :rq]
    cq_ref[...] = (cq * lax.rsqrt(jnp.mean(cq * cq, axis=-1, keepdims=True) + RMS_EPS)
                   * gq_ref[...]).astype(BF16)
    ckv = a[:, rq:rq + rkv]
    ckv_ref[...] = (ckv * lax.rsqrt(jnp.mean(ckv * ckv, axis=-1, keepdims=True) + RMS_EPS)
                    * gkv_ref[...]).astype(BF16)
    kr = a[:, rq + rkv:rq + rkv + LANES]
    kr_ref[...] = _rope_lanes(kr, cos_ref[...], sup_ref[...], sdn_ref[...]).astype(BF16)


def mla_prep(a, gq, gkv, cos, sin_up, sin_dn, name):
    s, n = a.shape
    tm = 512
    row = lambda w: pl.BlockSpec((tm, w), lambda i: (i, 0))
    vec = lambda w: pl.BlockSpec((1, w), lambda i: (0, 0))
    return pl.pallas_call(
        _mla_prep_kernel,
        grid=(s // tm,),
        in_specs=[row(n), vec(MLA_Q_RANK), vec(MLA_KV_RANK), row(LANES), row(LANES), row(LANES)],
        out_specs=[row(MLA_Q_RANK), row(MLA_KV_RANK), row(LANES)],
        out_shape=[jax.ShapeDtypeStruct((s, MLA_Q_RANK), BF16),
                   jax.ShapeDtypeStruct((s, MLA_KV_RANK), BF16),
                   jax.ShapeDtypeStruct((s, LANES), BF16)],
        compiler_params=_params("arbitrary"),
        name=name,
    )(a, gq.reshape(1, -1), gkv.reshape(1, -1), cos, sin_up, sin_dn)


def _mla_kernel(q_ref, kv_ref, kr_ref, cos_ref, sup_ref, sdn_ref, o_ref,
                m_ref, l_ref, acc_ref, *, t, hps, c2):
    qi = pl.program_id(1)
    wide = 2 * LANES
    cos, sin_up, sin_dn = cos_ref[...], sup_ref[...], sdn_ref[...]
    qc = []
    for ci in range(hps):
        q = q_ref[:, ci * wide:(ci + 1) * wide]
        q_rope = _rope_lanes(q[:, LANES:], cos, sin_up, sin_dn)
        qc.append(jnp.concatenate([q[:, :LANES].astype(BF16), q_rope.astype(BF16)], axis=1))
    m_ref[...] = jnp.full_like(m_ref, NEG_BIG)
    l_ref[...] = jnp.zeros_like(l_ref)
    acc_ref[...] = jnp.zeros_like(acc_ref)
    causal = (lax.broadcasted_iota(jnp.int32, (t, t), 1) <= lax.broadcasted_iota(jnp.int32, (t, t), 0))

    def step(j, masked):
        ks = pl.multiple_of(j * t, t)
        kr = kr_ref[pl.ds(ks, t), :]
        for ci in range(hps):
            kn = kv_ref[pl.ds(ks, t), ci * wide:ci * wide + LANES]
            v = kv_ref[pl.ds(ks, t), ci * wide + LANES:(ci + 1) * wide]
            s = lax.dot_general(qc[ci], jnp.concatenate([kn, kr], axis=1), _NT,
                                preferred_element_type=F32) * c2
            if masked:
                s = jnp.where(causal, s, NEG_BIG)
            m_prev = m_ref[ci]
            m_new = jnp.maximum(m_prev, jnp.max(s, axis=1, keepdims=True))
            alpha = jnp.exp2(m_prev - m_new)
            p = jnp.exp2(s - jnp.concatenate([m_new] * (t // LANES), axis=1))
            l_ref[ci] = alpha * l_ref[ci] + jnp.sum(p, axis=1, keepdims=True)
            acc_ref[ci] = alpha * acc_ref[ci] + jnp.dot(p.astype(BF16), v, preferred_element_type=F32)
            m_ref[ci] = m_new

    def body(j, _):
        step(j, False)
        return 0

    lax.fori_loop(0, qi, body, 0)
    step(qi, True)
    for ci in range(hps):
        o_ref[:, ci * LANES:(ci + 1) * LANES] = (acc_ref[ci] / l_ref[ci]).astype(o_ref.dtype)


def mla_attention_core(q, kv, k_rope, cos, sin_up, sin_dn, heads, name):
    assert MLA_V_DIM == LANES and MLA_NOPE_DIM == LANES
    s = q.shape[0]
    t = min(512, s)
    hps = MLA_HEADS_PER_STEP
    kern = functools.partial(_mla_kernel, t=t, hps=hps,
                             c2=(MLA_NOPE_DIM + MLA_ROPE_DIM) ** -0.5 / math.log(2.0))
    tab = pl.BlockSpec((t, LANES), lambda h, i: (i, 0))
    return pl.pallas_call(
        kern,
        grid=(heads // hps, s // t),
        in_specs=[pl.BlockSpec((t, hps * 2 * LANES), lambda h, i: (i, h)),
                  pl.BlockSpec((s, hps * 2 * LANES), lambda h, i: (0, h)),
                  pl.BlockSpec((s, LANES), lambda h, i: (0, 0)),
                  tab, tab, tab],
        out_specs=pl.BlockSpec((t, hps * LANES), lambda h, i: (i, h)),
        out_shape=jax.ShapeDtypeStruct((s, heads * MLA_V_DIM), BF16),
        scratch_shapes=[pltpu.VMEM((hps, t, LANES), F32), pltpu.VMEM((hps, t, LANES), F32),
                        pltpu.VMEM((hps, t, LANES), F32)],
        compiler_params=_params("arbitrary", "arbitrary"),
        name=name,
    )(q, kv, k_rope, cos, sin_up, sin_dn)


def _t5_bucket(rel):
    n = jnp.maximum(rel, 0)
    max_exact = REL_BUCKETS // 2
    nf = jnp.maximum(n, 1).astype(F32)
    large = max_exact + (jnp.log(nf / max_exact) / math.log(REL_MAX_DIST / max_exact)
                         * (REL_BUCKETS - max_exact)).astype(jnp.int32)
    large = jnp.minimum(large, REL_BUCKETS - 1)
    return jnp.where(n < max_exact, n, large)


def _dil_kernel(tab_ref, q_ref, kp_ref, kc_ref, vp_ref, vc_ref, pq_ref, pkp_ref, pkc_ref,
                o_ref, lse_ref, *, heads, scale):
    a = pl.program_id(1)
    w = DIL_WINDOW_KEYS
    dh = DIL_HEAD_DIM
    rel = pq_ref[...] - jnp.concatenate([pkp_ref[...], pkc_ref[...]], axis=1)
    back = (lax.broadcasted_iota(jnp.int32, (w, 2 * w), 0) + w
            - lax.broadcasted_iota(jnp.int32, (w, 2 * w), 1))
    col = lax.broadcasted_iota(jnp.int32, (w, 2 * w), 1)
    first_col = jnp.where(a > 0, 0, w)
    valid = (back >= 0) & (back <= w) & (col >= first_col)
    bucket = _t5_bucket(rel)
    bucket_chunks = [bucket[:, c * LANES:(c + 1) * LANES] for c in range(2 * w // LANES)]
    for h in range(heads):
        hs = slice(h * dh, (h + 1) * dh)
        table = jnp.broadcast_to(tab_ref[h:h + 1, :], (w, LANES))
        bias = jnp.concatenate([jnp.take_along_axis(table, chunk, axis=1) for chunk in bucket_chunks], axis=1)
        k = jnp.concatenate([kp_ref[:, hs], kc_ref[:, hs]], axis=0)
        v = jnp.concatenate([vp_ref[:, hs], vc_ref[:, hs]], axis=0)
        s = lax.dot_general(q_ref[:, hs], k, _NT, preferred_element_type=F32) * scale + bias
        s = jnp.where(valid, s, NEG_BIG)
        m = jnp.max(s, axis=1, keepdims=True)
        p = jnp.exp(s - m)
        l = jnp.sum(p, axis=1, keepdims=True)
        o_ref[:, hs] = jnp.dot(p.astype(BF16), v, preferred_element_type=F32) / l
        lse_ref[:, hs] = jnp.broadcast_to(m + jnp.log(l), (w, dh))


def dilated_group(qkv, pos, tab, g, name):
    s, n = qkv.shape
    d = DIL_DILATIONS[g]
    heads, dh, w = DIL_HEADS, DIL_HEAD_DIM, DIL_WINDOW_KEYS
    hw = heads * dh
    sub = s // d
    nb = sub // w
    pos_col = pos.reshape(sub, d).T.reshape(d, sub, 1)
    pos_row = pos.reshape(sub, d).T.reshape(d, 1, sub)
    prev = lambda i: jnp.maximum(i - 1, 0)
    kern = functools.partial(_dil_kernel, heads=heads, scale=dh ** -0.5)
    return pl.pallas_call(
        kern,
        grid=(d, nb),
        in_specs=[pl.BlockSpec((heads, LANES), lambda r, i: (0, 0)),
                  pl.BlockSpec((w, hw), lambda r, i: (r * nb + i, 0)),
                  pl.BlockSpec((w, hw), lambda r, i: (r * nb + prev(i), 1)),
                  pl.BlockSpec((w, hw), lambda r, i: (r * nb + i, 1)),
                  pl.BlockSpec((w, hw), lambda r, i: (r * nb + prev(i), 2)),
                  pl.BlockSpec((w, hw), lambda r, i: (r * nb + i, 2)),
                  pl.BlockSpec((None, w, 1), lambda r, i: (r, i, 0)),
                  pl.BlockSpec((None, 1, w), lambda r, i: (r, 0, prev(i))),
                  pl.BlockSpec((None, 1, w), lambda r, i: (r, 0, i))],
        out_specs=[pl.BlockSpec((w, hw), lambda r, i: (r * nb + i, 0)),
                   pl.BlockSpec((w, hw), lambda r, i: (r * nb + i, 0))],
        out_shape=[jax.ShapeDtypeStruct((s, hw), F32), jax.ShapeDtypeStruct((s, hw), F32)],
        compiler_params=_params("arbitrary", "arbitrary"),
        name=name,
    )(tab, qkv, qkv, qkv, qkv, qkv, pos_col, pos_row, pos_row)


def _dil_mix_kernel(*refs, dilations):
    n = len(dilations)
    out_ref = refs[2 * n]
    buffers = list(refs[2 * n + 1:])
    outs, lses = [], []
    for g, d in enumerate(dilations):
        o_ref, l_ref = refs[2 * g], refs[2 * g + 1]
        if d == 1:
            outs.append(o_ref[0])
            lses.append(l_ref[0])
            continue
        o_buf, l_buf = buffers.pop(0), buffers.pop(0)
        slabs, rows = o_buf.shape[0], o_buf.shape[1] // d
        for r in range(d):
            for c in range(slabs):
                lanes = slice(c * LANES, (c + 1) * LANES)
                o_buf.at[c][pl.ds(r, rows, stride=d), :] = o_ref[r, :, lanes]
                l_buf.at[c][pl.ds(r, rows, stride=d), :] = l_ref[r, :, lanes]
        outs.append(jnp.concatenate([o_buf[c] for c in range(slabs)], axis=1))
        lses.append(jnp.concatenate([l_buf[c] for c in range(slabs)], axis=1))
    m = functools.reduce(jnp.maximum, lses)
    es = [jnp.exp(l - m) for l in lses]
    num = functools.reduce(lambda a, b: a + b, [e * o for e, o in zip(es, outs)])
    out_ref[...] = (num / functools.reduce(lambda a, b: a + b, es)).astype(out_ref.dtype)


def dilated_mix(parts, name):
    s, hw = parts[0][0].shape
    tm = 256
    flat, specs, scratch = [], [], []
    for (o, lse), d in zip(parts, DIL_DILATIONS):
        for x in (o, lse):
            flat.append(x.reshape(d, s // d, hw))
            specs.append(pl.BlockSpec((d, tm // d, hw), lambda i: (0, i, 0)))
        if d > 1:
            scratch += [pltpu.VMEM((hw // LANES, tm, LANES), F32)] * 2
    return pl.pallas_call(
        functools.partial(_dil_mix_kernel, dilations=DIL_DILATIONS),
        grid=(s // tm,),
        in_specs=specs,
        out_specs=pl.BlockSpec((tm, hw), lambda i: (i, 0)),
        out_shape=jax.ShapeDtypeStruct((s, hw), BF16),
        scratch_shapes=scratch,
        compiler_params=_params("arbitrary"),
        name=name,
    )(*flat)


def _router_kernel(x_ref, w_ref, b_ref, route_ref, count_ref, run_ref):
    ng, ne = MOE_GROUPS, MOE_EXPERTS_PER_GROUP
    x = x_ref[...]
    x_hi = x.astype(BF16)
    x_lo = (x - x_hi.astype(F32)).astype(BF16)
    logits = jnp.dot(jnp.concatenate([x_hi, x_lo, x_hi], axis=1), w_ref[...],
                     preferred_element_type=F32) + b_ref[...]
    lane = lax.broadcasted_iota(jnp.int32, logits.shape, 1).astype(F32)
    big = float(1 << 20)
    is_group = (lane >= ng * ne) & (lane < ng * ne + ng)
    g_logit = jnp.where(is_group, logits, NEG_BIG)
    g_max = jnp.max(g_logit, axis=1, keepdims=True)
    g_top = jnp.min(jnp.where(is_group & (g_logit == g_max), lane, big), axis=1, keepdims=True) - ng * ne
    g_gate = 1.0 / jnp.sum(jnp.where(is_group, jnp.exp(g_logit - g_max), 0.0), axis=1, keepdims=True)
    in_group = (lane >= g_top * ne) & (lane < (g_top + 1) * ne)
    e_logit = jnp.where(in_group, logits, NEG_BIG)
    v1 = jnp.max(e_logit, axis=1, keepdims=True)
    i1 = jnp.min(jnp.where(in_group & (e_logit == v1), lane, big), axis=1, keepdims=True)
    rest = in_group & (lane != i1)
    e_rest = jnp.where(rest, logits, NEG_BIG)
    v2 = jnp.max(e_rest, axis=1, keepdims=True)
    i2 = jnp.min(jnp.where(rest & (e_rest == v2), lane, big), axis=1, keepdims=True)
    e21 = jnp.exp(v2 - v1)
    w1 = g_gate / (1.0 + e21)
    w2 = g_gate * e21 / (1.0 + e21)
    gates = jnp.where(lane == i1, w1, jnp.where(lane == i2, w2, 0.0))

    @pl.when(pl.program_id(0) == 0)
    def _():
        run_ref[...] = jnp.zeros_like(run_ref)

    tm = x_ref.shape[0]
    member = jnp.where(lane == g_top, 1.0, 0.0)
    earlier = (lax.broadcasted_iota(jnp.int32, (tm, tm), 1)
               < lax.broadcasted_iota(jnp.int32, (tm, tm), 0))
    before = jnp.dot(jnp.where(earlier, 1.0, 0.0).astype(BF16), member.astype(BF16),
                     preferred_element_type=F32)
    run = run_ref[...]
    rank = jnp.sum(jnp.where(lane == g_top, before + run, 0.0), axis=1, keepdims=True)
    run_ref[...] = run + jnp.sum(member, axis=0, keepdims=True)
    count_ref[...] = jnp.broadcast_to(run_ref[...], count_ref.shape)
    route_ref[...] = jnp.where(lane == ROUTE_RANK_LANE, rank,
                               jnp.where(lane == ROUTE_GROUP_LANE, g_top, gates))


def moe_route(h, w_router, b_router, name):
    s, d = h.shape
    tm = MOE_TILE
    return pl.pallas_call(
        _router_kernel,
        grid=(s // tm,),
        in_specs=[pl.BlockSpec((tm, d), lambda i: (i, 0)),
                  pl.BlockSpec((3 * d, LANES), lambda i: (0, 0)),
                  pl.BlockSpec((1, LANES), lambda i: (0, 0))],
        out_specs=[pl.BlockSpec((tm, LANES), lambda i: (i, 0)),
                   pl.BlockSpec((8, LANES), lambda i: (0, 0))],
        out_shape=[jax.ShapeDtypeStruct((s, LANES), F32), jax.ShapeDtypeStruct((8, LANES), F32)],
        scratch_shapes=[pltpu.VMEM((1, LANES), F32)],
        compiler_params=_params("arbitrary"),
        name=name,
    )(h, w_router, b_router)


def _pack_bf16_pair(lo, hi):
    lo_bits = pltpu.bitcast(lo.astype(BF16).astype(F32), jnp.uint32) >> 16
    hi_bits = pltpu.bitcast(hi.astype(BF16).astype(F32), jnp.uint32) & jnp.uint32(0xFFFF0000)
    return lo_bits | hi_bits


def _unpack_bf16_pair(packed):
    lo = pltpu.bitcast(packed << 16, F32).astype(BF16)
    hi = pltpu.bitcast(packed & jnp.uint32(0xFFFF0000), F32).astype(BF16)
    return lo, hi


def _dispatch_kernel(pos_ref, h_ref, route_ref, xs_in, rs_in, xs_ref, rs_ref, packed_ref, sem):
    del xs_in, rs_in
    tm, d = h_ref.shape
    base = pl.program_id(0) * tm
    packed_ref[...] = _pack_bf16_pair(h_ref[:, :d // 2], h_ref[:, d // 2:])

    def row_copies(r):
        dst = pos_ref[base + r]
        return (pltpu.make_async_copy(packed_ref.at[pl.ds(r, 1), :], xs_ref.at[pl.ds(dst, 1), :], sem.at[0]),
                pltpu.make_async_copy(route_ref.at[pl.ds(r, 1), :], rs_ref.at[pl.ds(dst, 1), :], sem.at[1]))

    def start(r, carry):
        for cp in row_copies(r):
            cp.start()
        return carry

    def wait(r, carry):
        for cp in row_copies(r):
            cp.wait()
        return carry

    lax.fori_loop(0, tm, start, 0)
    lax.fori_loop(0, tm, wait, 0)


def moe_dispatch(h, route, pos, rows, name):
    s, d = h.shape
    tm = MOE_TILE
    grid_spec = pltpu.PrefetchScalarGridSpec(
        num_scalar_prefetch=1,
        grid=(s // tm,),
        in_specs=[pl.BlockSpec((tm, d), lambda i, pos: (i, 0)),
                  pl.BlockSpec((tm, LANES), lambda i, pos: (i, 0)),
                  pl.BlockSpec(memory_space=pl.ANY),
                  pl.BlockSpec(memory_space=pl.ANY)],
        out_specs=[pl.BlockSpec(memory_space=pl.ANY), pl.BlockSpec(memory_space=pl.ANY)],
        scratch_shapes=[pltpu.VMEM((tm, d // 2), jnp.uint32), pltpu.SemaphoreType.DMA((2,))],
    )
    return pl.pallas_call(
        _dispatch_kernel,
        grid_spec=grid_spec,
        out_shape=[jax.ShapeDtypeStruct((rows, d // 2), jnp.uint32), jax.ShapeDtypeStruct((rows, LANES), F32)],
        input_output_aliases={3: 0, 4: 1},
        compiler_params=_params("arbitrary"),
        name=name,
    )(pos, h, route, jnp.zeros((rows, d // 2), jnp.uint32), jnp.zeros((rows, LANES), F32))


def _weights_changed(tg_ref, t):
    return jnp.logical_or(t == 0, tg_ref[t] != tg_ref[jnp.maximum(t - 1, 0)])


def _group_up_kernel(tg_ref, x_ref, route_ref, wg_ref, wu_ref, o_ref, wg_bf, wu_bf):
    e = pl.program_id(0)
    t = pl.program_id(1)

    @pl.when(_weights_changed(tg_ref, t))
    def _():
        wg_bf[...] = wg_ref[...].astype(BF16)
        wu_bf[...] = wu_ref[...].astype(BF16)

    x = jnp.concatenate(_unpack_bf16_pair(x_ref[...]), axis=1)
    gate = jnp.dot(x, wg_bf[...], preferred_element_type=F32)
    up = jnp.dot(x, wu_bf[...], preferred_element_type=F32)
    lane = lax.broadcasted_iota(jnp.int32, route_ref.shape, 1)
    flat_expert = tg_ref[t] * MOE_EXPERTS_PER_GROUP + e
    comb = jnp.sum(jnp.where(lane == flat_expert, route_ref[...], 0.0), axis=1, keepdims=True)
    hidden = gate * jax.nn.sigmoid(gate) * up
    o_ref[...] = (hidden * comb).astype(o_ref.dtype)


def moe_group_up(xs, rs, tile_group, w_gate, w_up, layer, name):
    rows = xs.shape[0]
    ne = MOE_EXPERTS_PER_GROUP
    d, f = w_gate.shape[-2:]
    tm = MOE_TILE
    wspec = pl.BlockSpec((None, None, d, f), lambda e, t, tg: (layer, tg[t] * ne + e, 0, 0))
    grid_spec = pltpu.PrefetchScalarGridSpec(
        num_scalar_prefetch=1,
        grid=(ne, rows // tm),
        in_specs=[pl.BlockSpec((tm, d // 2), lambda e, t, tg: (t, 0)),
                  pl.BlockSpec((tm, LANES), lambda e, t, tg: (t, 0)),
                  wspec, wspec],
        out_specs=pl.BlockSpec((tm, f), lambda e, t, tg: (t, e)),
        scratch_shapes=[pltpu.VMEM((d, f), BF16), pltpu.VMEM((d, f), BF16)],
    )
    return pl.pallas_call(
        _group_up_kernel,
        grid_spec=grid_spec,
        out_shape=jax.ShapeDtypeStruct((rows, ne * f), BF16),
        compiler_params=_params("arbitrary", "arbitrary"),
        name=name,
    )(tile_group, xs, rs, w_gate, w_up)


def _group_down_kernel(tg_ref, x_ref, w_ref, o_ref, w_bf):
    @pl.when(_weights_changed(tg_ref, pl.program_id(1)))
    def _():
        w_bf[...] = w_ref[...].astype(BF16)

    o_ref[...] = jnp.dot(x_ref[...], w_bf[...], preferred_element_type=F32)


def moe_group_down(hidden, tile_group, w_down, layer, name):
    rows, k = hidden.shape
    d = w_down.shape[-1]
    tm = MOE_TILE
    tn = min(1024, d)
    w4 = w_down.reshape(w_down.shape[0], MOE_GROUPS, k, d)
    grid_spec = pltpu.PrefetchScalarGridSpec(
        num_scalar_prefetch=1,
        grid=(d // tn, rows // tm),
        in_specs=[pl.BlockSpec((tm, k), lambda j, t, tg: (t, 0)),
                  pl.BlockSpec((None, None, k, tn), lambda j, t, tg: (layer, tg[t], 0, j))],
        out_specs=pl.BlockSpec((tm, tn), lambda j, t, tg: (t, j)),
        scratch_shapes=[pltpu.VMEM((k, tn), BF16)],
    )
    return pl.pallas_call(
        _group_down_kernel,
        grid_spec=grid_spec,
        out_shape=jax.ShapeDtypeStruct((rows, d), F32),
        compiler_params=_params("arbitrary", "arbitrary"),
        name=name,
    )(tile_group, hidden, w4)


def _ln_gather_kernel(pos_ref, h_ref, ys_ref, g_ref, b_ref, of_ref, ob_ref, *rest, dilations):
    strided_refs, (y_buf, sem, *stage) = rest[:len(dilations)], rest[len(dilations):]
    tm = h_ref.shape[0]
    i = pl.program_id(0)

    def row_copy(tile, r):
        return pltpu.make_async_copy(ys_ref.at[pl.ds(pos_ref[tile * tm + r], 1), :],
                                     y_buf.at[tile % 2, pl.ds(r, 1), :], sem.at[tile % 2])

    def start_tile(tile):
        def start(r, carry):
            row_copy(tile, r).start()
            return carry
        lax.fori_loop(0, tm, start, 0)

    def wait_tile(tile):
        def wait(r, carry):
            row_copy(tile, r).wait()
            return carry
        lax.fori_loop(0, tm, wait, 0)

    @pl.when(i == 0)
    def _():
        start_tile(i)

    @pl.when(i + 1 < pl.num_programs(0))
    def _():
        start_tile(i + 1)

    wait_tile(i)
    _ln_kernel(h_ref, y_buf.at[i % 2], g_ref, b_ref, of_ref, ob_ref, *strided_refs, dilations=dilations,
               stage_ref=stage[0] if stage else None)


def residual_layer_norm_gathered(h, ys, pos, g, b, name, dilations=()):
    s, d = h.shape
    tm = MOE_TILE
    row = pl.BlockSpec((tm, d), lambda i, pos: (i, 0))
    vec = pl.BlockSpec((1, d), lambda i, pos: (0, 0))
    grid_spec = pltpu.PrefetchScalarGridSpec(
        num_scalar_prefetch=1,
        grid=(s // tm,),
        in_specs=[row, pl.BlockSpec(memory_space=pl.ANY), vec, vec],
        out_specs=[row, row] + [pl.BlockSpec((dil, tm // dil, d), lambda i, pos: (0, i, 0))
                                for dil in dilations],
        scratch_shapes=[pltpu.VMEM((2, tm, d), F32), pltpu.SemaphoreType.DMA((2,))]
        + ([pltpu.VMEM((d // LANES, tm, LANES), F32)] if dilations else []),
    )
    return pl.pallas_call(
        functools.partial(_ln_gather_kernel, dilations=tuple(dilations)),
        grid_spec=grid_spec,
        out_shape=[jax.ShapeDtypeStruct((s, d), F32), jax.ShapeDtypeStruct((s, d), BF16)]
        + [jax.ShapeDtypeStruct((dil, s // dil, d), BF16) for dil in dilations],
        compiler_params=_params("arbitrary"),
        name=name,
    )(pos, h, ys, g.reshape(1, d), b.reshape(1, d))


def hierarchical_moe(h_f32, w_gr, b_gr, w_er, b_er, w_gate, w_up, w_down, layer, tag):
    s, d = h_f32.shape
    ng, ne = MOE_GROUPS, MOE_EXPERTS_PER_GROUP
    tm = MOE_TILE
    w_router = jnp.concatenate(
        [jnp.transpose(w_er, (1, 0, 2)).reshape(d, ng * ne), w_gr,
         jnp.zeros((d, LANES - ng * ne - ng), F32)], axis=1)
    b_router = jnp.concatenate(
        [b_er.reshape(ng * ne), b_gr, jnp.zeros((LANES - ng * ne - ng,), F32)]).reshape(1, LANES)
    w_hi = w_router.astype(BF16)
    w_lo = (w_router - w_hi.astype(F32)).astype(BF16)
    route, counts = moe_route(h_f32, jnp.concatenate([w_hi, w_hi, w_lo], axis=0), b_router,
                              f"moe_router_{tag}")
    counts = counts[0, :ng].astype(jnp.int32)
    padded = (counts + tm - 1) // tm * tm
    ends = jnp.cumsum(padded)
    rank = route[:, ROUTE_RANK_LANE].astype(jnp.int32)
    group = route[:, ROUTE_GROUP_LANE].astype(jnp.int32)
    pos = (ends - padded)[group] + rank
    rows = s + ng * tm
    tile_start = jnp.arange(rows // tm, dtype=jnp.int32) * tm
    tile_group = jnp.minimum(jnp.sum(tile_start[:, None] >= ends[None, :], axis=1), ng - 1).astype(jnp.int32)
    xs, rs = moe_dispatch(h_f32, route, pos, rows, f"moe_dispatch_{tag}")
    hidden = moe_group_up(xs, rs, tile_group, w_gate, w_up, layer, f"moe_up_{tag}")
    ys = moe_group_down(hidden, tile_group, w_down, layer, f"moe_down_{tag}")
    return ys, pos


def stick_breaking_mixer(h_bf16, w_qkv, w_o, j, tag):
    qkv = matmul(h_bf16, w_qkv, BF16, f"sb_qkv_{tag}", lead=(j,))
    o = sb_attention(qkv, SB_HEADS, f"sb_attn_{tag}")
    return matmul(o, w_o, F32, f"sb_out_{tag}", lead=(j,))


def _rope_tables(pos):
    half = MLA_ROPE_DIM // 2
    inv_freq = ROPE_THETA ** (-jnp.arange(half, dtype=F32) / half)
    ang = pos.astype(F32)[:, None] * inv_freq
    cos, sin = jnp.cos(ang), jnp.sin(ang)
    zeros = jnp.zeros((pos.shape[0], LANES - 2 * half), F32)
    z_half = jnp.zeros_like(sin)
    cos_t = jnp.concatenate([cos, cos, zeros], axis=1)
    sin_up = jnp.concatenate([z_half, sin, zeros], axis=1)
    sin_dn = jnp.concatenate([-sin, z_half, zeros], axis=1)
    return cos_t, sin_up, sin_dn


def mla_mixer(h_bf16, pos, w_q_a, q_a_norm, w_q_b, w_kv_a, kv_a_norm, w_kv_b, w_o, j, tag):
    d = h_bf16.shape[1]
    heads = MLA_HEADS
    nope, rope = MLA_NOPE_DIM, MLA_ROPE_DIM
    used = MLA_Q_RANK + MLA_KV_RANK + rope
    width = -(-(MLA_Q_RANK + MLA_KV_RANK + LANES) // 512) * 512
    w_a = jnp.concatenate([w_q_a, w_kv_a, jnp.zeros((d, width - used), F32)], axis=1)
    a = matmul(h_bf16, w_a, F32, f"mla_a_{tag}")
    cos_t, sin_up, sin_dn = _rope_tables(pos)
    cq, ckv, k_rope = mla_prep(a, q_a_norm, kv_a_norm, cos_t, sin_up, sin_dn, f"mla_prep_{tag}")
    w_qb = jnp.pad(w_q_b.reshape(MLA_Q_RANK, heads, nope + rope),
                   ((0, 0), (0, 0), (0, 2 * LANES - nope - rope))).reshape(MLA_Q_RANK, heads * 2 * LANES)
    q = matmul(cq, w_qb, F32, f"mla_qb_{tag}")
    kv = matmul(ckv, w_kv_b, BF16, f"mla_kvb_{tag}", lead=(j,))
    o = mla_attention_core(q, kv, k_rope, cos_t, sin_up, sin_dn, heads, f"mla_attn_{tag}")
    return matmul(o, w_o, F32, f"mla_out_{tag}", lead=(j,))


def dilated_mixer(h_by_dilation, pos, rel_bias, w_qkv, w_o, j, tag):
    groups = DIL_GROUPS
    hw = DIL_HEADS * DIL_HEAD_DIM
    tabs = rel_bias.reshape(REL_BUCKETS, groups, DIL_HEADS)
    parts = []
    for g, d in enumerate(DIL_DILATIONS):
        def col_block(jt, tn, g=g):
            per = hw // tn
            return ((jt // per) * groups + g) * per + jt % per
        qkv = matmul(h_by_dilation[d], w_qkv, BF16, f"dil_qkv_{tag}_g{g}", lead=(j,),
                     n_out=3 * hw, col_block=col_block)
        tab = jnp.pad(tabs[:, g, :].T, ((0, 0), (0, LANES - REL_BUCKETS)))
        parts.append(dilated_group(qkv, pos, tab, g, f"dil_attn_{tag}_g{g}"))
    o = dilated_mix(parts, f"dil_mix_{tag}")
    return matmul(o, w_o, F32, f"dil_out_{tag}", lead=(j,))


def kernel(x, positions, rel_bias, sb_w_qkv, sb_w_o, mla_w_q_a, mla_q_a_norm, mla_w_q_b, mla_w_kv_a,
           mla_kv_a_norm, mla_w_kv_b, mla_w_o, dil_w_qkv, dil_w_o, ln_gain, ln_bias,
           moe_w_group_router, moe_b_group_router, moe_w_expert_router, moe_b_expert_router,
           moe_w_gate, moe_w_up, moe_w_down):
    batch, seq, d = x.shape
    outs = []
    for b in range(batch):
        h = x[b]
        h_bf16 = h.astype(BF16)
        pos = positions[b]
        strided = ()
        assert N_MIXERS > 2 and DEPTH > 0
        for i in range(DEPTH):
            kind, j = i % N_MIXERS, i // N_MIXERS
            tag = f"l{i}"
            if kind == 0:
                mix = stick_breaking_mixer(h_bf16, sb_w_qkv, sb_w_o, j, tag)
            elif kind == 1:
                mix = mla_mixer(h_bf16, pos, mla_w_q_a[j], mla_q_a_norm[j], mla_w_q_b[j], mla_w_kv_a[j],
                                mla_kv_a_norm[j], mla_w_kv_b, mla_w_o, j, tag)
            else:
                h_by_dilation = {1: h_bf16}
                h_by_dilation.update({dil: hs.reshape(seq, d) for dil, hs in zip(next_dilations, strided)})
                mix = dilated_mixer(h_by_dilation, pos, rel_bias, dil_w_qkv, dil_w_o, j, tag)
            h, h_bf16 = residual_layer_norm(h, mix, ln_gain[i, 0], ln_bias[i, 0], f"ln_mix_{tag}")
            ys, slot = hierarchical_moe(h, moe_w_group_router[i], moe_b_group_router[i],
                                        moe_w_expert_router[i], moe_b_expert_router[i],
                                        moe_w_gate, moe_w_up, moe_w_down, i, tag)
            feeds_dilated = i + 1 < DEPTH and (i + 1) % N_MIXERS == 2
            next_dilations = tuple(dil for dil in DIL_DILATIONS if dil > 1) if feeds_dilated else ()
            h, h_bf16, *strided = residual_layer_norm_gathered(h, ys, slot, ln_gain[i, 1], ln_bias[i, 1],
                                                               f"ln_ffn_{tag}", dilations=next_dilations)
        outs.append(h)
    return jnp.stack(outs)
```

```python
import functools
import math

import jax
import jax.numpy as jnp
from jax import lax
from jax.experimental import pallas as pl
from jax.experimental.pallas import tpu as pltpu

DEPTH = 4
N_MIXERS = 3
LN_EPS = 1e-5
RMS_EPS = 1e-6

SB_HEADS = 32
SB_HEAD_DIM = 128

MLA_HEADS = 32
MLA_Q_RANK = 1024
MLA_KV_RANK = 512
MLA_NOPE_DIM = 128
MLA_ROPE_DIM = 64
MLA_V_DIM = 128
ROPE_THETA = 10000.0

DIL_DILATIONS = (1, 4, 16)
DIL_GROUPS = 3
DIL_HEADS = 16
DIL_HEAD_DIM = 128
DIL_WINDOW_KEYS = 128

REL_BUCKETS = 32
REL_MAX_DIST = 2048

MOE_GROUPS = 4
MOE_EXPERTS_PER_GROUP = 8
EXPERT_HIDDEN = 256

DN_ALPHA = (2 * DEPTH) ** 0.25

LANES = 128
VMEM_LIMIT_BYTES = 56 * 1024 * 1024
MATMUL_VMEM_BUDGET = 40 * 1024 * 1024
NEG_BIG = -1e30
EXP2_UNDERFLOW = -151.0
SB_HEADS_PER_STEP = 2
MLA_HEADS_PER_STEP = 2
MLA_QUERY_TILE = 1024
MLA_KEY_TILE = 512
ROW_TILE = 256
DMA_LOOP_UNROLL = 8
MOE_TILE = 512
ROUTE_RANK_LANE = MOE_GROUPS * MOE_EXPERTS_PER_GROUP
ROUTE_GROUP_LANE = ROUTE_RANK_LANE + 1

BF16 = jnp.bfloat16
F32 = jnp.float32

_NT = (((1,), (1,)), ((), ()))


def _params(*sem):
    return pltpu.CompilerParams(dimension_semantics=sem, vmem_limit_bytes=VMEM_LIMIT_BYTES)


def _mm_kernel(x_ref, w_ref, o_ref, wbf_ref):
    @pl.when(pl.program_id(1) == 0)
    def _():
        wbf_ref[...] = w_ref[...].astype(BF16)

    o_ref[...] = jnp.dot(x_ref[...], wbf_ref[...], preferred_element_type=F32).astype(o_ref.dtype)


def _mm_tiles(m, k, n, out_bytes):
    for tm, tn in ((1024, 512), (512, 512), (512, 256), (256, 256), (256, 128), (128, 128)):
        if m % tm or n % tn:
            continue
        need = 2 * tm * k * 2 + 2 * k * tn * 4 + k * tn * 2 + 2 * tm * tn * out_bytes
        if need <= MATMUL_VMEM_BUDGET:
            return tm, tn
    raise ValueError(f"no matmul tiling for {(m, k, n)}")


def matmul(x, w, out_dtype, name, lead=(), n_out=None, col_block=None):
    m, k = x.shape
    n = w.shape[-1] if n_out is None else n_out
    tm, tn = _mm_tiles(m, k, n, jnp.dtype(out_dtype).itemsize)
    wcol = (lambda j: j) if col_block is None else (lambda j: col_block(j, tn))
    return pl.pallas_call(
        _mm_kernel,
        grid=(n // tn, m // tm),
        in_specs=[pl.BlockSpec((tm, k), lambda j, i: (i, 0)),
                  pl.BlockSpec((None,) * len(lead) + (k, tn), lambda j, i: tuple(lead) + (0, wcol(j)))],
        out_specs=pl.BlockSpec((tm, tn), lambda j, i: (i, j)),
        out_shape=jax.ShapeDtypeStruct((m, n), out_dtype),
        scratch_shapes=[pltpu.VMEM((k, tn), BF16)],
        compiler_params=_params("arbitrary", "arbitrary"),
        name=name,
    )(x, w)


def _ln_kernel(h_ref, mix_ref, g_ref, b_ref, of_ref, ob_ref, *strided_refs, dilations=(), stage_ref=None):
    x = DN_ALPHA * h_ref[...] + mix_ref[...]
    mu = jnp.mean(x, axis=-1, keepdims=True)
    xc = x - mu
    var = jnp.mean(xc * xc, axis=-1, keepdims=True)
    y = xc * lax.rsqrt(var + LN_EPS) * g_ref[...] + b_ref[...]
    of_ref[...] = y
    ob_ref[...] = y.astype(BF16)
    if not dilations:
        return
    rows, width = y.shape
    slabs = [slice(c * LANES, (c + 1) * LANES) for c in range(width // LANES)]
    for c, lanes in enumerate(slabs):
        stage_ref[c] = y[:, lanes]
    for ref, d in zip(strided_refs, dilations):
        for r in range(d):
            for c, lanes in enumerate(slabs):
                ref[r, :, lanes] = stage_ref.at[c][pl.ds(r, rows // d, stride=d), :].astype(BF16)


def residual_layer_norm(h, mix, g, b, name):
    s, d = h.shape
    tm = 256
    row = pl.BlockSpec((tm, d), lambda i: (i, 0))
    vec = pl.BlockSpec((1, d), lambda i: (0, 0))
    return pl.pallas_call(
        _ln_kernel,
        grid=(s // tm,),
        in_specs=[row, row, vec, vec],
        out_specs=[row, row],
        out_shape=[jax.ShapeDtypeStruct((s, d), F32), jax.ShapeDtypeStruct((s, d), BF16)],
        compiler_params=_params("arbitrary"),
        name=name,
    )(h, mix, g.reshape(1, d), b.reshape(1, d))


def _sb_kernel(q_ref, k_ref, v_ref, o_ref, acc_ref, carry_ref, *, t, hps, c2):
    qi = pl.program_id(1)
    dh = SB_HEAD_DIM
    heads = [slice(c * dh, (c + 1) * dh) for c in range(hps)]
    q_neg = [-q_ref[:, hs] for hs in heads]
    acc_ref[...] = jnp.zeros_like(acc_ref)
    carry_ref[...] = jnp.zeros_like(carry_ref)
    r = lax.broadcasted_iota(jnp.int32, (t, t), 0)
    c = lax.broadcasted_iota(jnp.int32, (t, t), 1)
    strict = c < r
    later_mask = jnp.where(r > c, 1.0, 0.0).astype(BF16)
    later_mask2 = jnp.concatenate([later_mask, later_mask], axis=0)

    def step(j, masked):
        ks = pl.multiple_of(j * t, t)
        for ci, hs in enumerate(heads):
            k = k_ref[pl.ds(ks, t), hs]
            v = v_ref[pl.ds(ks, t), hs]
            nz = lax.dot_general(q_neg[ci], k, _NT, preferred_element_type=F32) * c2
            log_keep = jnp.minimum(nz, 0.0) - jnp.log2(1.0 + jnp.exp2(-jnp.abs(nz)))
            if masked:
                log_keep = jnp.where(strict, log_keep, 0.0)
            hi = pltpu.bitcast(pltpu.bitcast(log_keep, jnp.uint32) & jnp.uint32(0xFFFF0000), F32)
            hi_lo = jnp.concatenate([hi.astype(BF16), (log_keep - hi).astype(BF16)], axis=1)
            later = jnp.dot(hi_lo, later_mask2, preferred_element_type=F32)
            carry = carry_ref[ci]
            tot = later + jnp.concatenate([carry] * (t // LANES), axis=1)
            a = jnp.exp2(log_keep - nz + tot)
            if masked:
                a = jnp.where(strict, a, 0.0)
            acc_ref[ci] += jnp.dot(a.astype(BF16), v, preferred_element_type=F32)
            carry_ref[ci] = carry + jnp.sum(log_keep, axis=1, keepdims=True)

    def largest_carry():
        m = carry_ref[0]
        for ci in range(1, hps):
            m = jnp.maximum(m, carry_ref[ci])
        return jnp.max(m)

    step(qi, True)

    def cond(state):
        j, top = state
        return jnp.logical_and(j >= 0, top >= EXP2_UNDERFLOW)

    def body(state):
        j, _ = state
        step(j, False)
        return j - 1, largest_carry()

    lax.while_loop(cond, body, (qi - 1, largest_carry()))
    for ci, hs in enumerate(heads):
        o_ref[:, hs] = acc_ref[ci].astype(o_ref.dtype)


def sb_attention(qkv, heads, name):
    s = qkv.shape[0]
    dh = SB_HEAD_DIM
    t = min(256, s)
    hps = SB_HEADS_PER_STEP
    groups = heads // hps
    kern = functools.partial(_sb_kernel, t=t, hps=hps, c2=dh ** -0.5 / math.log(2.0))
    return pl.pallas_call(
        kern,
        grid=(groups, s // t),
        in_specs=[pl.BlockSpec((t, hps * dh), lambda h, i: (i, h)),
                  pl.BlockSpec((s, hps * dh), lambda h, i: (0, groups + h)),
                  pl.BlockSpec((s, hps * dh), lambda h, i: (0, 2 * groups + h))],
        out_specs=pl.BlockSpec((t, hps * dh), lambda h, i: (i, h)),
        out_shape=jax.ShapeDtypeStruct((s, heads * dh), BF16),
        scratch_shapes=[pltpu.VMEM((hps, t, dh), F32), pltpu.VMEM((hps, t, LANES), F32)],
        compiler_params=_params("arbitrary", "arbitrary"),
        name=name,
    )(qkv, qkv, qkv)


def _rope_lanes(x, cos, sin_up, sin_dn):
    half = MLA_ROPE_DIM // 2
    return (x * cos + pltpu.roll(x, half, 1) * sin_up + pltpu.roll(x, LANES - half, 1) * sin_dn)


def _mla_prep_kernel(a_ref, gq_ref, gkv_ref, cos_ref, sup_ref, sdn_ref, cq_ref, ckv_ref, kr_ref):
    a = a_ref[...]
    rq, rkv = MLA_Q_RANK, MLA_KV_RANK
    cq = a[:, :rq]
    cq_ref[...] = (cq * lax.rsqrt(jnp.mean(cq * cq, axis=-1, keepdims=True) + RMS_EPS)
                   * gq_ref[...]).astype(BF16)
    ckv = a[:, rq:rq + rkv]
    ckv_ref[...] = (ckv * lax.rsqrt(jnp.mean(ckv * ckv, axis=-1, keepdims=True) + RMS_EPS)
                    * gkv_ref[...]).astype(BF16)
    kr = a[:, rq + rkv:rq + rkv + LANES]
    kr_ref[...] = _rope_lanes(kr, cos_ref[...], sup_ref[...], sdn_ref[...]).astype(BF16)


def mla_prep(a, gq, gkv, cos, sin_up, sin_dn, name):
    s, n = a.shape
    tm = 512
    row = lambda w: pl.BlockSpec((tm, w), lambda i: (i, 0))
    vec = lambda w: pl.BlockSpec((1, w), lambda i: (0, 0))
    return pl.pallas_call(
        _mla_prep_kernel,
        grid=(s // tm,),
        in_specs=[row(n), vec(MLA_Q_RANK), vec(MLA_KV_RANK), row(LANES), row(LANES), row(LANES)],
        out_specs=[row(MLA_Q_RANK), row(MLA_KV_RANK), row(LANES)],
        out_shape=[jax.ShapeDtypeStruct((s, MLA_Q_RANK), BF16),
                   jax.ShapeDtypeStruct((s, MLA_KV_RANK), BF16),
                   jax.ShapeDtypeStruct((s, LANES), BF16)],
        compiler_params=_params("arbitrary"),
        name=name,
    )(a, gq.reshape(1, -1), gkv.reshape(1, -1), cos, sin_up, sin_dn)


def _mla_kernel(q_ref, kv_ref, kr_ref, cos_ref, sup_ref, sdn_ref, o_ref,
                m_ref, l_ref, acc_ref, *, tq, tk, hps, c2):
    qi = pl.program_id(1)
    wide = 2 * LANES
    cos, sin_up, sin_dn = cos_ref[...], sup_ref[...], sdn_ref[...]
    qc = []
    for ci in range(hps):
        q = q_ref[:, ci * wide:(ci + 1) * wide]
        q_rope = _rope_lanes(q[:, LANES:], cos, sin_up, sin_dn)
        qc.append(jnp.concatenate([q[:, :LANES].astype(BF16), q_rope.astype(BF16)], axis=1))
    m_ref[...] = jnp.full_like(m_ref, NEG_BIG)
    l_ref[...] = jnp.zeros_like(l_ref)
    acc_ref[...] = jnp.zeros_like(acc_ref)
    ahead = (lax.broadcasted_iota(jnp.int32, (tq, tk), 0) - lax.broadcasted_iota(jnp.int32, (tq, tk), 1))

    def step(j, diag):
        ks = pl.multiple_of(j * tk, tk)
        kr = kr_ref[pl.ds(ks, tk), :]
        for ci in range(hps):
            kn = kv_ref[pl.ds(ks, tk), ci * wide:ci * wide + LANES]
            v = kv_ref[pl.ds(ks, tk), ci * wide + LANES:(ci + 1) * wide]
            s = lax.dot_general(qc[ci], jnp.concatenate([kn, kr], axis=1), _NT,
                                preferred_element_type=F32) * c2
            if diag is not None:
                s = jnp.where(ahead >= diag * tk, s, NEG_BIG)
            m_prev = m_ref[ci]
            m_new = jnp.maximum(m_prev, jnp.max(s, axis=1, keepdims=True))
            alpha = jnp.exp2(m_prev - m_new)
            p = jnp.exp2(s - jnp.concatenate([m_new] * (tk // LANES), axis=1))
            l_ref[ci] = alpha * l_ref[ci] + jnp.sum(p, axis=1, keepdims=True)
            acc_ref[ci] = alpha * acc_ref[ci] + jnp.dot(p.astype(BF16), v, preferred_element_type=F32)
            m_ref[ci] = m_new

    def body(j, _):
        step(j, None)
        return 0

    per = tq // tk
    lax.fori_loop(0, qi * per, body, 0)
    for dj in range(per):
        step(qi * per + dj, dj)
    for ci in range(hps):
        o_ref[:, ci * LANES:(ci + 1) * LANES] = (acc_ref[ci] / l_ref[ci]).astype(o_ref.dtype)


def mla_attention_core(q, kv, k_rope, cos, sin_up, sin_dn, heads, name):
    assert MLA_V_DIM == LANES and MLA_NOPE_DIM == LANES
    s = q.shape[0]
    tq = min(MLA_QUERY_TILE, s)
    tk = min(MLA_KEY_TILE, s)
    hps = MLA_HEADS_PER_STEP
    kern = functools.partial(_mla_kernel, tq=tq, tk=tk, hps=hps,
                             c2=(MLA_NOPE_DIM + MLA_ROPE_DIM) ** -0.5 / math.log(2.0))
    tab = pl.BlockSpec((tq, LANES), lambda h, i: (i, 0))
    return pl.pallas_call(
        kern,
        grid=(heads // hps, s // tq),
        in_specs=[pl.BlockSpec((tq, hps * 2 * LANES), lambda h, i: (i, h)),
                  pl.BlockSpec((s, hps * 2 * LANES), lambda h, i: (0, h)),
                  pl.BlockSpec((s, LANES), lambda h, i: (0, 0)),
                  tab, tab, tab],
        out_specs=pl.BlockSpec((tq, hps * LANES), lambda h, i: (i, h)),
        out_shape=jax.ShapeDtypeStruct((s, heads * MLA_V_DIM), BF16),
        scratch_shapes=[pltpu.VMEM((hps, tq, LANES), F32), pltpu.VMEM((hps, tq, LANES), F32),
                        pltpu.VMEM((hps, tq, LANES), F32)],
        compiler_params=_params("arbitrary", "arbitrary"),
        name=name,
    )(q, kv, k_rope, cos, sin_up, sin_dn)


def _t5_bucket(rel):
    n = jnp.maximum(rel, 0)
    max_exact = REL_BUCKETS // 2
    nf = jnp.maximum(n, 1).astype(F32)
    large = max_exact + (jnp.log(nf / max_exact) / math.log(REL_MAX_DIST / max_exact)
                         * (REL_BUCKETS - max_exact)).astype(jnp.int32)
    large = jnp.minimum(large, REL_BUCKETS - 1)
    return jnp.where(n < max_exact, n, large)


def _dil_kernel(tab_ref, q_ref, kp_ref, kc_ref, vp_ref, vc_ref, pq_ref, pkp_ref, pkc_ref,
                o_ref, lse_ref, *, heads, scale):
    a = pl.program_id(1)
    w = DIL_WINDOW_KEYS
    dh = DIL_HEAD_DIM
    rel = pq_ref[...] - jnp.concatenate([pkp_ref[...], pkc_ref[...]], axis=1)
    back = (lax.broadcasted_iota(jnp.int32, (w, 2 * w), 0) + w
            - lax.broadcasted_iota(jnp.int32, (w, 2 * w), 1))
    col = lax.broadcasted_iota(jnp.int32, (w, 2 * w), 1)
    first_col = jnp.where(a > 0, 0, w)
    valid = (back >= 0) & (back <= w) & (col >= first_col)
    bucket = _t5_bucket(rel)
    bucket_chunks = [bucket[:, c * LANES:(c + 1) * LANES] for c in range(2 * w // LANES)]
    for h in range(heads):
        hs = slice(h * dh, (h + 1) * dh)
        table = jnp.broadcast_to(tab_ref[h:h + 1, :], (w, LANES))
        bias = jnp.concatenate([jnp.take_along_axis(table, chunk, axis=1) for chunk in bucket_chunks], axis=1)
        k = jnp.concatenate([kp_ref[:, hs], kc_ref[:, hs]], axis=0)
        v = jnp.concatenate([vp_ref[:, hs], vc_ref[:, hs]], axis=0)
        s = lax.dot_general(q_ref[:, hs], k, _NT, preferred_element_type=F32) * scale + bias
        s = jnp.where(valid, s, NEG_BIG)
        m = jnp.max(s, axis=1, keepdims=True)
        p = jnp.exp(s - m)
        l = jnp.sum(p, axis=1, keepdims=True)
        o_ref[:, hs] = jnp.dot(p.astype(BF16), v, preferred_element_type=F32) / l
        lse_ref[:, hs] = jnp.broadcast_to(m + jnp.log(l), (w, dh))


def dilated_group(qkv, pos, tab, g, name):
    s, n = qkv.shape
    d = DIL_DILATIONS[g]
    heads, dh, w = DIL_HEADS, DIL_HEAD_DIM, DIL_WINDOW_KEYS
    hw = heads * dh
    sub = s // d
    nb = sub // w
    pos_col = pos.reshape(sub, d).T.reshape(d, sub, 1)
    pos_row = pos.reshape(sub, d).T.reshape(d, 1, sub)
    prev = lambda i: jnp.maximum(i - 1, 0)
    kern = functools.partial(_dil_kernel, heads=heads, scale=dh ** -0.5)
    return pl.pallas_call(
        kern,
        grid=(d, nb),
        in_specs=[pl.BlockSpec((heads, LANES), lambda r, i: (0, 0)),
                  pl.BlockSpec((w, hw), lambda r, i: (r * nb + i, 0)),
                  pl.BlockSpec((w, hw), lambda r, i: (r * nb + prev(i), 1)),
                  pl.BlockSpec((w, hw), lambda r, i: (r * nb + i, 1)),
                  pl.BlockSpec((w, hw), lambda r, i: (r * nb + prev(i), 2)),
                  pl.BlockSpec((w, hw), lambda r, i: (r * nb + i, 2)),
                  pl.BlockSpec((None, w, 1), lambda r, i: (r, i, 0)),
                  pl.BlockSpec((None, 1, w), lambda r, i: (r, 0, prev(i))),
                  pl.BlockSpec((None, 1, w), lambda r, i: (r, 0, i))],
        out_specs=[pl.BlockSpec((w, hw), lambda r, i: (r * nb + i, 0)),
                   pl.BlockSpec((w, hw), lambda r, i: (r * nb + i, 0))],
        out_shape=[jax.ShapeDtypeStruct((s, hw), F32), jax.ShapeDtypeStruct((s, hw), F32)],
        compiler_params=_params("arbitrary", "arbitrary"),
        name=name,
    )(tab, qkv, qkv, qkv, qkv, qkv, pos_col, pos_row, pos_row)


def _dil_mix_kernel(*refs, dilations):
    n = len(dilations)
    out_ref = refs[2 * n]
    buffers = list(refs[2 * n + 1:])
    outs, lses = [], []
    for g, d in enumerate(dilations):
        o_ref, l_ref = refs[2 * g], refs[2 * g + 1]
        if d == 1:
            outs.append(o_ref[0])
            lses.append(l_ref[0])
            continue
        o_buf, l_buf = buffers.pop(0), buffers.pop(0)
        slabs, rows = o_buf.shape[0], o_buf.shape[1] // d
        for r in range(d):
            for c in range(slabs):
                lanes = slice(c * LANES, (c + 1) * LANES)
                o_buf.at[c][pl.ds(r, rows, stride=d), :] = o_ref[r, :, lanes]
                l_buf.at[c][pl.ds(r, rows, stride=d), :] = l_ref[r, :, lanes]
        outs.append(jnp.concatenate([o_buf[c] for c in range(slabs)], axis=1))
        lses.append(jnp.concatenate([l_buf[c] for c in range(slabs)], axis=1))
    m = functools.reduce(jnp.maximum, lses)
    es = [jnp.exp(l - m) for l in lses]
    num = functools.reduce(lambda a, b: a + b, [e * o for e, o in zip(es, outs)])
    out_ref[...] = (num / functools.reduce(lambda a, b: a + b, es)).astype(out_ref.dtype)


def dilated_mix(parts, name):
    s, hw = parts[0][0].shape
    tm = 256
    flat, specs, scratch = [], [], []
    for (o, lse), d in zip(parts, DIL_DILATIONS):
        for x in (o, lse):
            flat.append(x.reshape(d, s // d, hw))
            specs.append(pl.BlockSpec((d, tm // d, hw), lambda i: (0, i, 0)))
        if d > 1:
            scratch += [pltpu.VMEM((hw // LANES, tm, LANES), F32)] * 2
    return pl.pallas_call(
        functools.partial(_dil_mix_kernel, dilations=DIL_DILATIONS),
        grid=(s // tm,),
        in_specs=specs,
        out_specs=pl.BlockSpec((tm, hw), lambda i: (i, 0)),
        out_shape=jax.ShapeDtypeStruct((s, hw), BF16),
        scratch_shapes=scratch,
        compiler_params=_params("arbitrary"),
        name=name,
    )(*flat)


def _router_kernel(x_ref, w_ref, b_ref, route_ref, count_ref, run_ref):
    ng, ne = MOE_GROUPS, MOE_EXPERTS_PER_GROUP
    x = x_ref[...]
    x_hi = x.astype(BF16)
    x_lo = (x - x_hi.astype(F32)).astype(BF16)
    logits = jnp.dot(jnp.concatenate([x_hi, x_lo, x_hi], axis=1), w_ref[...],
                     preferred_element_type=F32) + b_ref[...]
    lane = lax.broadcasted_iota(jnp.int32, logits.shape, 1).astype(F32)
    big = float(1 << 20)
    is_group = (lane >= ng * ne) & (lane < ng * ne + ng)
    g_logit = jnp.where(is_group, logits, NEG_BIG)
    g_max = jnp.max(g_logit, axis=1, keepdims=True)
    g_top = jnp.min(jnp.where(is_group & (g_logit == g_max), lane, big), axis=1, keepdims=True) - ng * ne
    g_gate = 1.0 / jnp.sum(jnp.where(is_group, jnp.exp(g_logit - g_max), 0.0), axis=1, keepdims=True)
    in_group = (lane >= g_top * ne) & (lane < (g_top + 1) * ne)
    e_logit = jnp.where(in_group, logits, NEG_BIG)
    v1 = jnp.max(e_logit, axis=1, keepdims=True)
    i1 = jnp.min(jnp.where(in_group & (e_logit == v1), lane, big), axis=1, keepdims=True)
    rest = in_group & (lane != i1)
    e_rest = jnp.where(rest, logits, NEG_BIG)
    v2 = jnp.max(e_rest, axis=1, keepdims=True)
    i2 = jnp.min(jnp.where(rest & (e_rest == v2), lane, big), axis=1, keepdims=True)
    e21 = jnp.exp(v2 - v1)
    w1 = g_gate / (1.0 + e21)
    w2 = g_gate * e21 / (1.0 + e21)
    gates = jnp.where(lane == i1, w1, jnp.where(lane == i2, w2, 0.0))

    @pl.when(pl.program_id(0) == 0)
    def _():
        run_ref[...] = jnp.zeros_like(run_ref)

    tm = x_ref.shape[0]
    member = jnp.where(lane == g_top, 1.0, 0.0)
    earlier = (lax.broadcasted_iota(jnp.int32, (tm, tm), 1)
               < lax.broadcasted_iota(jnp.int32, (tm, tm), 0))
    before = jnp.dot(jnp.where(earlier, 1.0, 0.0).astype(BF16), member.astype(BF16),
                     preferred_element_type=F32)
    run = run_ref[...]
    rank = jnp.sum(jnp.where(lane == g_top, before + run, 0.0), axis=1, keepdims=True)
    run_ref[...] = run + jnp.sum(member, axis=0, keepdims=True)
    count_ref[...] = jnp.broadcast_to(run_ref[...], count_ref.shape)
    route_ref[...] = jnp.where(lane == ROUTE_RANK_LANE, rank,
                               jnp.where(lane == ROUTE_GROUP_LANE, g_top, gates))


def moe_route(h, w_router, b_router, name):
    s, d = h.shape
    tm = ROW_TILE
    return pl.pallas_call(
        _router_kernel,
        grid=(s // tm,),
        in_specs=[pl.BlockSpec((tm, d), lambda i: (i, 0)),
                  pl.BlockSpec((3 * d, LANES), lambda i: (0, 0)),
                  pl.BlockSpec((1, LANES), lambda i: (0, 0))],
        out_specs=[pl.BlockSpec((tm, LANES), lambda i: (i, 0)),
                   pl.BlockSpec((8, LANES), lambda i: (0, 0))],
        out_shape=[jax.ShapeDtypeStruct((s, LANES), F32), jax.ShapeDtypeStruct((8, LANES), F32)],
        scratch_shapes=[pltpu.VMEM((1, LANES), F32)],
        compiler_params=_params("arbitrary"),
        name=name,
    )(h, w_router, b_router)


def _pack_bf16_pair(lo, hi):
    lo_bits = pltpu.bitcast(lo.astype(BF16).astype(F32), jnp.uint32) >> 16
    hi_bits = pltpu.bitcast(hi.astype(BF16).astype(F32), jnp.uint32) & jnp.uint32(0xFFFF0000)
    return lo_bits | hi_bits


def _unpack_bf16_pair(packed):
    lo = pltpu.bitcast(packed << 16, F32).astype(BF16)
    hi = pltpu.bitcast(packed & jnp.uint32(0xFFFF0000), F32).astype(BF16)
    return lo, hi


def _dispatch_kernel(pos_ref, h_ref, route_ref, xs_in, xs_ref, row_ref, sem):
    del xs_in
    tm, d = h_ref.shape
    base = pl.program_id(0) * tm
    row_ref[:, :d // 2] = _pack_bf16_pair(h_ref[:, :d // 2], h_ref[:, d // 2:])
    row_ref[:, d // 2:] = pltpu.bitcast(route_ref[...], jnp.uint32)

    def row_copy(r):
        return pltpu.make_async_copy(row_ref.at[pl.ds(r, 1), :],
                                     xs_ref.at[pl.ds(pos_ref[base + r], 1), :], sem.at[0])

    def start(r, carry):
        row_copy(r).start()
        return carry

    def wait(r, carry):
        row_copy(r).wait()
        return carry

    lax.fori_loop(0, tm, start, 0, unroll=DMA_LOOP_UNROLL)
    lax.fori_loop(0, tm, wait, 0, unroll=DMA_LOOP_UNROLL)


def moe_dispatch(h, route, pos, rows, name):
    s, d = h.shape
    tm = ROW_TILE
    width = d // 2 + LANES
    grid_spec = pltpu.PrefetchScalarGridSpec(
        num_scalar_prefetch=1,
        grid=(s // tm,),
        in_specs=[pl.BlockSpec((tm, d), lambda i, pos: (i, 0)),
                  pl.BlockSpec((tm, LANES), lambda i, pos: (i, 0)),
                  pl.BlockSpec(memory_space=pl.ANY)],
        out_specs=pl.BlockSpec(memory_space=pl.ANY),
        scratch_shapes=[pltpu.VMEM((tm, width), jnp.uint32), pltpu.SemaphoreType.DMA((1,))],
    )
    return pl.pallas_call(
        _dispatch_kernel,
        grid_spec=grid_spec,
        out_shape=jax.ShapeDtypeStruct((rows, width), jnp.uint32),
        input_output_aliases={3: 0},
        compiler_params=_params("arbitrary"),
        name=name,
    )(pos, h, route, jnp.zeros((rows, width), jnp.uint32))


def _weights_changed(tg_ref, t):
    return jnp.logical_or(t == 0, tg_ref[t] != tg_ref[jnp.maximum(t - 1, 0)])


def _group_up_kernel(tg_ref, x_ref, route_ref, wg_ref, wu_ref, o_ref, wg_bf, wu_bf):
    e = pl.program_id(0)
    t = pl.program_id(1)

    @pl.when(_weights_changed(tg_ref, t))
    def _():
        wg_bf[...] = wg_ref[...].astype(BF16)
        wu_bf[...] = wu_ref[...].astype(BF16)

    x = jnp.concatenate(_unpack_bf16_pair(x_ref[...]), axis=1)
    gate = jnp.dot(x, wg_bf[...], preferred_element_type=F32)
    up = jnp.dot(x, wu_bf[...], preferred_element_type=F32)
    lane = lax.broadcasted_iota(jnp.int32, route_ref.shape, 1)
    flat_expert = tg_ref[t] * MOE_EXPERTS_PER_GROUP + e
    route = pltpu.bitcast(route_ref[...], F32)
    comb = jnp.sum(jnp.where(lane == flat_expert, route, 0.0), axis=1, keepdims=True)
    hidden = gate * jax.nn.sigmoid(gate) * up
    o_ref[...] = (hidden * comb).astype(o_ref.dtype)


def moe_group_up(xs, tile_group, w_gate, w_up, layer, name):
    rows = xs.shape[0]
    ne = MOE_EXPERTS_PER_GROUP
    d, f = w_gate.shape[-2:]
    tm = MOE_TILE
    wspec = pl.BlockSpec((None, None, d, f), lambda e, t, tg: (layer, tg[t] * ne + e, 0, 0))
    grid_spec = pltpu.PrefetchScalarGridSpec(
        num_scalar_prefetch=1,
        grid=(ne, rows // tm),
        in_specs=[pl.BlockSpec((tm, d // 2), lambda e, t, tg: (t, 0)),
                  pl.BlockSpec((tm, LANES), lambda e, t, tg: (t, d // 2 // LANES)),
                  wspec, wspec],
        out_specs=pl.BlockSpec((tm, f), lambda e, t, tg: (t, e)),
        scratch_shapes=[pltpu.VMEM((d, f), BF16), pltpu.VMEM((d, f), BF16)],
    )
    return pl.pallas_call(
        _group_up_kernel,
        grid_spec=grid_spec,
        out_shape=jax.ShapeDtypeStruct((rows, ne * f), BF16),
        compiler_params=_params("arbitrary", "arbitrary"),
        name=name,
    )(tile_group, xs, xs, w_gate, w_up)


def _group_down_kernel(tg_ref, x_ref, w_ref, o_ref, w_bf):
    @pl.when(_weights_changed(tg_ref, pl.program_id(1)))
    def _():
        w_bf[...] = w_ref[...].astype(BF16)

    o_ref[...] = jnp.dot(x_ref[...], w_bf[...], preferred_element_type=F32)


def moe_group_down(hidden, tile_group, w_down, layer, name):
    rows, k = hidden.shape
    d = w_down.shape[-1]
    tm = MOE_TILE
    tn = min(1024, d)
    w4 = w_down.reshape(w_down.shape[0], MOE_GROUPS, k, d)
    grid_spec = pltpu.PrefetchScalarGridSpec(
        num_scalar_prefetch=1,
        grid=(d // tn, rows // tm),
        in_specs=[pl.BlockSpec((tm, k), lambda j, t, tg: (t, 0)),
                  pl.BlockSpec((None, None, k, tn), lambda j, t, tg: (layer, tg[t], 0, j))],
        out_specs=pl.BlockSpec((tm, tn), lambda j, t, tg: (t, j)),
        scratch_shapes=[pltpu.VMEM((k, tn), BF16)],
    )
    return pl.pallas_call(
        _group_down_kernel,
        grid_spec=grid_spec,
        out_shape=jax.ShapeDtypeStruct((rows, d), F32),
        compiler_params=_params("arbitrary", "arbitrary"),
        name=name,
    )(tile_group, hidden, w4)


def _ln_gather_kernel(pos_ref, h_ref, ys_ref, g_ref, b_ref, of_ref, ob_ref, *rest, dilations):
    strided_refs, (y_buf, sem, *stage) = rest[:len(dilations)], rest[len(dilations):]
    tm = h_ref.shape[0]
    i = pl.program_id(0)

    def row_copy(tile, r):
        return pltpu.make_async_copy(ys_ref.at[pl.ds(pos_ref[tile * tm + r], 1), :],
                                     y_buf.at[tile % 2, pl.ds(r, 1), :], sem.at[tile % 2])

    def start_tile(tile):
        def start(r, carry):
            row_copy(tile, r).start()
            return carry
        lax.fori_loop(0, tm, start, 0, unroll=DMA_LOOP_UNROLL)

    def wait_tile(tile):
        def wait(r, carry):
            row_copy(tile, r).wait()
            return carry
        lax.fori_loop(0, tm, wait, 0, unroll=DMA_LOOP_UNROLL)

    @pl.when(i == 0)
    def _():
        start_tile(i)

    @pl.when(i + 1 < pl.num_programs(0))
    def _():
        start_tile(i + 1)

    wait_tile(i)
    _ln_kernel(h_ref, y_buf.at[i % 2], g_ref, b_ref, of_ref, ob_ref, *strided_refs, dilations=dilations,
               stage_ref=stage[0] if stage else None)


def residual_layer_norm_gathered(h, ys, pos, g, b, name, dilations=()):
    s, d = h.shape
    tm = ROW_TILE
    row = pl.BlockSpec((tm, d), lambda i, pos: (i, 0))
    vec = pl.BlockSpec((1, d), lambda i, pos: (0, 0))
    grid_spec = pltpu.PrefetchScalarGridSpec(
        num_scalar_prefetch=1,
        grid=(s // tm,),
        in_specs=[row, pl.BlockSpec(memory_space=pl.ANY), vec, vec],
        out_specs=[row, row] + [pl.BlockSpec((dil, tm // dil, d), lambda i, pos: (0, i, 0))
                                for dil in dilations],
        scratch_shapes=[pltpu.VMEM((2, tm, d), F32), pltpu.SemaphoreType.DMA((2,))]
        + ([pltpu.VMEM((d // LANES, tm, LANES), F32)] if dilations else []),
    )
    return pl.pallas_call(
        functools.partial(_ln_gather_kernel, dilations=tuple(dilations)),
        grid_spec=grid_spec,
        out_shape=[jax.ShapeDtypeStruct((s, d), F32), jax.ShapeDtypeStruct((s, d), BF16)]
        + [jax.ShapeDtypeStruct((dil, s // dil, d), BF16) for dil in dilations],
        compiler_params=_params("arbitrary"),
        name=name,
    )(pos, h, ys, g.reshape(1, d), b.reshape(1, d))


def hierarchical_moe(h_f32, w_gr, b_gr, w_er, b_er, w_gate, w_up, w_down, layer, tag):
    s, d = h_f32.shape
    ng, ne = MOE_GROUPS, MOE_EXPERTS_PER_GROUP
    tm = MOE_TILE
    w_router = jnp.concatenate(
        [jnp.transpose(w_er, (1, 0, 2)).reshape(d, ng * ne), w_gr,
         jnp.zeros((d, LANES - ng * ne - ng), F32)], axis=1)
    b_router = jnp.concatenate(
        [b_er.reshape(ng * ne), b_gr, jnp.zeros((LANES - ng * ne - ng,), F32)]).reshape(1, LANES)
    w_hi = w_router.astype(BF16)
    w_lo = (w_router - w_hi.astype(F32)).astype(BF16)
    route, counts = moe_route(h_f32, jnp.concatenate([w_hi, w_hi, w_lo], axis=0), b_router,
                              f"moe_router_{tag}")
    counts = counts[0, :ng].astype(jnp.int32)
    padded = (counts + tm - 1) // tm * tm
    ends = jnp.cumsum(padded)
    rank = route[:, ROUTE_RANK_LANE].astype(jnp.int32)
    group = route[:, ROUTE_GROUP_LANE].astype(jnp.int32)
    pos = (ends - padded)[group] + rank
    rows = s + ng * tm
    tile_start = jnp.arange(rows // tm, dtype=jnp.int32) * tm
    tile_group = jnp.minimum(jnp.sum(tile_start[:, None] >= ends[None, :], axis=1), ng - 1).astype(jnp.int32)
    xs = moe_dispatch(h_f32, route, pos, rows, f"moe_dispatch_{tag}")
    hidden = moe_group_up(xs, tile_group, w_gate, w_up, layer, f"moe_up_{tag}")
    ys = moe_group_down(hidden, tile_group, w_down, layer, f"moe_down_{tag}")
    return ys, pos


def stick_breaking_mixer(h_bf16, w_qkv, w_o, j, tag):
    qkv = matmul(h_bf16, w_qkv, BF16, f"sb_qkv_{tag}", lead=(j,))
    o = sb_attention(qkv, SB_HEADS, f"sb_attn_{tag}")
    return matmul(o, w_o, F32, f"sb_out_{tag}", lead=(j,))


def _rope_tables(pos):
    half = MLA_ROPE_DIM // 2
    inv_freq = ROPE_THETA ** (-jnp.arange(half, dtype=F32) / half)
    ang = pos.astype(F32)[:, None] * inv_freq
    cos, sin = jnp.cos(ang), jnp.sin(ang)
    zeros = jnp.zeros((pos.shape[0], LANES - 2 * half), F32)
    z_half = jnp.zeros_like(sin)
    cos_t = jnp.concatenate([cos, cos, zeros], axis=1)
    sin_up = jnp.concatenate([z_half, sin, zeros], axis=1)
    sin_dn = jnp.concatenate([-sin, z_half, zeros], axis=1)
    return cos_t, sin_up, sin_dn


def mla_mixer(h_bf16, pos, w_q_a, q_a_norm, w_q_b, w_kv_a, kv_a_norm, w_kv_b, w_o, j, tag):
    d = h_bf16.shape[1]
    heads = MLA_HEADS
    nope, rope = MLA_NOPE_DIM, MLA_ROPE_DIM
    used = MLA_Q_RANK + MLA_KV_RANK + rope
    width = -(-(MLA_Q_RANK + MLA_KV_RANK + LANES) // 512) * 512
    w_a = jnp.concatenate([w_q_a, w_kv_a, jnp.zeros((d, width - used), F32)], axis=1)
    a = matmul(h_bf16, w_a, F32, f"mla_a_{tag}")
    cos_t, sin_up, sin_dn = _rope_tables(pos)
    cq, ckv, k_rope = mla_prep(a, q_a_norm, kv_a_norm, cos_t, sin_up, sin_dn, f"mla_prep_{tag}")
    w_qb = jnp.pad(w_q_b.reshape(MLA_Q_RANK, heads, nope + rope),
                   ((0, 0), (0, 0), (0, 2 * LANES - nope - rope))).reshape(MLA_Q_RANK, heads * 2 * LANES)
    q = matmul(cq, w_qb, F32, f"mla_qb_{tag}")
    kv = matmul(ckv, w_kv_b, BF16, f"mla_kvb_{tag}", lead=(j,))
    o = mla_attention_core(q, kv, k_rope, cos_t, sin_up, sin_dn, heads, f"mla_attn_{tag}")
    return matmul(o, w_o, F32, f"mla_out_{tag}", lead=(j,))


def dilated_mixer(h_by_dilation, pos, rel_bias, w_qkv, w_o, j, tag):
    groups = DIL_GROUPS
    hw = DIL_HEADS * DIL_HEAD_DIM
    tabs = rel_bias.reshape(REL_BUCKETS, groups, DIL_HEADS)
    parts = []
    for g, d in enumerate(DIL_DILATIONS):
        def col_block(jt, tn, g=g):
            per = hw // tn
            return ((jt // per) * groups + g) * per + jt % per
        qkv = matmul(h_by_dilation[d], w_qkv, BF16, f"dil_qkv_{tag}_g{g}", lead=(j,),
                     n_out=3 * hw, col_block=col_block)
        tab = jnp.pad(tabs[:, g, :].T, ((0, 0), (0, LANES - REL_BUCKETS)))
        parts.append(dilated_group(qkv, pos, tab, g, f"dil_attn_{tag}_g{g}"))
    o = dilated_mix(parts, f"dil_mix_{tag}")
    return matmul(o, w_o, F32, f"dil_out_{tag}", lead=(j,))


def kernel(x, positions, rel_bias, sb_w_qkv, sb_w_o, mla_w_q_a, mla_q_a_norm, mla_w_q_b, mla_w_kv_a,
           mla_kv_a_norm, mla_w_kv_b, mla_w_o, dil_w_qkv, dil_w_o, ln_gain, ln_bias,
           moe_w_group_router, moe_b_group_router, moe_w_expert_router, moe_b_expert_router,
           moe_w_gate, moe_w_up, moe_w_down):
    batch, seq, d = x.shape
    outs = []
    for b in range(batch):
        h = x[b]
        h_bf16 = h.astype(BF16)
        pos = positions[b]
        strided = ()
        assert N_MIXERS > 2 and DEPTH > 0
        for i in range(DEPTH):
            kind, j = i % N_MIXERS, i // N_MIXERS
            tag = f"l{i}"
            if kind == 0:
                mix = stick_breaking_mixer(h_bf16, sb_w_qkv, sb_w_o, j, tag)
            elif kind == 1:
                mix = mla_mixer(h_bf16, pos, mla_w_q_a[j], mla_q_a_norm[j], mla_w_q_b[j], mla_w_kv_a[j],
                                mla_kv_a_norm[j], mla_w_kv_b, mla_w_o, j, tag)
            else:
                h_by_dilation = {1: h_bf16}
                h_by_dilation.update({dil: hs.reshape(seq, d) for dil, hs in zip(next_dilations, strided)})
                mix = dilated_mixer(h_by_dilation, pos, rel_bias, dil_w_qkv, dil_w_o, j, tag)
            h, h_bf16 = residual_layer_norm(h, mix, ln_gain[i, 0], ln_bias[i, 0], f"ln_mix_{tag}")
            ys, slot = hierarchical_moe(h, moe_w_group_router[i], moe_b_group_router[i],
                                        moe_w_expert_router[i], moe_b_expert_router[i],
                                        moe_w_gate, moe_w_up, moe_w_down, i, tag)
            feeds_dilated = i + 1 < DEPTH and (i + 1) % N_MIXERS == 2
            next_dilations = tuple(dil for dil in DIL_DILATIONS if dil > 1) if feeds_dilated else ()
            h, h_bf16, *strided = residual_layer_norm_gathered(h, ys, slot, ln_gain[i, 1], ln_bias[i, 1],
                                                               f"ln_ffn_{tag}", dilations=next_dilations)
        outs.append(h)
    return jnp.stack(outs)
```

```python
import functools
import math

import jax
import jax.numpy as jnp
from jax import lax
from jax.experimental import pallas as pl
from jax.experimental.pallas import tpu as pltpu

DEPTH = 4
N_MIXERS = 3
LN_EPS = 1e-5
RMS_EPS = 1e-6

SB_HEADS = 32
SB_HEAD_DIM = 128

MLA_HEADS = 32
MLA_Q_RANK = 1024
MLA_KV_RANK = 512
MLA_NOPE_DIM = 128
MLA_ROPE_DIM = 64
MLA_V_DIM = 128
ROPE_THETA = 10000.0

DIL_DILATIONS = (1, 4, 16)
DIL_GROUPS = 3
DIL_HEADS = 16
DIL_HEAD_DIM = 128
DIL_WINDOW_KEYS = 128

REL_BUCKETS = 32
REL_MAX_DIST = 2048

MOE_GROUPS = 4
MOE_EXPERTS_PER_GROUP = 8
EXPERT_HIDDEN = 256

DN_ALPHA = (2 * DEPTH) ** 0.25

LANES = 128
VMEM_LIMIT_BYTES = 56 * 1024 * 1024
MATMUL_VMEM_BUDGET = 40 * 1024 * 1024
NEG_BIG = -1e30
EXP2_UNDERFLOW = -151.0
SB_HEADS_PER_STEP = 2
MLA_HEADS_PER_STEP = 2
MLA_QUERY_TILE = 1024
MLA_KEY_TILE = 512
ROW_TILE = 256
DMA_LOOP_UNROLL = 8
MOE_TILE = 512
ROUTE_RANK_LANE = MOE_GROUPS * MOE_EXPERTS_PER_GROUP
ROUTE_GROUP_LANE = ROUTE_RANK_LANE + 1

BF16 = jnp.bfloat16
F32 = jnp.float32

_NT = (((1,), (1,)), ((), ()))


def _params(*sem):
    return pltpu.CompilerParams(dimension_semantics=sem, vmem_limit_bytes=VMEM_LIMIT_BYTES)


def _mm_kernel(x_ref, w_ref, o_ref, wbf_ref):
    @pl.when(pl.program_id(1) == 0)
    def _():
        wbf_ref[...] = w_ref[...].astype(BF16)

    o_ref[...] = jnp.dot(x_ref[...], wbf_ref[...], preferred_element_type=F32).astype(o_ref.dtype)


def _mm_tiles(m, k, n, out_bytes):
    for tm, tn in ((1024, 512), (512, 512), (512, 256), (256, 256), (256, 128), (128, 128)):
        if m % tm or n % tn:
            continue
        need = 2 * tm * k * 2 + 2 * k * tn * 4 + k * tn * 2 + 2 * tm * tn * out_bytes
        if need <= MATMUL_VMEM_BUDGET:
            return tm, tn
    raise ValueError(f"no matmul tiling for {(m, k, n)}")


def matmul(x, w, out_dtype, name, lead=(), n_out=None, col_block=None):
    m, k = x.shape
    n = w.shape[-1] if n_out is None else n_out
    tm, tn = _mm_tiles(m, k, n, jnp.dtype(out_dtype).itemsize)
    wcol = (lambda j: j) if col_block is None else (lambda j: col_block(j, tn))
    return pl.pallas_call(
        _mm_kernel,
        grid=(n // tn, m // tm),
        in_specs=[pl.BlockSpec((tm, k), lambda j, i: (i, 0)),
                  pl.BlockSpec((None,) * len(lead) + (k, tn), lambda j, i: tuple(lead) + (0, wcol(j)))],
        out_specs=pl.BlockSpec((tm, tn), lambda j, i: (i, j)),
        out_shape=jax.ShapeDtypeStruct((m, n), out_dtype),
        scratch_shapes=[pltpu.VMEM((k, tn), BF16)],
        compiler_params=_params("arbitrary", "arbitrary"),
        name=name,
    )(x, w)


def _ln_kernel(h_ref, mix_ref, g_ref, b_ref, of_ref, ob_ref, *strided_refs, dilations=(), stage_ref=None):
    x = DN_ALPHA * h_ref[...] + mix_ref[...]
    mu = jnp.mean(x, axis=-1, keepdims=True)
    xc = x - mu
    var = jnp.mean(xc * xc, axis=-1, keepdims=True)
    y = xc * lax.rsqrt(var + LN_EPS) * g_ref[...] + b_ref[...]
    of_ref[...] = y
    ob_ref[...] = y.astype(BF16)
    if not dilations:
        return
    rows, width = y.shape
    slabs = [slice(c * LANES, (c + 1) * LANES) for c in range(width // LANES)]
    for c, lanes in enumerate(slabs):
        stage_ref[c] = y[:, lanes]
    for ref, d in zip(strided_refs, dilations):
        for r in range(d):
            for c, lanes in enumerate(slabs):
                ref[r, :, lanes] = stage_ref.at[c][pl.ds(r, rows // d, stride=d), :].astype(BF16)


def residual_layer_norm(h, mix, g, b, name):
    s, d = h.shape
    tm = 256
    row = pl.BlockSpec((tm, d), lambda i: (i, 0))
    vec = pl.BlockSpec((1, d), lambda i: (0, 0))
    return pl.pallas_call(
        _ln_kernel,
        grid=(s // tm,),
        in_specs=[row, row, vec, vec],
        out_specs=[row, row],
        out_shape=[jax.ShapeDtypeStruct((s, d), F32), jax.ShapeDtypeStruct((s, d), BF16)],
        compiler_params=_params("arbitrary"),
        name=name,
    )(h, mix, g.reshape(1, d), b.reshape(1, d))


def _sb_kernel(q_ref, k_ref, v_ref, o_ref, *scratch, t, hps, c2):
    qi = pl.program_id(1)
    dh = SB_HEAD_DIM
    heads = [slice(c * dh, (c + 1) * dh) for c in range(hps)]
    q_neg = [-q_ref[:, hs] for hs in heads]
    acc_refs, carry_refs = scratch[:hps], scratch[hps:]
    for ref in scratch:
        ref[...] = jnp.zeros_like(ref)
    r = lax.broadcasted_iota(jnp.int32, (t, t), 0)
    c = lax.broadcasted_iota(jnp.int32, (t, t), 1)
    strict = c < r
    later_mask = jnp.where(r > c, 1.0, 0.0).astype(BF16)
    later_mask2 = jnp.concatenate([later_mask, later_mask], axis=0)

    def step(j, masked):
        ks = pl.multiple_of(j * t, t)
        for ci, hs in enumerate(heads):
            k = k_ref[pl.ds(ks, t), hs]
            v = v_ref[pl.ds(ks, t), hs]
            nz = lax.dot_general(q_neg[ci], k, _NT, preferred_element_type=F32) * c2
            log_keep = jnp.minimum(nz, 0.0) - jnp.log2(1.0 + jnp.exp2(-jnp.abs(nz)))
            if masked:
                log_keep = jnp.where(strict, log_keep, 0.0)
            hi = pltpu.bitcast(pltpu.bitcast(log_keep, jnp.uint32) & jnp.uint32(0xFFFF0000), F32)
            hi_lo = jnp.concatenate([hi.astype(BF16), (log_keep - hi).astype(BF16)], axis=1)
            later = jnp.dot(hi_lo, later_mask2, preferred_element_type=F32)
            carry = carry_refs[ci][...]
            tot = later + jnp.concatenate([carry] * (t // LANES), axis=1)
            a = jnp.exp2(log_keep - nz + tot)
            if masked:
                a = jnp.where(strict, a, 0.0)
            acc_refs[ci][...] += jnp.dot(a.astype(BF16), v, preferred_element_type=F32)
            carry_refs[ci][...] = carry + jnp.sum(log_keep, axis=1, keepdims=True)

    def largest_carry():
        return jnp.max(functools.reduce(jnp.maximum, [ref[...] for ref in carry_refs]))

    def first_two_tiles():
        ks = pl.multiple_of((qi - 1) * t, t)
        col = lax.broadcasted_iota(jnp.int32, (t, 2 * t), 1)
        row = lax.broadcasted_iota(jnp.int32, (t, 2 * t), 0)
        visible = col - t < row
        for ci, hs in enumerate(heads):
            k = k_ref[pl.ds(ks, 2 * t), hs]
            v = v_ref[pl.ds(ks, 2 * t), hs]
            nz = lax.dot_general(q_neg[ci], k, _NT, preferred_element_type=F32) * c2
            log_keep = jnp.minimum(nz, 0.0) - jnp.log2(1.0 + jnp.exp2(-jnp.abs(nz)))
            log_keep = jnp.where(visible, log_keep, 0.0)
            hi = pltpu.bitcast(pltpu.bitcast(log_keep, jnp.uint32) & jnp.uint32(0xFFFF0000), F32)
            lo = log_keep - hi
            hi_lo = jnp.concatenate(
                [jnp.concatenate([hi[:, :t].astype(BF16), lo[:, :t].astype(BF16)], axis=1),
                 jnp.concatenate([hi[:, t:].astype(BF16), lo[:, t:].astype(BF16)], axis=1)], axis=0)
            later = jnp.dot(hi_lo, later_mask2, preferred_element_type=F32)
            sum_old = jnp.sum(log_keep[:, :t], axis=1, keepdims=True)
            sum_diag = jnp.sum(log_keep[:, t:], axis=1, keepdims=True)
            tot = jnp.concatenate([later[:t] + sum_diag, later[t:]], axis=1)
            a = jnp.where(visible, jnp.exp2(log_keep - nz + tot), 0.0)
            acc_refs[ci][...] = jnp.dot(a.astype(BF16), v, preferred_element_type=F32)
            carry_refs[ci][...] = jnp.broadcast_to(sum_old + sum_diag, carry_refs[ci].shape)

    @pl.when(qi == 0)
    def _():
        step(qi, True)

    @pl.when(qi > 0)
    def _():
        first_two_tiles()

    def cond(state):
        j, top = state
        return jnp.logical_and(j >= 0, top >= EXP2_UNDERFLOW)

    def body(state):
        j, _ = state
        step(j, False)
        return j - 1, largest_carry()

    lax.while_loop(cond, body, (qi - 2, largest_carry()))
    for ci, hs in enumerate(heads):
        o_ref[:, hs] = acc_refs[ci][...].astype(o_ref.dtype)


def sb_attention(qkv, heads, name):
    s = qkv.shape[0]
    dh = SB_HEAD_DIM
    t = min(256, s)
    hps = SB_HEADS_PER_STEP
    groups = heads // hps
    kern = functools.partial(_sb_kernel, t=t, hps=hps, c2=dh ** -0.5 / math.log(2.0))
    return pl.pallas_call(
        kern,
        grid=(groups, s // t),
        in_specs=[pl.BlockSpec((t, hps * dh), lambda h, i: (i, h)),
                  pl.BlockSpec((s, hps * dh), lambda h, i: (0, groups + h)),
                  pl.BlockSpec((s, hps * dh), lambda h, i: (0, 2 * groups + h))],
        out_specs=pl.BlockSpec((t, hps * dh), lambda h, i: (i, h)),
        out_shape=jax.ShapeDtypeStruct((s, heads * dh), BF16),
        scratch_shapes=[pltpu.VMEM((t, dh), F32)] * hps + [pltpu.VMEM((t, LANES), F32)] * hps,
        compiler_params=_params("arbitrary", "arbitrary"),
        name=name,
    )(qkv, qkv, qkv)


def _rope_lanes(x, cos, sin_up, sin_dn):
    half = MLA_ROPE_DIM // 2
    return (x * cos + pltpu.roll(x, half, 1) * sin_up + pltpu.roll(x, LANES - half, 1) * sin_dn)


def _mla_prep_kernel(a_ref, gq_ref, gkv_ref, cos_ref, sup_ref, sdn_ref, cq_ref, ckv_ref, kr_ref):
    a = a_ref[...]
    rq, rkv = MLA_Q_RANK, MLA_KV_RANK
    cq = a[:, :rq]
    cq_ref[...] = (cq * lax.rsqrt(jnp.mean(cq * cq, axis=-1, keepdims=True) + RMS_EPS)
                   * gq_ref[...]).astype(BF16)
    ckv = a[:, rq:rq + rkv]
    ckv_ref[...] = (ckv * lax.rsqrt(jnp.mean(ckv * ckv, axis=-1, keepdims=True) + RMS_EPS)
                    * gkv_ref[...]).astype(BF16)
    kr = a[:, rq + rkv:rq + rkv + LANES]
    kr_ref[...] = _rope_lanes(kr, cos_ref[...], sup_ref[...], sdn_ref[...]).astype(BF16)


def mla_prep(a, gq, gkv, cos, sin_up, sin_dn, name):
    s, n = a.shape
    tm = 512
    row = lambda w: pl.BlockSpec((tm, w), lambda i: (i, 0))
    vec = lambda w: pl.BlockSpec((1, w), lambda i: (0, 0))
    return pl.pallas_call(
        _mla_prep_kernel,
        grid=(s // tm,),
        in_specs=[row(n), vec(MLA_Q_RANK), vec(MLA_KV_RANK), row(LANES), row(LANES), row(LANES)],
        out_specs=[row(MLA_Q_RANK), row(MLA_KV_RANK), row(LANES)],
        out_shape=[jax.ShapeDtypeStruct((s, MLA_Q_RANK), BF16),
                   jax.ShapeDtypeStruct((s, MLA_KV_RANK), BF16),
                   jax.ShapeDtypeStruct((s, LANES), BF16)],
        compiler_params=_params("arbitrary"),
        name=name,
    )(a, gq.reshape(1, -1), gkv.reshape(1, -1), cos, sin_up, sin_dn)


def _mla_kernel(q_ref, kv_ref, kr_ref, cos_ref, sup_ref, sdn_ref, o_ref,
                m_ref, l_ref, acc_ref, *, tq, tk, hps, c2):
    qi = pl.program_id(1)
    wide = 2 * LANES
    cos, sin_up, sin_dn = cos_ref[...], sup_ref[...], sdn_ref[...]
    qc = []
    for ci in range(hps):
        q = q_ref[:, ci * wide:(ci + 1) * wide]
        q_rope = _rope_lanes(q[:, LANES:], cos, sin_up, sin_dn)
        qc.append(jnp.concatenate([q[:, :LANES].astype(BF16), q_rope.astype(BF16)], axis=1))
    m_ref[...] = jnp.full_like(m_ref, NEG_BIG)
    l_ref[...] = jnp.zeros_like(l_ref)
    acc_ref[...] = jnp.zeros_like(acc_ref)
    ahead = (lax.broadcasted_iota(jnp.int32, (tq, tk), 0) - lax.broadcasted_iota(jnp.int32, (tq, tk), 1))

    def step(j, diag):
        ks = pl.multiple_of(j * tk, tk)
        kr = kr_ref[pl.ds(ks, tk), :]
        for ci in range(hps):
            kn = kv_ref[pl.ds(ks, tk), ci * wide:ci * wide + LANES]
            v = kv_ref[pl.ds(ks, tk), ci * wide + LANES:(ci + 1) * wide]
            s = lax.dot_general(qc[ci], jnp.concatenate([kn, kr], axis=1), _NT,
                                preferred_element_type=F32) * c2
            if diag is not None:
                s = jnp.where(ahead >= diag * tk, s, NEG_BIG)
            m_prev = m_ref[ci]
            m_new = jnp.maximum(m_prev, jnp.max(s, axis=1, keepdims=True))
            alpha = jnp.exp2(m_prev - m_new)
            p = jnp.exp2(s - jnp.concatenate([m_new] * (tk // LANES), axis=1))
            l_ref[ci] = alpha * l_ref[ci] + jnp.sum(p, axis=1, keepdims=True)
            acc_ref[ci] = alpha * acc_ref[ci] + jnp.dot(p.astype(BF16), v, preferred_element_type=F32)
            m_ref[ci] = m_new

    def body(j, _):
        step(j, None)
        return 0

    per = tq // tk
    lax.fori_loop(0, qi * per, body, 0)
    for dj in range(per):
        step(qi * per + dj, dj)
    for ci in range(hps):
        o_ref[:, ci * LANES:(ci + 1) * LANES] = (acc_ref[ci] / l_ref[ci]).astype(o_ref.dtype)


def mla_attention_core(q, kv, k_rope, cos, sin_up, sin_dn, heads, name):
    assert MLA_V_DIM == LANES and MLA_NOPE_DIM == LANES
    s = q.shape[0]
    tq = min(MLA_QUERY_TILE, s)
    tk = min(MLA_KEY_TILE, s)
    hps = MLA_HEADS_PER_STEP
    kern = functools.partial(_mla_kernel, tq=tq, tk=tk, hps=hps,
                             c2=(MLA_NOPE_DIM + MLA_ROPE_DIM) ** -0.5 / math.log(2.0))
    tab = pl.BlockSpec((tq, LANES), lambda h, i: (i, 0))
    return pl.pallas_call(
        kern,
        grid=(heads // hps, s // tq),
        in_specs=[pl.BlockSpec((tq, hps * 2 * LANES), lambda h, i: (i, h)),
                  pl.BlockSpec((s, hps * 2 * LANES), lambda h, i: (0, h)),
                  pl.BlockSpec((s, LANES), lambda h, i: (0, 0)),
                  tab, tab, tab],
        out_specs=pl.BlockSpec((tq, hps * LANES), lambda h, i: (i, h)),
        out_shape=jax.ShapeDtypeStruct((s, heads * MLA_V_DIM), BF16),
        scratch_shapes=[pltpu.VMEM((hps, tq, LANES), F32), pltpu.VMEM((hps, tq, LANES), F32),
                        pltpu.VMEM((hps, tq, LANES), F32)],
        compiler_params=_params("arbitrary", "arbitrary"),
        name=name,
    )(q, kv, k_rope, cos, sin_up, sin_dn)


def _t5_bucket(rel):
    n = jnp.maximum(rel, 0)
    max_exact = REL_BUCKETS // 2
    nf = jnp.maximum(n, 1).astype(F32)
    large = max_exact + (jnp.log(nf / max_exact) / math.log(REL_MAX_DIST / max_exact)
                         * (REL_BUCKETS - max_exact)).astype(jnp.int32)
    large = jnp.minimum(large, REL_BUCKETS - 1)
    return jnp.where(n < max_exact, n, large)


def _dil_kernel(tab_ref, q_ref, kp_ref, kc_ref, vp_ref, vc_ref, pq_ref, pkp_ref, pkc_ref,
                o_ref, lse_ref, *, heads, scale):
    a = pl.program_id(1)
    w = DIL_WINDOW_KEYS
    dh = DIL_HEAD_DIM
    rel = pq_ref[...] - jnp.concatenate([pkp_ref[...], pkc_ref[...]], axis=1)
    back = (lax.broadcasted_iota(jnp.int32, (w, 2 * w), 0) + w
            - lax.broadcasted_iota(jnp.int32, (w, 2 * w), 1))
    col = lax.broadcasted_iota(jnp.int32, (w, 2 * w), 1)
    first_col = jnp.where(a > 0, 0, w)
    valid = (back >= 0) & (back <= w) & (col >= first_col)
    bucket = _t5_bucket(rel)
    bucket_chunks = [bucket[:, c * LANES:(c + 1) * LANES] for c in range(2 * w // LANES)]
    for h in range(heads):
        hs = slice(h * dh, (h + 1) * dh)
        table = jnp.broadcast_to(tab_ref[h:h + 1, :], (w, LANES))
        bias = jnp.concatenate([jnp.take_along_axis(table, chunk, axis=1) for chunk in bucket_chunks], axis=1)
        k = jnp.concatenate([kp_ref[:, hs], kc_ref[:, hs]], axis=0)
        v = jnp.concatenate([vp_ref[:, hs], vc_ref[:, hs]], axis=0)
        s = lax.dot_general(q_ref[:, hs], k, _NT, preferred_element_type=F32) * scale + bias
        s = jnp.where(valid, s, NEG_BIG)
        m = jnp.max(s, axis=1, keepdims=True)
        p = jnp.exp(s - m)
        l = jnp.sum(p, axis=1, keepdims=True)
        o_ref[:, hs] = jnp.dot(p.astype(BF16), v, preferred_element_type=F32) / l
        lse_ref[:, hs] = jnp.broadcast_to(m + jnp.log(l), (w, dh))


def dilated_group(qkv, pos, tab, g, name):
    s, n = qkv.shape
    d = DIL_DILATIONS[g]
    heads, dh, w = DIL_HEADS, DIL_HEAD_DIM, DIL_WINDOW_KEYS
    hw = heads * dh
    sub = s // d
    nb = sub // w
    pos_col = pos.reshape(sub, d).T.reshape(d, sub, 1)
    pos_row = pos.reshape(sub, d).T.reshape(d, 1, sub)
    prev = lambda i: jnp.maximum(i - 1, 0)
    kern = functools.partial(_dil_kernel, heads=heads, scale=dh ** -0.5)
    return pl.pallas_call(
        kern,
        grid=(d, nb),
        in_specs=[pl.BlockSpec((heads, LANES), lambda r, i: (0, 0)),
                  pl.BlockSpec((w, hw), lambda r, i: (r * nb + i, 0)),
                  pl.BlockSpec((w, hw), lambda r, i: (r * nb + prev(i), 1)),
                  pl.BlockSpec((w, hw), lambda r, i: (r * nb + i, 1)),
                  pl.BlockSpec((w, hw), lambda r, i: (r * nb + prev(i), 2)),
                  pl.BlockSpec((w, hw), lambda r, i: (r * nb + i, 2)),
                  pl.BlockSpec((None, w, 1), lambda r, i: (r, i, 0)),
                  pl.BlockSpec((None, 1, w), lambda r, i: (r, 0, prev(i))),
                  pl.BlockSpec((None, 1, w), lambda r, i: (r, 0, i))],
        out_specs=[pl.BlockSpec((w, hw), lambda r, i: (r * nb + i, 0)),
                   pl.BlockSpec((w, hw), lambda r, i: (r * nb + i, 0))],
        out_shape=[jax.ShapeDtypeStruct((s, hw), F32), jax.ShapeDtypeStruct((s, hw), F32)],
        compiler_params=_params("arbitrary", "arbitrary"),
        name=name,
    )(tab, qkv, qkv, qkv, qkv, qkv, pos_col, pos_row, pos_row)


def _dil_mix_kernel(*refs, dilations):
    n = len(dilations)
    out_ref = refs[2 * n]
    buffers = list(refs[2 * n + 1:])
    outs, lses = [], []
    for g, d in enumerate(dilations):
        o_ref, l_ref = refs[2 * g], refs[2 * g + 1]
        if d == 1:
            outs.append(o_ref[0])
            lses.append(l_ref[0])
            continue
        o_buf, l_buf = buffers.pop(0), buffers.pop(0)
        slabs, rows = o_buf.shape[0], o_buf.shape[1] // d
        for r in range(d):
            for c in range(slabs):
                lanes = slice(c * LANES, (c + 1) * LANES)
                o_buf.at[c][pl.ds(r, rows, stride=d), :] = o_ref[r, :, lanes]
                l_buf.at[c][pl.ds(r, rows, stride=d), :] = l_ref[r, :, lanes]
        outs.append(jnp.concatenate([o_buf[c] for c in range(slabs)], axis=1))
        lses.append(jnp.concatenate([l_buf[c] for c in range(slabs)], axis=1))
    m = functools.reduce(jnp.maximum, lses)
    es = [jnp.exp(l - m) for l in lses]
    num = functools.reduce(lambda a, b: a + b, [e * o for e, o in zip(es, outs)])
    out_ref[...] = (num / functools.reduce(lambda a, b: a + b, es)).astype(out_ref.dtype)


def dilated_mix(parts, name):
    s, hw = parts[0][0].shape
    tm = 256
    flat, specs, scratch = [], [], []
    for (o, lse), d in zip(parts, DIL_DILATIONS):
        for x in (o, lse):
            flat.append(x.reshape(d, s // d, hw))
            specs.append(pl.BlockSpec((d, tm // d, hw), lambda i: (0, i, 0)))
        if d > 1:
            scratch += [pltpu.VMEM((hw // LANES, tm, LANES), F32)] * 2
    return pl.pallas_call(
        functools.partial(_dil_mix_kernel, dilations=DIL_DILATIONS),
        grid=(s // tm,),
        in_specs=specs,
        out_specs=pl.BlockSpec((tm, hw), lambda i: (i, 0)),
        out_shape=jax.ShapeDtypeStruct((s, hw), BF16),
        scratch_shapes=scratch,
        compiler_params=_params("arbitrary"),
        name=name,
    )(*flat)


def _router_kernel(x_ref, w_ref, b_ref, route_ref, count_ref, run_ref):
    ng, ne = MOE_GROUPS, MOE_EXPERTS_PER_GROUP
    x = x_ref[...]
    x_hi = x.astype(BF16)
    x_lo = (x - x_hi.astype(F32)).astype(BF16)
    logits = jnp.dot(jnp.concatenate([x_hi, x_lo, x_hi], axis=1), w_ref[...],
                     preferred_element_type=F32) + b_ref[...]
    lane = lax.broadcasted_iota(jnp.int32, logits.shape, 1).astype(F32)
    big = float(1 << 20)
    is_group = (lane >= ng * ne) & (lane < ng * ne + ng)
    g_logit = jnp.where(is_group, logits, NEG_BIG)
    g_max = jnp.max(g_logit, axis=1, keepdims=True)
    g_top = jnp.min(jnp.where(is_group & (g_logit == g_max), lane, big), axis=1, keepdims=True) - ng * ne
    g_gate = 1.0 / jnp.sum(jnp.where(is_group, jnp.exp(g_logit - g_max), 0.0), axis=1, keepdims=True)
    in_group = (lane >= g_top * ne) & (lane < (g_top + 1) * ne)
    e_logit = jnp.where(in_group, logits, NEG_BIG)
    v1 = jnp.max(e_logit, axis=1, keepdims=True)
    i1 = jnp.min(jnp.where(in_group & (e_logit == v1), lane, big), axis=1, keepdims=True)
    rest = in_group & (lane != i1)
    e_rest = jnp.where(rest, logits, NEG_BIG)
    v2 = jnp.max(e_rest, axis=1, keepdims=True)
    i2 = jnp.min(jnp.where(rest & (e_rest == v2), lane, big), axis=1, keepdims=True)
    e21 = jnp.exp(v2 - v1)
    w1 = g_gate / (1.0 + e21)
    w2 = g_gate * e21 / (1.0 + e21)
    gates = jnp.where(lane == i1, w1, jnp.where(lane == i2, w2, 0.0))

    @pl.when(pl.program_id(0) == 0)
    def _():
        run_ref[...] = jnp.zeros_like(run_ref)

    tm = x_ref.shape[0]
    member = jnp.where(lane == g_top, 1.0, 0.0)
    earlier = (lax.broadcasted_iota(jnp.int32, (tm, tm), 1)
               < lax.broadcasted_iota(jnp.int32, (tm, tm), 0))
    before = jnp.dot(jnp.where(earlier, 1.0, 0.0).astype(BF16), member.astype(BF16),
                     preferred_element_type=F32)
    run = run_ref[...]
    rank = jnp.sum(jnp.where(lane == g_top, before + run, 0.0), axis=1, keepdims=True)
    run_ref[...] = run + jnp.sum(member, axis=0, keepdims=True)
    count_ref[...] = jnp.broadcast_to(run_ref[...], count_ref.shape)
    route_ref[...] = jnp.where(lane == ROUTE_RANK_LANE, rank,
                               jnp.where(lane == ROUTE_GROUP_LANE, g_top, gates))


def moe_route(h, w_router, b_router, name):
    s, d = h.shape
    tm = ROW_TILE
    return pl.pallas_call(
        _router_kernel,
        grid=(s // tm,),
        in_specs=[pl.BlockSpec((tm, d), lambda i: (i, 0)),
                  pl.BlockSpec((3 * d, LANES), lambda i: (0, 0)),
                  pl.BlockSpec((1, LANES), lambda i: (0, 0))],
        out_specs=[pl.BlockSpec((tm, LANES), lambda i: (i, 0)),
                   pl.BlockSpec((8, LANES), lambda i: (0, 0))],
        out_shape=[jax.ShapeDtypeStruct((s, LANES), F32), jax.ShapeDtypeStruct((8, LANES), F32)],
        scratch_shapes=[pltpu.VMEM((1, LANES), F32)],
        compiler_params=_params("arbitrary"),
        name=name,
    )(h, w_router, b_router)


def _pack_bf16_pair(lo, hi):
    lo_bits = pltpu.bitcast(lo.astype(BF16).astype(F32), jnp.uint32) >> 16
    hi_bits = pltpu.bitcast(hi.astype(BF16).astype(F32), jnp.uint32) & jnp.uint32(0xFFFF0000)
    return lo_bits | hi_bits


def _unpack_bf16_pair(packed):
    lo = pltpu.bitcast(packed << 16, F32).astype(BF16)
    hi = pltpu.bitcast(packed & jnp.uint32(0xFFFF0000), F32).astype(BF16)
    return lo, hi


def _dispatch_kernel(pos_ref, h_ref, route_ref, xs_in, xs_ref, row_ref, sem):
    del xs_in
    tm, d = h_ref.shape
    base = pl.program_id(0) * tm
    row_ref[:, :d // 2] = _pack_bf16_pair(h_ref[:, :d // 2], h_ref[:, d // 2:])
    row_ref[:, d // 2:] = pltpu.bitcast(route_ref[...], jnp.uint32)

    def row_copy(r):
        return pltpu.make_async_copy(row_ref.at[pl.ds(r, 1), :],
                                     xs_ref.at[pl.ds(pos_ref[base + r], 1), :], sem.at[0])

    def start(r, carry):
        row_copy(r).start()
        return carry

    def wait(r, carry):
        row_copy(r).wait()
        return carry

    lax.fori_loop(0, tm, start, 0, unroll=DMA_LOOP_UNROLL)
    lax.fori_loop(0, tm, wait, 0, unroll=DMA_LOOP_UNROLL)


def moe_dispatch(h, route, pos, rows, name):
    s, d = h.shape
    tm = ROW_TILE
    width = d // 2 + LANES
    grid_spec = pltpu.PrefetchScalarGridSpec(
        num_scalar_prefetch=1,
        grid=(s // tm,),
        in_specs=[pl.BlockSpec((tm, d), lambda i, pos: (i, 0)),
                  pl.BlockSpec((tm, LANES), lambda i, pos: (i, 0)),
                  pl.BlockSpec(memory_space=pl.ANY)],
        out_specs=pl.BlockSpec(memory_space=pl.ANY),
        scratch_shapes=[pltpu.VMEM((tm, width), jnp.uint32), pltpu.SemaphoreType.DMA((1,))],
    )
    return pl.pallas_call(
        _dispatch_kernel,
        grid_spec=grid_spec,
        out_shape=jax.ShapeDtypeStruct((rows, width), jnp.uint32),
        input_output_aliases={3: 0},
        compiler_params=_params("arbitrary"),
        name=name,
    )(pos, h, route, jnp.zeros((rows, width), jnp.uint32))


def _weights_changed(tg_ref, t):
    return jnp.logical_or(t == 0, tg_ref[t] != tg_ref[jnp.maximum(t - 1, 0)])


def _group_up_kernel(tg_ref, x_ref, route_ref, wg_ref, wu_ref, o_ref, wg_bf, wu_bf):
    e = pl.program_id(0)
    t = pl.program_id(1)

    @pl.when(_weights_changed(tg_ref, t))
    def _():
        wg_bf[...] = wg_ref[...].astype(BF16)
        wu_bf[...] = wu_ref[...].astype(BF16)

    x = jnp.concatenate(_unpack_bf16_pair(x_ref[...]), axis=1)
    gate = jnp.dot(x, wg_bf[...], preferred_element_type=F32)
    up = jnp.dot(x, wu_bf[...], preferred_element_type=F32)
    lane = lax.broadcasted_iota(jnp.int32, route_ref.shape, 1)
    flat_expert = tg_ref[t] * MOE_EXPERTS_PER_GROUP + e
    route = pltpu.bitcast(route_ref[...], F32)
    comb = jnp.sum(jnp.where(lane == flat_expert, route, 0.0), axis=1, keepdims=True)
    hidden = gate * jax.nn.sigmoid(gate) * up
    o_ref[...] = (hidden * comb).astype(o_ref.dtype)


def moe_group_up(xs, tile_group, w_gate, w_up, layer, name):
    rows = xs.shape[0]
    ne = MOE_EXPERTS_PER_GROUP
    d, f = w_gate.shape[-2:]
    tm = MOE_TILE
    wspec = pl.BlockSpec((None, None, d, f), lambda e, t, tg: (layer, tg[t] * ne + e, 0, 0))
    grid_spec = pltpu.PrefetchScalarGridSpec(
        num_scalar_prefetch=1,
        grid=(ne, rows // tm),
        in_specs=[pl.BlockSpec((tm, d // 2), lambda e, t, tg: (t, 0)),
                  pl.BlockSpec((tm, LANES), lambda e, t, tg: (t, d // 2 // LANES)),
                  wspec, wspec],
        out_specs=pl.BlockSpec((tm, f), lambda e, t, tg: (t, e)),
        scratch_shapes=[pltpu.VMEM((d, f), BF16), pltpu.VMEM((d, f), BF16)],
    )
    return pl.pallas_call(
        _group_up_kernel,
        grid_spec=grid_spec,
        out_shape=jax.ShapeDtypeStruct((rows, ne * f), BF16),
        compiler_params=_params("arbitrary", "arbitrary"),
        name=name,
    )(tile_group, xs, xs, w_gate, w_up)


def _group_down_kernel(tg_ref, x_ref, w_ref, o_ref, w_bf):
    @pl.when(_weights_changed(tg_ref, pl.program_id(1)))
    def _():
        w_bf[...] = w_ref[...].astype(BF16)

    o_ref[...] = jnp.dot(x_ref[...], w_bf[...], preferred_element_type=F32)


def moe_group_down(hidden, tile_group, w_down, layer, name):
    rows, k = hidden.shape
    d = w_down.shape[-1]
    tm = MOE_TILE
    tn = min(1024, d)
    w4 = w_down.reshape(w_down.shape[0], MOE_GROUPS, k, d)
    grid_spec = pltpu.PrefetchScalarGridSpec(
        num_scalar_prefetch=1,
        grid=(d // tn, rows // tm),
        in_specs=[pl.BlockSpec((tm, k), lambda j, t, tg: (t, 0)),
                  pl.BlockSpec((None, None, k, tn), lambda j, t, tg: (layer, tg[t], 0, j))],
        out_specs=pl.BlockSpec((tm, tn), lambda j, t, tg: (t, j)),
        scratch_shapes=[pltpu.VMEM((k, tn), BF16)],
    )
    return pl.pallas_call(
        _group_down_kernel,
        grid_spec=grid_spec,
        out_shape=jax.ShapeDtypeStruct((rows, d), F32),
        compiler_params=_params("arbitrary", "arbitrary"),
        name=name,
    )(tile_group, hidden, w4)


def _ln_gather_kernel(pos_ref, h_ref, ys_ref, g_ref, b_ref, of_ref, ob_ref, *rest, dilations):
    strided_refs, (y_buf, sem, *stage) = rest[:len(dilations)], rest[len(dilations):]
    tm = h_ref.shape[0]
    i = pl.program_id(0)

    def row_copy(tile, r):
        return pltpu.make_async_copy(ys_ref.at[pl.ds(pos_ref[tile * tm + r], 1), :],
                                     y_buf.at[tile % 2, pl.ds(r, 1), :], sem.at[tile % 2])

    def start_tile(tile):
        def start(r, carry):
            row_copy(tile, r).start()
            return carry
        lax.fori_loop(0, tm, start, 0, unroll=DMA_LOOP_UNROLL)

    def wait_tile(tile):
        def wait(r, carry):
            row_copy(tile, r).wait()
            return carry
        lax.fori_loop(0, tm, wait, 0, unroll=DMA_LOOP_UNROLL)

    @pl.when(i == 0)
    def _():
        start_tile(i)

    @pl.when(i + 1 < pl.num_programs(0))
    def _():
        start_tile(i + 1)

    wait_tile(i)
    _ln_kernel(h_ref, y_buf.at[i % 2], g_ref, b_ref, of_ref, ob_ref, *strided_refs, dilations=dilations,
               stage_ref=stage[0] if stage else None)


def residual_layer_norm_gathered(h, ys, pos, g, b, name, dilations=()):
    s, d = h.shape
    tm = ROW_TILE
    row = pl.BlockSpec((tm, d), lambda i, pos: (i, 0))
    vec = pl.BlockSpec((1, d), lambda i, pos: (0, 0))
    grid_spec = pltpu.PrefetchScalarGridSpec(
        num_scalar_prefetch=1,
        grid=(s // tm,),
        in_specs=[row, pl.BlockSpec(memory_space=pl.ANY), vec, vec],
        out_specs=[row, row] + [pl.BlockSpec((dil, tm // dil, d), lambda i, pos: (0, i, 0))
                                for dil in dilations],
        scratch_shapes=[pltpu.VMEM((2, tm, d), F32), pltpu.SemaphoreType.DMA((2,))]
        + ([pltpu.VMEM((d // LANES, tm, LANES), F32)] if dilations else []),
    )
    return pl.pallas_call(
        functools.partial(_ln_gather_kernel, dilations=tuple(dilations)),
        grid_spec=grid_spec,
        out_shape=[jax.ShapeDtypeStruct((s, d), F32), jax.ShapeDtypeStruct((s, d), BF16)]
        + [jax.ShapeDtypeStruct((dil, s // dil, d), BF16) for dil in dilations],
        compiler_params=_params("arbitrary"),
        name=name,
    )(pos, h, ys, g.reshape(1, d), b.reshape(1, d))


def hierarchical_moe(h_f32, w_gr, b_gr, w_er, b_er, w_gate, w_up, w_down, layer, tag):
    s, d = h_f32.shape
    ng, ne = MOE_GROUPS, MOE_EXPERTS_PER_GROUP
    tm = MOE_TILE
    w_router = jnp.concatenate(
        [jnp.transpose(w_er, (1, 0, 2)).reshape(d, ng * ne), w_gr,
         jnp.zeros((d, LANES - ng * ne - ng), F32)], axis=1)
    b_router = jnp.concatenate(
        [b_er.reshape(ng * ne), b_gr, jnp.zeros((LANES - ng * ne - ng,), F32)]).reshape(1, LANES)
    w_hi = w_router.astype(BF16)
    w_lo = (w_router - w_hi.astype(F32)).astype(BF16)
    route, counts = moe_route(h_f32, jnp.concatenate([w_hi, w_hi, w_lo], axis=0), b_router,
                              f"moe_router_{tag}")
    counts = counts[0, :ng].astype(jnp.int32)
    padded = (counts + tm - 1) // tm * tm
    ends = jnp.cumsum(padded)
    rank = route[:, ROUTE_RANK_LANE].astype(jnp.int32)
    group = route[:, ROUTE_GROUP_LANE].astype(jnp.int32)
    pos = (ends - padded)[group] + rank
    rows = s + ng * tm
    tile_start = jnp.arange(rows // tm, dtype=jnp.int32) * tm
    tile_group = jnp.minimum(jnp.sum(tile_start[:, None] >= ends[None, :], axis=1), ng - 1).astype(jnp.int32)
    xs = moe_dispatch(h_f32, route, pos, rows, f"moe_dispatch_{tag}")
    hidden = moe_group_up(xs, tile_group, w_gate, w_up, layer, f"moe_up_{tag}")
    ys = moe_group_down(hidden, tile_group, w_down, layer, f"moe_down_{tag}")
    return ys, pos


def stick_breaking_mixer(h_bf16, w_qkv, w_o, j, tag):
    qkv = matmul(h_bf16, w_qkv, BF16, f"sb_qkv_{tag}", lead=(j,))
    o = sb_attention(qkv, SB_HEADS, f"sb_attn_{tag}")
    return matmul(o, w_o, F32, f"sb_out_{tag}", lead=(j,))


def _rope_tables(pos):
    half = MLA_ROPE_DIM // 2
    inv_freq = ROPE_THETA ** (-jnp.arange(half, dtype=F32) / half)
    ang = pos.astype(F32)[:, None] * inv_freq
    cos, sin = jnp.cos(ang), jnp.sin(ang)
    zeros = jnp.zeros((pos.shape[0], LANES - 2 * half), F32)
    z_half = jnp.zeros_like(sin)
    cos_t = jnp.concatenate([cos, cos, zeros], axis=1)
    sin_up = jnp.concatenate([z_half, sin, zeros], axis=1)
    sin_dn = jnp.concatenate([-sin, z_half, zeros], axis=1)
    return cos_t, sin_up, sin_dn


def mla_mixer(h_bf16, pos, w_q_a, q_a_norm, w_q_b, w_kv_a, kv_a_norm, w_kv_b, w_o, j, tag):
    d = h_bf16.shape[1]
    heads = MLA_HEADS
    nope, rope = MLA_NOPE_DIM, MLA_ROPE_DIM
    used = MLA_Q_RANK + MLA_KV_RANK + rope
    width = -(-(MLA_Q_RANK + MLA_KV_RANK + LANES) // 512) * 512
    w_a = jnp.concatenate([w_q_a, w_kv_a, jnp.zeros((d, width - used), F32)], axis=1)
    a = matmul(h_bf16, w_a, F32, f"mla_a_{tag}")
    cos_t, sin_up, sin_dn = _rope_tables(pos)
    cq, ckv, k_rope = mla_prep(a, q_a_norm, kv_a_norm, cos_t, sin_up, sin_dn, f"mla_prep_{tag}")
    w_qb = jnp.pad(w_q_b.reshape(MLA_Q_RANK, heads, nope + rope),
                   ((0, 0), (0, 0), (0, 2 * LANES - nope - rope))).reshape(MLA_Q_RANK, heads * 2 * LANES)
    q = matmul(cq, w_qb, F32, f"mla_qb_{tag}")
    kv = matmul(ckv, w_kv_b, BF16, f"mla_kvb_{tag}", lead=(j,))
    o = mla_attention_core(q, kv, k_rope, cos_t, sin_up, sin_dn, heads, f"mla_attn_{tag}")
    return matmul(o, w_o, F32, f"mla_out_{tag}", lead=(j,))


def dilated_mixer(h_by_dilation, pos, rel_bias, w_qkv, w_o, j, tag):
    groups = DIL_GROUPS
    hw = DIL_HEADS * DIL_HEAD_DIM
    tabs = rel_bias.reshape(REL_BUCKETS, groups, DIL_HEADS)
    parts = []
    for g, d in enumerate(DIL_DILATIONS):
        def col_block(jt, tn, g=g):
            per = hw // tn
            return ((jt // per) * groups + g) * per + jt % per
        qkv = matmul(h_by_dilation[d], w_qkv, BF16, f"dil_qkv_{tag}_g{g}", lead=(j,),
                     n_out=3 * hw, col_block=col_block)
        tab = jnp.pad(tabs[:, g, :].T, ((0, 0), (0, LANES - REL_BUCKETS)))
        parts.append(dilated_group(qkv, pos, tab, g, f"dil_attn_{tag}_g{g}"))
    o = dilated_mix(parts, f"dil_mix_{tag}")
    return matmul(o, w_o, F32, f"dil_out_{tag}", lead=(j,))


def kernel(x, positions, rel_bias, sb_w_qkv, sb_w_o, mla_w_q_a, mla_q_a_norm, mla_w_q_b, mla_w_kv_a,
           mla_kv_a_norm, mla_w_kv_b, mla_w_o, dil_w_qkv, dil_w_o, ln_gain, ln_bias,
           moe_w_group_router, moe_b_group_router, moe_w_expert_router, moe_b_expert_router,
           moe_w_gate, moe_w_up, moe_w_down):
    batch, seq, d = x.shape
    outs = []
    for b in range(batch):
        h = x[b]
        h_bf16 = h.astype(BF16)
        pos = positions[b]
        strided = ()
        assert N_MIXERS > 2 and DEPTH > 0
        for i in range(DEPTH):
            kind, j = i % N_MIXERS, i // N_MIXERS
            tag = f"l{i}"
            if kind == 0:
                mix = stick_breaking_mixer(h_bf16, sb_w_qkv, sb_w_o, j, tag)
            elif kind == 1:
                mix = mla_mixer(h_bf16, pos, mla_w_q_a[j], mla_q_a_norm[j], mla_w_q_b[j], mla_w_kv_a[j],
                                mla_kv_a_norm[j], mla_w_kv_b, mla_w_o, j, tag)
            else:
                h_by_dilation = {1: h_bf16}
                h_by_dilation.update({dil: hs.reshape(seq, d) for dil, hs in zip(next_dilations, strided)})
                mix = dilated_mixer(h_by_dilation, pos, rel_bias, dil_w_qkv, dil_w_o, j, tag)
            h, h_bf16 = residual_layer_norm(h, mix, ln_gain[i, 0], ln_bias[i, 0], f"ln_mix_{tag}")
            ys, slot = hierarchical_moe(h, moe_w_group_router[i], moe_b_group_router[i],
                                        moe_w_expert_router[i], moe_b_expert_router[i],
                                        moe_w_gate, moe_w_up, moe_w_down, i, tag)
            feeds_dilated = i + 1 < DEPTH and (i + 1) % N_MIXERS == 2
            next_dilations = tuple(dil for dil in DIL_DILATIONS if dil > 1) if feeds_dilated else ()
            h, h_bf16, *strided = residual_layer_norm_gathered(h, ys, slot, ln_gain[i, 1], ln_bias[i, 1],
                                                               f"ln_ffn_{tag}", dilations=next_dilations)
        outs.append(h)
    return jnp.stack(outs)
```

```python
import functools
import math

import jax
import jax.numpy as jnp
from jax import lax
from jax.experimental import pallas as pl
from jax.experimental.pallas import tpu as pltpu

DEPTH = 4
N_MIXERS = 3
LN_EPS = 1e-5
RMS_EPS = 1e-6

SB_HEADS = 32
SB_HEAD_DIM = 128

MLA_HEADS = 32
MLA_Q_RANK = 1024
MLA_KV_RANK = 512
MLA_NOPE_DIM = 128
MLA_ROPE_DIM = 64
MLA_V_DIM = 128
ROPE_THETA = 10000.0

DIL_DILATIONS = (1, 4, 16)
DIL_GROUPS = 3
DIL_HEADS = 16
DIL_HEAD_DIM = 128
DIL_WINDOW_KEYS = 128

REL_BUCKETS = 32
REL_MAX_DIST = 2048

MOE_GROUPS = 4
MOE_EXPERTS_PER_GROUP = 8
EXPERT_HIDDEN = 256

DN_ALPHA = (2 * DEPTH) ** 0.25

LANES = 128
VMEM_LIMIT_BYTES = 56 * 1024 * 1024
MATMUL_VMEM_BUDGET = 40 * 1024 * 1024
NEG_BIG = -1e30
EXP2_UNDERFLOW = -151.0
SB_HEADS_PER_STEP = 2
MLA_HEADS_PER_STEP = 2
MLA_QUERY_TILE = 1024
MLA_KEY_TILE = 512
ROW_TILE = 256
DMA_LOOP_UNROLL = 8
MOE_TILE = 512
ROUTE_RANK_LANE = MOE_GROUPS * MOE_EXPERTS_PER_GROUP
ROUTE_GROUP_LANE = ROUTE_RANK_LANE + 1

BF16 = jnp.bfloat16
F32 = jnp.float32

_NT = (((1,), (1,)), ((), ()))


def _params(*sem):
    return pltpu.CompilerParams(dimension_semantics=sem, vmem_limit_bytes=VMEM_LIMIT_BYTES)


def _mm_kernel(x_ref, w_ref, o_ref, wbf_ref):
    @pl.when(pl.program_id(1) == 0)
    def _():
        wbf_ref[...] = w_ref[...].astype(BF16)

    o_ref[...] = jnp.dot(x_ref[...], wbf_ref[...], preferred_element_type=F32).astype(o_ref.dtype)


def _mm_tiles(m, k, n, out_bytes):
    for tm, tn in ((1024, 512), (512, 512), (512, 256), (256, 256), (256, 128), (128, 128)):
        if m % tm or n % tn:
            continue
        need = 2 * tm * k * 2 + 2 * k * tn * 4 + k * tn * 2 + 2 * tm * tn * out_bytes
        if need <= MATMUL_VMEM_BUDGET:
            return tm, tn
    raise ValueError(f"no matmul tiling for {(m, k, n)}")


def matmul(x, w, out_dtype, name, lead=(), n_out=None, col_block=None):
    m, k = x.shape
    n = w.shape[-1] if n_out is None else n_out
    tm, tn = _mm_tiles(m, k, n, jnp.dtype(out_dtype).itemsize)
    wcol = (lambda j: j) if col_block is None else (lambda j: col_block(j, tn))
    return pl.pallas_call(
        _mm_kernel,
        grid=(n // tn, m // tm),
        in_specs=[pl.BlockSpec((tm, k), lambda j, i: (i, 0)),
                  pl.BlockSpec((None,) * len(lead) + (k, tn), lambda j, i: tuple(lead) + (0, wcol(j)))],
        out_specs=pl.BlockSpec((tm, tn), lambda j, i: (i, j)),
        out_shape=jax.ShapeDtypeStruct((m, n), out_dtype),
        scratch_shapes=[pltpu.VMEM((k, tn), BF16)],
        compiler_params=_params("arbitrary", "arbitrary"),
        name=name,
    )(x, w)


def _ln_kernel(h_ref, mix_ref, g_ref, b_ref, of_ref, ob_ref, *strided_refs, dilations=(), stage_ref=None):
    x = DN_ALPHA * h_ref[...] + mix_ref[...]
    mu = jnp.mean(x, axis=-1, keepdims=True)
    xc = x - mu
    var = jnp.mean(xc * xc, axis=-1, keepdims=True)
    y = xc * lax.rsqrt(var + LN_EPS) * g_ref[...] + b_ref[...]
    of_ref[...] = y
    if ob_ref is not None:
        ob_ref[...] = y.astype(BF16)
    if not dilations:
        return
    rows, width = y.shape
    slabs = [slice(c * LANES, (c + 1) * LANES) for c in range(width // LANES)]
    for c, lanes in enumerate(slabs):
        stage_ref[c] = y[:, lanes]
    for ref, d in zip(strided_refs, dilations):
        for r in range(d):
            for c, lanes in enumerate(slabs):
                ref[r, :, lanes] = stage_ref.at[c][pl.ds(r, rows // d, stride=d), :].astype(BF16)


def _sb_kernel(q_ref, k_ref, v_ref, o_ref, *scratch, t, hps, c2):
    qi = pl.program_id(1)
    dh = SB_HEAD_DIM
    heads = [slice(c * dh, (c + 1) * dh) for c in range(hps)]
    q_neg = [-q_ref[:, hs] for hs in heads]
    acc_refs, carry_refs = scratch[:hps], scratch[hps:]
    for ref in scratch:
        ref[...] = jnp.zeros_like(ref)
    r = lax.broadcasted_iota(jnp.int32, (t, t), 0)
    c = lax.broadcasted_iota(jnp.int32, (t, t), 1)
    strict = c < r
    later_mask = jnp.where(r > c, 1.0, 0.0).astype(BF16)
    later_mask2 = jnp.concatenate([later_mask, later_mask], axis=0)

    def step(j, masked):
        ks = pl.multiple_of(j * t, t)
        for ci, hs in enumerate(heads):
            k = k_ref[pl.ds(ks, t), hs]
            v = v_ref[pl.ds(ks, t), hs]
            nz = lax.dot_general(q_neg[ci], k, _NT, preferred_element_type=F32) * c2
            log_keep = jnp.minimum(nz, 0.0) - jnp.log2(1.0 + jnp.exp2(-jnp.abs(nz)))
            if masked:
                log_keep = jnp.where(strict, log_keep, 0.0)
            hi = pltpu.bitcast(pltpu.bitcast(log_keep, jnp.uint32) & jnp.uint32(0xFFFF0000), F32)
            hi_lo = jnp.concatenate([hi.astype(BF16), (log_keep - hi).astype(BF16)], axis=1)
            later = jnp.dot(hi_lo, later_mask2, preferred_element_type=F32)
            carry = carry_refs[ci][...]
            tot = later + jnp.concatenate([carry] * (t // LANES), axis=1)
            a = jnp.exp2(log_keep - nz + tot)
            if masked:
                a = jnp.where(strict, a, 0.0)
            acc_refs[ci][...] += jnp.dot(a.astype(BF16), v, preferred_element_type=F32)
            carry_refs[ci][...] = carry + jnp.sum(log_keep, axis=1, keepdims=True)

    def largest_carry():
        return jnp.max(functools.reduce(jnp.maximum, [ref[...] for ref in carry_refs]))

    def first_two_tiles():
        ks = pl.multiple_of((qi - 1) * t, t)
        col = lax.broadcasted_iota(jnp.int32, (t, 2 * t), 1)
        row = lax.broadcasted_iota(jnp.int32, (t, 2 * t), 0)
        visible = col - t < row
        for ci, hs in enumerate(heads):
            k = k_ref[pl.ds(ks, 2 * t), hs]
            v = v_ref[pl.ds(ks, 2 * t), hs]
            nz = lax.dot_general(q_neg[ci], k, _NT, preferred_element_type=F32) * c2
            log_keep = jnp.minimum(nz, 0.0) - jnp.log2(1.0 + jnp.exp2(-jnp.abs(nz)))
            log_keep = jnp.where(visible, log_keep, 0.0)
            hi = pltpu.bitcast(pltpu.bitcast(log_keep, jnp.uint32) & jnp.uint32(0xFFFF0000), F32)
            lo = log_keep - hi
            hi_lo = jnp.concatenate(
                [jnp.concatenate([hi[:, :t].astype(BF16), lo[:, :t].astype(BF16)], axis=1),
                 jnp.concatenate([hi[:, t:].astype(BF16), lo[:, t:].astype(BF16)], axis=1)], axis=0)
            later = jnp.dot(hi_lo, later_mask2, preferred_element_type=F32)
            sum_old = jnp.sum(log_keep[:, :t], axis=1, keepdims=True)
            sum_diag = jnp.sum(log_keep[:, t:], axis=1, keepdims=True)
            tot = jnp.concatenate([later[:t] + sum_diag, later[t:]], axis=1)
            a = jnp.where(visible, jnp.exp2(log_keep - nz + tot), 0.0)
            acc_refs[ci][...] = jnp.dot(a.astype(BF16), v, preferred_element_type=F32)
            carry_refs[ci][...] = jnp.broadcast_to(sum_old + sum_diag, carry_refs[ci].shape)

    @pl.when(qi == 0)
    def _():
        step(qi, True)

    @pl.when(qi > 0)
    def _():
        first_two_tiles()

    def cond(state):
        j, top = state
        return jnp.logical_and(j >= 0, top >= EXP2_UNDERFLOW)

    def body(state):
        j, _ = state
        step(j, False)
        return j - 1, largest_carry()

    lax.while_loop(cond, body, (qi - 2, largest_carry()))
    for ci, hs in enumerate(heads):
        o_ref[:, hs] = acc_refs[ci][...].astype(o_ref.dtype)


def sb_attention(qkv, heads, name):
    s = qkv.shape[0]
    dh = SB_HEAD_DIM
    t = min(256, s)
    hps = SB_HEADS_PER_STEP
    groups = heads // hps
    kern = functools.partial(_sb_kernel, t=t, hps=hps, c2=dh ** -0.5 / math.log(2.0))
    return pl.pallas_call(
        kern,
        grid=(groups, s // t),
        in_specs=[pl.BlockSpec((t, hps * dh), lambda h, i: (i, h)),
                  pl.BlockSpec((s, hps * dh), lambda h, i: (0, groups + h)),
                  pl.BlockSpec((s, hps * dh), lambda h, i: (0, 2 * groups + h))],
        out_specs=pl.BlockSpec((t, hps * dh), lambda h, i: (i, h)),
        out_shape=jax.ShapeDtypeStruct((s, heads * dh), BF16),
        scratch_shapes=[pltpu.VMEM((t, dh), F32)] * hps + [pltpu.VMEM((t, LANES), F32)] * hps,
        compiler_params=_params("arbitrary", "arbitrary"),
        name=name,
    )(qkv, qkv, qkv)


def _rope_lanes(x, cos, sin_up, sin_dn):
    half = MLA_ROPE_DIM // 2
    return (x * cos + pltpu.roll(x, half, 1) * sin_up + pltpu.roll(x, LANES - half, 1) * sin_dn)


def _mla_prep_kernel(a_ref, gq_ref, gkv_ref, cos_ref, sup_ref, sdn_ref, cq_ref, ckv_ref, kr_ref):
    a = a_ref[...]
    rq, rkv = MLA_Q_RANK, MLA_KV_RANK
    cq = a[:, :rq]
    cq_ref[...] = (cq * lax.rsqrt(jnp.mean(cq * cq, axis=-1, keepdims=True) + RMS_EPS)
                   * gq_ref[...]).astype(BF16)
    ckv = a[:, rq:rq + rkv]
    ckv_ref[...] = (ckv * lax.rsqrt(jnp.mean(ckv * ckv, axis=-1, keepdims=True) + RMS_EPS)
                    * gkv_ref[...]).astype(BF16)
    kr = a[:, rq + rkv:rq + rkv + LANES]
    kr_ref[...] = _rope_lanes(kr, cos_ref[...], sup_ref[...], sdn_ref[...]).astype(BF16)


def mla_prep(a, gq, gkv, cos, sin_up, sin_dn, name):
    s, n = a.shape
    tm = 512
    row = lambda w: pl.BlockSpec((tm, w), lambda i: (i, 0))
    vec = lambda w: pl.BlockSpec((1, w), lambda i: (0, 0))
    return pl.pallas_call(
        _mla_prep_kernel,
        grid=(s // tm,),
        in_specs=[row(n), vec(MLA_Q_RANK), vec(MLA_KV_RANK), row(LANES), row(LANES), row(LANES)],
        out_specs=[row(MLA_Q_RANK), row(MLA_KV_RANK), row(LANES)],
        out_shape=[jax.ShapeDtypeStruct((s, MLA_Q_RANK), BF16),
                   jax.ShapeDtypeStruct((s, MLA_KV_RANK), BF16),
                   jax.ShapeDtypeStruct((s, LANES), BF16)],
        compiler_params=_params("arbitrary"),
        name=name,
    )(a, gq.reshape(1, -1), gkv.reshape(1, -1), cos, sin_up, sin_dn)


def _mla_kernel(q_ref, kv_ref, kr_ref, cos_ref, sup_ref, sdn_ref, o_ref,
                m_ref, l_ref, acc_ref, *, tq, tk, hps, c2):
    qi = pl.program_id(1)
    wide = 2 * LANES
    cos, sin_up, sin_dn = cos_ref[...], sup_ref[...], sdn_ref[...]
    qc = []
    for ci in range(hps):
        q = q_ref[:, ci * wide:(ci + 1) * wide]
        q_rope = _rope_lanes(q[:, LANES:], cos, sin_up, sin_dn)
        qc.append(jnp.concatenate([q[:, :LANES].astype(BF16), q_rope.astype(BF16)], axis=1))
    m_ref[...] = jnp.full_like(m_ref, NEG_BIG)
    l_ref[...] = jnp.zeros_like(l_ref)
    acc_ref[...] = jnp.zeros_like(acc_ref)
    ahead = (lax.broadcasted_iota(jnp.int32, (tq, tk), 0) - lax.broadcasted_iota(jnp.int32, (tq, tk), 1))

    def step(j, diag):
        ks = pl.multiple_of(j * tk, tk)
        kr = kr_ref[pl.ds(ks, tk), :]
        for ci in range(hps):
            kn = kv_ref[pl.ds(ks, tk), ci * wide:ci * wide + LANES]
            v = kv_ref[pl.ds(ks, tk), ci * wide + LANES:(ci + 1) * wide]
            s = lax.dot_general(qc[ci], jnp.concatenate([kn, kr], axis=1), _NT,
                                preferred_element_type=F32) * c2
            if diag is not None:
                s = jnp.where(ahead >= diag * tk, s, NEG_BIG)
            m_prev = m_ref[ci]
            m_new = jnp.maximum(m_prev, jnp.max(s, axis=1, keepdims=True))
            alpha = jnp.exp2(m_prev - m_new)
            p = jnp.exp2(s - jnp.concatenate([m_new] * (tk // LANES), axis=1))
            l_ref[ci] = alpha * l_ref[ci] + jnp.sum(p, axis=1, keepdims=True)
            acc_ref[ci] = alpha * acc_ref[ci] + jnp.dot(p.astype(BF16), v, preferred_element_type=F32)
            m_ref[ci] = m_new

    def body(j, _):
        step(j, None)
        return 0

    per = tq // tk
    lax.fori_loop(0, qi * per, body, 0)
    for dj in range(per):
        step(qi * per + dj, dj)
    for ci in range(hps):
        o_ref[:, ci * LANES:(ci + 1) * LANES] = (acc_ref[ci] / l_ref[ci]).astype(o_ref.dtype)


def mla_attention_core(q, kv, k_rope, cos, sin_up, sin_dn, heads, name):
    assert MLA_V_DIM == LANES and MLA_NOPE_DIM == LANES
    s = q.shape[0]
    tq = min(MLA_QUERY_TILE, s)
    tk = min(MLA_KEY_TILE, s)
    hps = MLA_HEADS_PER_STEP
    kern = functools.partial(_mla_kernel, tq=tq, tk=tk, hps=hps,
                             c2=(MLA_NOPE_DIM + MLA_ROPE_DIM) ** -0.5 / math.log(2.0))
    tab = pl.BlockSpec((tq, LANES), lambda h, i: (i, 0))
    return pl.pallas_call(
        kern,
        grid=(heads // hps, s // tq),
        in_specs=[pl.BlockSpec((tq, hps * 2 * LANES), lambda h, i: (i, h)),
                  pl.BlockSpec((s, hps * 2 * LANES), lambda h, i: (0, h)),
                  pl.BlockSpec((s, LANES), lambda h, i: (0, 0)),
                  tab, tab, tab],
        out_specs=pl.BlockSpec((tq, hps * LANES), lambda h, i: (i, h)),
        out_shape=jax.ShapeDtypeStruct((s, heads * MLA_V_DIM), BF16),
        scratch_shapes=[pltpu.VMEM((hps, tq, LANES), F32), pltpu.VMEM((hps, tq, LANES), F32),
                        pltpu.VMEM((hps, tq, LANES), F32)],
        compiler_params=_params("arbitrary", "arbitrary"),
        name=name,
    )(q, kv, k_rope, cos, sin_up, sin_dn)


def _t5_bucket(rel):
    n = jnp.maximum(rel, 0)
    max_exact = REL_BUCKETS // 2
    nf = jnp.maximum(n, 1).astype(F32)
    large = max_exact + (jnp.log(nf / max_exact) / math.log(REL_MAX_DIST / max_exact)
                         * (REL_BUCKETS - max_exact)).astype(jnp.int32)
    large = jnp.minimum(large, REL_BUCKETS - 1)
    return jnp.where(n < max_exact, n, large)


def _dil_kernel(tab_ref, q_ref, kp_ref, kc_ref, vp_ref, vc_ref, pq_ref, pkp_ref, pkc_ref,
                o_ref, lse_ref, *, heads, scale):
    a = pl.program_id(1)
    w = DIL_WINDOW_KEYS
    dh = DIL_HEAD_DIM
    rel = pq_ref[...] - jnp.concatenate([pkp_ref[...], pkc_ref[...]], axis=1)
    back = (lax.broadcasted_iota(jnp.int32, (w, 2 * w), 0) + w
            - lax.broadcasted_iota(jnp.int32, (w, 2 * w), 1))
    col = lax.broadcasted_iota(jnp.int32, (w, 2 * w), 1)
    first_col = jnp.where(a > 0, 0, w)
    valid = (back >= 0) & (back <= w) & (col >= first_col)
    bucket = _t5_bucket(rel)
    bucket_chunks = [bucket[:, c * LANES:(c + 1) * LANES] for c in range(2 * w // LANES)]
    for h in range(heads):
        hs = slice(h * dh, (h + 1) * dh)
        table = jnp.broadcast_to(tab_ref[h:h + 1, :], (w, LANES))
        bias = jnp.concatenate([jnp.take_along_axis(table, chunk, axis=1) for chunk in bucket_chunks], axis=1)
        k = jnp.concatenate([kp_ref[:, hs], kc_ref[:, hs]], axis=0)
        v = jnp.concatenate([vp_ref[:, hs], vc_ref[:, hs]], axis=0)
        s = lax.dot_general(q_ref[:, hs], k, _NT, preferred_element_type=F32) * scale + bias
        s = jnp.where(valid, s, NEG_BIG)
        m = jnp.max(s, axis=1, keepdims=True)
        p = jnp.exp(s - m)
        l = jnp.sum(p, axis=1, keepdims=True)
        o_ref[:, hs] = jnp.dot(p.astype(BF16), v, preferred_element_type=F32) / l
        lse_ref[:, hs] = jnp.broadcast_to(m + jnp.log(l), (w, dh))


def dilated_group(qkv, pos, tab, g, name):
    s, n = qkv.shape
    d = DIL_DILATIONS[g]
    heads, dh, w = DIL_HEADS, DIL_HEAD_DIM, DIL_WINDOW_KEYS
    hw = heads * dh
    sub = s // d
    nb = sub // w
    pos_col = pos.reshape(sub, d).T.reshape(d, sub, 1)
    pos_row = pos.reshape(sub, d).T.reshape(d, 1, sub)
    prev = lambda i: jnp.maximum(i - 1, 0)
    kern = functools.partial(_dil_kernel, heads=heads, scale=dh ** -0.5)
    return pl.pallas_call(
        kern,
        grid=(d, nb),
        in_specs=[pl.BlockSpec((heads, LANES), lambda r, i: (0, 0)),
                  pl.BlockSpec((w, hw), lambda r, i: (r * nb + i, 0)),
                  pl.BlockSpec((w, hw), lambda r, i: (r * nb + prev(i), 1)),
                  pl.BlockSpec((w, hw), lambda r, i: (r * nb + i, 1)),
                  pl.BlockSpec((w, hw), lambda r, i: (r * nb + prev(i), 2)),
                  pl.BlockSpec((w, hw), lambda r, i: (r * nb + i, 2)),
                  pl.BlockSpec((None, w, 1), lambda r, i: (r, i, 0)),
                  pl.BlockSpec((None, 1, w), lambda r, i: (r, 0, prev(i))),
                  pl.BlockSpec((None, 1, w), lambda r, i: (r, 0, i))],
        out_specs=[pl.BlockSpec((w, hw), lambda r, i: (r * nb + i, 0)),
                   pl.BlockSpec((w, hw), lambda r, i: (r * nb + i, 0))],
        out_shape=[jax.ShapeDtypeStruct((s, hw), F32), jax.ShapeDtypeStruct((s, hw), F32)],
        compiler_params=_params("arbitrary", "arbitrary"),
        name=name,
    )(tab, qkv, qkv, qkv, qkv, qkv, pos_col, pos_row, pos_row)


def _dil_mix_kernel(*refs, dilations):
    n = len(dilations)
    out_ref = refs[2 * n]
    buffers = list(refs[2 * n + 1:])
    outs, lses = [], []
    for g, d in enumerate(dilations):
        o_ref, l_ref = refs[2 * g], refs[2 * g + 1]
        if d == 1:
            outs.append(o_ref[0])
            lses.append(l_ref[0])
            continue
        o_buf, l_buf = buffers.pop(0), buffers.pop(0)
        slabs, rows = o_buf.shape[0], o_buf.shape[1] // d
        for r in range(d):
            for c in range(slabs):
                lanes = slice(c * LANES, (c + 1) * LANES)
                o_buf.at[c][pl.ds(r, rows, stride=d), :] = o_ref[r, :, lanes]
                l_buf.at[c][pl.ds(r, rows, stride=d), :] = l_ref[r, :, lanes]
        outs.append(jnp.concatenate([o_buf[c] for c in range(slabs)], axis=1))
        lses.append(jnp.concatenate([l_buf[c] for c in range(slabs)], axis=1))
    m = functools.reduce(jnp.maximum, lses)
    es = [jnp.exp(l - m) for l in lses]
    num = functools.reduce(lambda a, b: a + b, [e * o for e, o in zip(es, outs)])
    out_ref[...] = (num / functools.reduce(lambda a, b: a + b, es)).astype(out_ref.dtype)


def dilated_mix(parts, name):
    s, hw = parts[0][0].shape
    tm = 256
    flat, specs, scratch = [], [], []
    for (o, lse), d in zip(parts, DIL_DILATIONS):
        for x in (o, lse):
            flat.append(x.reshape(d, s // d, hw))
            specs.append(pl.BlockSpec((d, tm // d, hw), lambda i: (0, i, 0)))
        if d > 1:
            scratch += [pltpu.VMEM((hw // LANES, tm, LANES), F32)] * 2
    return pl.pallas_call(
        functools.partial(_dil_mix_kernel, dilations=DIL_DILATIONS),
        grid=(s // tm,),
        in_specs=specs,
        out_specs=pl.BlockSpec((tm, hw), lambda i: (i, 0)),
        out_shape=jax.ShapeDtypeStruct((s, hw), BF16),
        scratch_shapes=scratch,
        compiler_params=_params("arbitrary"),
        name=name,
    )(*flat)


def _route_tile(x, w_ref, b_ref, route_ref, count_ref, run_ref):
    ng, ne = MOE_GROUPS, MOE_EXPERTS_PER_GROUP
    x_hi = x.astype(BF16)
    x_lo = (x - x_hi.astype(F32)).astype(BF16)
    logits = jnp.dot(jnp.concatenate([x_hi, x_lo, x_hi], axis=1), w_ref[...],
                     preferred_element_type=F32) + b_ref[...]
    lane = lax.broadcasted_iota(jnp.int32, logits.shape, 1).astype(F32)
    big = float(1 << 20)
    is_group = (lane >= ng * ne) & (lane < ng * ne + ng)
    g_logit = jnp.where(is_group, logits, NEG_BIG)
    g_max = jnp.max(g_logit, axis=1, keepdims=True)
    g_top = jnp.min(jnp.where(is_group & (g_logit == g_max), lane, big), axis=1, keepdims=True) - ng * ne
    g_gate = 1.0 / jnp.sum(jnp.where(is_group, jnp.exp(g_logit - g_max), 0.0), axis=1, keepdims=True)
    in_group = (lane >= g_top * ne) & (lane < (g_top + 1) * ne)
    e_logit = jnp.where(in_group, logits, NEG_BIG)
    v1 = jnp.max(e_logit, axis=1, keepdims=True)
    i1 = jnp.min(jnp.where(in_group & (e_logit == v1), lane, big), axis=1, keepdims=True)
    rest = in_group & (lane != i1)
    e_rest = jnp.where(rest, logits, NEG_BIG)
    v2 = jnp.max(e_rest, axis=1, keepdims=True)
    i2 = jnp.min(jnp.where(rest & (e_rest == v2), lane, big), axis=1, keepdims=True)
    e21 = jnp.exp(v2 - v1)
    w1 = g_gate / (1.0 + e21)
    w2 = g_gate * e21 / (1.0 + e21)
    gates = jnp.where(lane == i1, w1, jnp.where(lane == i2, w2, 0.0))

    @pl.when(pl.program_id(0) == 0)
    def _():
        run_ref[...] = jnp.zeros_like(run_ref)

    tm = x.shape[0]
    member = jnp.where(lane == g_top, 1.0, 0.0)
    earlier = (lax.broadcasted_iota(jnp.int32, (tm, tm), 1)
               < lax.broadcasted_iota(jnp.int32, (tm, tm), 0))
    before = jnp.dot(jnp.where(earlier, 1.0, 0.0).astype(BF16), member.astype(BF16),
                     preferred_element_type=F32)
    run = run_ref[...]
    rank = jnp.sum(jnp.where(lane == g_top, before + run, 0.0), axis=1, keepdims=True)
    run_ref[...] = run + jnp.sum(member, axis=0, keepdims=True)
    count_ref[...] = jnp.broadcast_to(run_ref[...], count_ref.shape)
    route_ref[...] = jnp.where(lane == ROUTE_RANK_LANE, rank,
                               jnp.where(lane == ROUTE_GROUP_LANE, g_top, gates))


def _ln_route_kernel(h_ref, mix_ref, g_ref, b_ref, w_ref, br_ref, of_ref, route_ref, count_ref, run_ref):
    _ln_kernel(h_ref, mix_ref, g_ref, b_ref, of_ref, None)
    _route_tile(of_ref[...], w_ref, br_ref, route_ref, count_ref, run_ref)


def residual_layer_norm_routed(h, mix, g, b, w_router, b_router, name):
    s, d = h.shape
    tm = ROW_TILE
    row = pl.BlockSpec((tm, d), lambda i: (i, 0))
    vec = pl.BlockSpec((1, d), lambda i: (0, 0))
    return pl.pallas_call(
        _ln_route_kernel,
        grid=(s // tm,),
        in_specs=[row, row, vec, vec,
                  pl.BlockSpec((3 * d, LANES), lambda i: (0, 0)),
                  pl.BlockSpec((1, LANES), lambda i: (0, 0))],
        out_specs=[row,
                   pl.BlockSpec((tm, LANES), lambda i: (i, 0)),
                   pl.BlockSpec((8, LANES), lambda i: (0, 0))],
        out_shape=[jax.ShapeDtypeStruct((s, d), F32),
                   jax.ShapeDtypeStruct((s, LANES), F32), jax.ShapeDtypeStruct((8, LANES), F32)],
        scratch_shapes=[pltpu.VMEM((1, LANES), F32)],
        compiler_params=_params("arbitrary"),
        name=name,
    )(h, mix, g.reshape(1, d), b.reshape(1, d), w_router, b_router)


def _pack_bf16_pair(lo, hi):
    lo_bits = pltpu.bitcast(lo.astype(BF16).astype(F32), jnp.uint32) >> 16
    hi_bits = pltpu.bitcast(hi.astype(BF16).astype(F32), jnp.uint32) & jnp.uint32(0xFFFF0000)
    return lo_bits | hi_bits


def _unpack_bf16_pair(packed):
    lo = pltpu.bitcast(packed << 16, F32).astype(BF16)
    hi = pltpu.bitcast(packed & jnp.uint32(0xFFFF0000), F32).astype(BF16)
    return lo, hi


def _dispatch_kernel(pos_ref, h_ref, route_ref, xs_in, xs_ref, row_ref, sem):
    del xs_in
    tm, d = h_ref.shape
    base = pl.program_id(0) * tm
    row_ref[:, :d // 2] = _pack_bf16_pair(h_ref[:, :d // 2], h_ref[:, d // 2:])
    row_ref[:, d // 2:] = pltpu.bitcast(route_ref[...], jnp.uint32)

    def row_copy(r):
        return pltpu.make_async_copy(row_ref.at[pl.ds(r, 1), :],
                                     xs_ref.at[pl.ds(pos_ref[base + r], 1), :], sem.at[0])

    def start(r, carry):
        row_copy(r).start()
        return carry

    def wait(r, carry):
        row_copy(r).wait()
        return carry

    lax.fori_loop(0, tm, start, 0, unroll=DMA_LOOP_UNROLL)
    lax.fori_loop(0, tm, wait, 0, unroll=DMA_LOOP_UNROLL)


def moe_dispatch(h, route, pos, rows, name):
    s, d = h.shape
    tm = ROW_TILE
    width = d // 2 + LANES
    grid_spec = pltpu.PrefetchScalarGridSpec(
        num_scalar_prefetch=1,
        grid=(s // tm,),
        in_specs=[pl.BlockSpec((tm, d), lambda i, pos: (i, 0)),
                  pl.BlockSpec((tm, LANES), lambda i, pos: (i, 0)),
                  pl.BlockSpec(memory_space=pl.ANY)],
        out_specs=pl.BlockSpec(memory_space=pl.ANY),
        scratch_shapes=[pltpu.VMEM((tm, width), jnp.uint32), pltpu.SemaphoreType.DMA((1,))],
    )
    return pl.pallas_call(
        _dispatch_kernel,
        grid_spec=grid_spec,
        out_shape=jax.ShapeDtypeStruct((rows, width), jnp.uint32),
        input_output_aliases={3: 0},
        compiler_params=_params("arbitrary"),
        name=name,
    )(pos, h, route, jnp.zeros((rows, width), jnp.uint32))


def _weights_changed(tg_ref, t):
    return jnp.logical_or(t == 0, tg_ref[t] != tg_ref[jnp.maximum(t - 1, 0)])


def _group_up_kernel(tg_ref, x_ref, route_ref, wg_ref, wu_ref, o_ref, wg_bf, wu_bf):
    e = pl.program_id(0)
    t = pl.program_id(1)

    @pl.when(_weights_changed(tg_ref, t))
    def _():
        wg_bf[...] = wg_ref[...].astype(BF16)
        wu_bf[...] = wu_ref[...].astype(BF16)

    x = jnp.concatenate(_unpack_bf16_pair(x_ref[...]), axis=1)
    gate = jnp.dot(x, wg_bf[...], preferred_element_type=F32)
    up = jnp.dot(x, wu_bf[...], preferred_element_type=F32)
    lane = lax.broadcasted_iota(jnp.int32, route_ref.shape, 1)
    flat_expert = tg_ref[t] * MOE_EXPERTS_PER_GROUP + e
    route = pltpu.bitcast(route_ref[...], F32)
    comb = jnp.sum(jnp.where(lane == flat_expert, route, 0.0), axis=1, keepdims=True)
    hidden = gate * jax.nn.sigmoid(gate) * up
    o_ref[...] = (hidden * comb).astype(o_ref.dtype)


def moe_group_up(xs, tile_group, w_gate, w_up, layer, name):
    rows = xs.shape[0]
    ne = MOE_EXPERTS_PER_GROUP
    d, f = w_gate.shape[-2:]
    tm = MOE_TILE
    wspec = pl.BlockSpec((None, None, d, f), lambda e, t, tg: (layer, tg[t] * ne + e, 0, 0))
    grid_spec = pltpu.PrefetchScalarGridSpec(
        num_scalar_prefetch=1,
        grid=(ne, rows // tm),
        in_specs=[pl.BlockSpec((tm, d // 2), lambda e, t, tg: (t, 0)),
                  pl.BlockSpec((tm, LANES), lambda e, t, tg: (t, d // 2 // LANES)),
                  wspec, wspec],
        out_specs=pl.BlockSpec((tm, f), lambda e, t, tg: (t, e)),
        scratch_shapes=[pltpu.VMEM((d, f), BF16), pltpu.VMEM((d, f), BF16)],
    )
    return pl.pallas_call(
        _group_up_kernel,
        grid_spec=grid_spec,
        out_shape=jax.ShapeDtypeStruct((rows, ne * f), BF16),
        compiler_params=_params("arbitrary", "arbitrary"),
        name=name,
    )(tile_group, xs, xs, w_gate, w_up)


def _group_down_kernel(tg_ref, x_ref, w_ref, o_ref, w_bf):
    @pl.when(_weights_changed(tg_ref, pl.program_id(1)))
    def _():
        w_bf[...] = w_ref[...].astype(BF16)

    o_ref[...] = jnp.dot(x_ref[...], w_bf[...], preferred_element_type=F32)


def moe_group_down(hidden, tile_group, w_down, layer, name):
    rows, k = hidden.shape
    d = w_down.shape[-1]
    tm = MOE_TILE
    tn = min(1024, d)
    w4 = w_down.reshape(w_down.shape[0], MOE_GROUPS, k, d)
    grid_spec = pltpu.PrefetchScalarGridSpec(
        num_scalar_prefetch=1,
        grid=(d // tn, rows // tm),
        in_specs=[pl.BlockSpec((tm, k), lambda j, t, tg: (t, 0)),
                  pl.BlockSpec((None, None, k, tn), lambda j, t, tg: (layer, tg[t], 0, j))],
        out_specs=pl.BlockSpec((tm, tn), lambda j, t, tg: (t, j)),
        scratch_shapes=[pltpu.VMEM((k, tn), BF16)],
    )
    return pl.pallas_call(
        _group_down_kernel,
        grid_spec=grid_spec,
        out_shape=jax.ShapeDtypeStruct((rows, d), F32),
        compiler_params=_params("arbitrary", "arbitrary"),
        name=name,
    )(tile_group, hidden, w4)


def _ln_gather_kernel(pos_ref, h_ref, ys_ref, g_ref, b_ref, of_ref, ob_ref, *rest, dilations):
    strided_refs, (y_buf, sem, *stage) = rest[:len(dilations)], rest[len(dilations):]
    tm = h_ref.shape[0]
    i = pl.program_id(0)

    def row_copy(tile, r):
        return pltpu.make_async_copy(ys_ref.at[pl.ds(pos_ref[tile * tm + r], 1), :],
                                     y_buf.at[tile % 2, pl.ds(r, 1), :], sem.at[tile % 2])

    def start_tile(tile):
        def start(r, carry):
            row_copy(tile, r).start()
            return carry
        lax.fori_loop(0, tm, start, 0, unroll=DMA_LOOP_UNROLL)

    def wait_tile(tile):
        def wait(r, carry):
            row_copy(tile, r).wait()
            return carry
        lax.fori_loop(0, tm, wait, 0, unroll=DMA_LOOP_UNROLL)

    @pl.when(i == 0)
    def _():
        start_tile(i)

    @pl.when(i + 1 < pl.num_programs(0))
    def _():
        start_tile(i + 1)

    wait_tile(i)
    _ln_kernel(h_ref, y_buf.at[i % 2], g_ref, b_ref, of_ref, ob_ref, *strided_refs, dilations=dilations,
               stage_ref=stage[0] if stage else None)


def residual_layer_norm_gathered(h, ys, pos, g, b, name, dilations=()):
    s, d = h.shape
    tm = ROW_TILE
    row = pl.BlockSpec((tm, d), lambda i, pos: (i, 0))
    vec = pl.BlockSpec((1, d), lambda i, pos: (0, 0))
    grid_spec = pltpu.PrefetchScalarGridSpec(
        num_scalar_prefetch=1,
        grid=(s // tm,),
        in_specs=[row, pl.BlockSpec(memory_space=pl.ANY), vec, vec],
        out_specs=[row, row] + [pl.BlockSpec((dil, tm // dil, d), lambda i, pos: (0, i, 0))
                                for dil in dilations],
        scratch_shapes=[pltpu.VMEM((2, tm, d), F32), pltpu.SemaphoreType.DMA((2,))]
        + ([pltpu.VMEM((d // LANES, tm, LANES), F32)] if dilations else []),
    )
    return pl.pallas_call(
        functools.partial(_ln_gather_kernel, dilations=tuple(dilations)),
        grid_spec=grid_spec,
        out_shape=[jax.ShapeDtypeStruct((s, d), F32), jax.ShapeDtypeStruct((s, d), BF16)]
        + [jax.ShapeDtypeStruct((dil, s // dil, d), BF16) for dil in dilations],
        compiler_params=_params("arbitrary"),
        name=name,
    )(pos, h, ys, g.reshape(1, d), b.reshape(1, d))


def router_weights(w_gr, b_gr, w_er, b_er):
    d = w_gr.shape[0]
    ng, ne = MOE_GROUPS, MOE_EXPERTS_PER_GROUP
    w_router = jnp.concatenate(
        [jnp.transpose(w_er, (1, 0, 2)).reshape(d, ng * ne), w_gr,
         jnp.zeros((d, LANES - ng * ne - ng), F32)], axis=1)
    b_router = jnp.concatenate(
        [b_er.reshape(ng * ne), b_gr, jnp.zeros((LANES - ng * ne - ng,), F32)]).reshape(1, LANES)
    w_hi = w_router.astype(BF16)
    w_lo = (w_router - w_hi.astype(F32)).astype(BF16)
    return jnp.concatenate([w_hi, w_hi, w_lo], axis=0), b_router


def hierarchical_moe(h_f32, route, counts, w_gate, w_up, w_down, layer, tag):
    s, d = h_f32.shape
    ng = MOE_GROUPS
    tm = MOE_TILE
    counts = counts[0, :ng].astype(jnp.int32)
    padded = (counts + tm - 1) // tm * tm
    ends = jnp.cumsum(padded)
    rank = route[:, ROUTE_RANK_LANE].astype(jnp.int32)
    group = route[:, ROUTE_GROUP_LANE].astype(jnp.int32)
    pos = (ends - padded)[group] + rank
    rows = s + ng * tm
    tile_start = jnp.arange(rows // tm, dtype=jnp.int32) * tm
    tile_group = jnp.minimum(jnp.sum(tile_start[:, None] >= ends[None, :], axis=1), ng - 1).astype(jnp.int32)
    xs = moe_dispatch(h_f32, route, pos, rows, f"moe_dispatch_{tag}")
    hidden = moe_group_up(xs, tile_group, w_gate, w_up, layer, f"moe_up_{tag}")
    ys = moe_group_down(hidden, tile_group, w_down, layer, f"moe_down_{tag}")
    return ys, pos


def stick_breaking_mixer(h_bf16, w_qkv, w_o, j, tag):
    qkv = matmul(h_bf16, w_qkv, BF16, f"sb_qkv_{tag}", lead=(j,))
    o = sb_attention(qkv, SB_HEADS, f"sb_attn_{tag}")
    return matmul(o, w_o, F32, f"sb_out_{tag}", lead=(j,))


def _rope_tables(pos):
    half = MLA_ROPE_DIM // 2
    inv_freq = ROPE_THETA ** (-jnp.arange(half, dtype=F32) / half)
    ang = pos.astype(F32)[:, None] * inv_freq
    cos, sin = jnp.cos(ang), jnp.sin(ang)
    zeros = jnp.zeros((pos.shape[0], LANES - 2 * half), F32)
    z_half = jnp.zeros_like(sin)
    cos_t = jnp.concatenate([cos, cos, zeros], axis=1)
    sin_up = jnp.concatenate([z_half, sin, zeros], axis=1)
    sin_dn = jnp.concatenate([-sin, z_half, zeros], axis=1)
    return cos_t, sin_up, sin_dn


def mla_mixer(h_bf16, pos, w_q_a, q_a_norm, w_q_b, w_kv_a, kv_a_norm, w_kv_b, w_o, j, tag):
    d = h_bf16.shape[1]
    heads = MLA_HEADS
    nope, rope = MLA_NOPE_DIM, MLA_ROPE_DIM
    used = MLA_Q_RANK + MLA_KV_RANK + rope
    width = -(-(MLA_Q_RANK + MLA_KV_RANK + LANES) // 512) * 512
    w_a = jnp.concatenate([w_q_a, w_kv_a, jnp.zeros((d, width - used), F32)], axis=1)
    a = matmul(h_bf16, w_a, F32, f"mla_a_{tag}")
    cos_t, sin_up, sin_dn = _rope_tables(pos)
    cq, ckv, k_rope = mla_prep(a, q_a_norm, kv_a_norm, cos_t, sin_up, sin_dn, f"mla_prep_{tag}")
    w_qb = jnp.pad(w_q_b.reshape(MLA_Q_RANK, heads, nope + rope),
                   ((0, 0), (0, 0), (0, 2 * LANES - nope - rope))).reshape(MLA_Q_RANK, heads * 2 * LANES)
    q = matmul(cq, w_qb, F32, f"mla_qb_{tag}")
    kv = matmul(ckv, w_kv_b, BF16, f"mla_kvb_{tag}", lead=(j,))
    o = mla_attention_core(q, kv, k_rope, cos_t, sin_up, sin_dn, heads, f"mla_attn_{tag}")
    return matmul(o, w_o, F32, f"mla_out_{tag}", lead=(j,))


def dilated_mixer(h_by_dilation, pos, rel_bias, w_qkv, w_o, j, tag):
    groups = DIL_GROUPS
    hw = DIL_HEADS * DIL_HEAD_DIM
    tabs = rel_bias.reshape(REL_BUCKETS, groups, DIL_HEADS)
    parts = []
    for g, d in enumerate(DIL_DILATIONS):
        def col_block(jt, tn, g=g):
            per = hw // tn
            return ((jt // per) * groups + g) * per + jt % per
        qkv = matmul(h_by_dilation[d], w_qkv, BF16, f"dil_qkv_{tag}_g{g}", lead=(j,),
                     n_out=3 * hw, col_block=col_block)
        tab = jnp.pad(tabs[:, g, :].T, ((0, 0), (0, LANES - REL_BUCKETS)))
        parts.append(dilated_group(qkv, pos, tab, g, f"dil_attn_{tag}_g{g}"))
    o = dilated_mix(parts, f"dil_mix_{tag}")
    return matmul(o, w_o, F32, f"dil_out_{tag}", lead=(j,))


def kernel(x, positions, rel_bias, sb_w_qkv, sb_w_o, mla_w_q_a, mla_q_a_norm, mla_w_q_b, mla_w_kv_a,
           mla_kv_a_norm, mla_w_kv_b, mla_w_o, dil_w_qkv, dil_w_o, ln_gain, ln_bias,
           moe_w_group_router, moe_b_group_router, moe_w_expert_router, moe_b_expert_router,
           moe_w_gate, moe_w_up, moe_w_down):
    batch, seq, d = x.shape
    outs = []
    for b in range(batch):
        h = x[b]
        h_bf16 = h.astype(BF16)
        pos = positions[b]
        strided = ()
        assert N_MIXERS > 2 and DEPTH > 0
        for i in range(DEPTH):
            kind, j = i % N_MIXERS, i // N_MIXERS
            tag = f"l{i}"
            if kind == 0:
                mix = stick_breaking_mixer(h_bf16, sb_w_qkv, sb_w_o, j, tag)
            elif kind == 1:
                mix = mla_mixer(h_bf16, pos, mla_w_q_a[j], mla_q_a_norm[j], mla_w_q_b[j], mla_w_kv_a[j],
                                mla_kv_a_norm[j], mla_w_kv_b, mla_w_o, j, tag)
            else:
                h_by_dilation = {1: h_bf16}
                h_by_dilation.update({dil: hs.reshape(seq, d) for dil, hs in zip(next_dilations, strided)})
                mix = dilated_mixer(h_by_dilation, pos, rel_bias, dil_w_qkv, dil_w_o, j, tag)
            w_router, b_router = router_weights(moe_w_group_router[i], moe_b_group_router[i],
                                                moe_w_expert_router[i], moe_b_expert_router[i])
            h, route, counts = residual_layer_norm_routed(
                h, mix, ln_gain[i, 0], ln_bias[i, 0], w_router, b_router, f"ln_mix_{tag}")
            ys, slot = hierarchical_moe(h, route, counts, moe_w_gate, moe_w_up, moe_w_down, i, tag)
            feeds_dilated = i + 1 < DEPTH and (i + 1) % N_MIXERS == 2
            next_dilations = tuple(dil for dil in DIL_DILATIONS if dil > 1) if feeds_dilated else ()
            h, h_bf16, *strided = residual_layer_norm_gathered(h, ys, slot, ln_gain[i, 1], ln_bias[i, 1],
                                                               f"ln_ffn_{tag}", dilations=next_dilations)
        outs.append(h)
    return jnp.stack(outs)
```

```python
import functools
import math

import jax
import jax.numpy as jnp
from jax import lax
from jax.experimental import pallas as pl
from jax.experimental.pallas import tpu as pltpu

DEPTH = 4
N_MIXERS = 3
LN_EPS = 1e-5
RMS_EPS = 1e-6

SB_HEADS = 32
SB_HEAD_DIM = 128

MLA_HEADS = 32
MLA_Q_RANK = 1024
MLA_KV_RANK = 512
MLA_NOPE_DIM = 128
MLA_ROPE_DIM = 64
MLA_V_DIM = 128
ROPE_THETA = 10000.0

DIL_DILATIONS = (1, 4, 16)
DIL_GROUPS = 3
DIL_HEADS = 16
DIL_HEAD_DIM = 128
DIL_WINDOW_KEYS = 128

REL_BUCKETS = 32
REL_MAX_DIST = 2048

MOE_GROUPS = 4
MOE_EXPERTS_PER_GROUP = 8
MOE_TOP_K = 2
EXPERT_HIDDEN = 256

DN_ALPHA = (2 * DEPTH) ** 0.25

LANES = 128
VMEM_LIMIT_BYTES = 56 * 1024 * 1024
MATMUL_VMEM_BUDGET = 40 * 1024 * 1024
NEG_BIG = -1e30
EXP2_UNDERFLOW = -151.0
SB_HEADS_PER_STEP = 2
MLA_HEADS_PER_STEP = 2
MLA_QUERY_TILE = 1024
MLA_KEY_TILE = 512
ROW_TILE = 256
DMA_LOOP_UNROLL = 8
MOE_TILE = 256
ROUTE_INFO_LANE = MOE_GROUPS * MOE_EXPERTS_PER_GROUP

BF16 = jnp.bfloat16
F32 = jnp.float32

_NT = (((1,), (1,)), ((), ()))


def _params(*sem):
    return pltpu.CompilerParams(dimension_semantics=sem, vmem_limit_bytes=VMEM_LIMIT_BYTES)


def _mm_kernel(x_ref, w_ref, o_ref, wbf_ref):
    @pl.when(pl.program_id(1) == 0)
    def _():
        wbf_ref[...] = w_ref[...].astype(BF16)

    o_ref[...] = jnp.dot(x_ref[...], wbf_ref[...], preferred_element_type=F32).astype(o_ref.dtype)


def _mm_tiles(m, k, n, out_bytes):
    for tm, tn in ((1024, 512), (512, 512), (512, 256), (256, 256), (256, 128), (128, 128)):
        if m % tm or n % tn:
            continue
        need = 2 * tm * k * 2 + 2 * k * tn * 4 + k * tn * 2 + 2 * tm * tn * out_bytes
        if need <= MATMUL_VMEM_BUDGET:
            return tm, tn
    raise ValueError(f"no matmul tiling for {(m, k, n)}")


def matmul(x, w, out_dtype, name, lead=(), n_out=None, col_block=None):
    m, k = x.shape
    n = w.shape[-1] if n_out is None else n_out
    tm, tn = _mm_tiles(m, k, n, jnp.dtype(out_dtype).itemsize)
    wcol = (lambda j: j) if col_block is None else (lambda j: col_block(j, tn))
    return pl.pallas_call(
        _mm_kernel,
        grid=(n // tn, m // tm),
        in_specs=[pl.BlockSpec((tm, k), lambda j, i: (i, 0)),
                  pl.BlockSpec((None,) * len(lead) + (k, tn), lambda j, i: tuple(lead) + (0, wcol(j)))],
        out_specs=pl.BlockSpec((tm, tn), lambda j, i: (i, j)),
        out_shape=jax.ShapeDtypeStruct((m, n), out_dtype),
        scratch_shapes=[pltpu.VMEM((k, tn), BF16)],
        compiler_params=_params("arbitrary", "arbitrary"),
        name=name,
    )(x, w)


def _ln_kernel(h_ref, mix_ref, g_ref, b_ref, of_ref, ob_ref, *strided_refs, dilations=(), stage_ref=None):
    x = DN_ALPHA * h_ref[...] + mix_ref[...]
    mu = jnp.mean(x, axis=-1, keepdims=True)
    xc = x - mu
    var = jnp.mean(xc * xc, axis=-1, keepdims=True)
    y = xc * lax.rsqrt(var + LN_EPS) * g_ref[...] + b_ref[...]
    of_ref[...] = y
    if ob_ref is not None:
        ob_ref[...] = y.astype(BF16)
    if not dilations:
        return
    rows, width = y.shape
    slabs = [slice(c * LANES, (c + 1) * LANES) for c in range(width // LANES)]
    for c, lanes in enumerate(slabs):
        stage_ref[c] = y[:, lanes]
    for ref, d in zip(strided_refs, dilations):
        for r in range(d):
            for c, lanes in enumerate(slabs):
                ref[r, :, lanes] = stage_ref.at[c][pl.ds(r, rows // d, stride=d), :].astype(BF16)


def _sb_kernel(q_ref, k_ref, v_ref, o_ref, *scratch, t, hps, c2):
    qi = pl.program_id(1)
    dh = SB_HEAD_DIM
    heads = [slice(c * dh, (c + 1) * dh) for c in range(hps)]
    q_neg = [-q_ref[:, hs] for hs in heads]
    acc_refs, carry_refs = scratch[:hps], scratch[hps:]
    for ref in scratch:
        ref[...] = jnp.zeros_like(ref)
    r = lax.broadcasted_iota(jnp.int32, (t, t), 0)
    c = lax.broadcasted_iota(jnp.int32, (t, t), 1)
    strict = c < r
    later_mask = jnp.where(r > c, 1.0, 0.0).astype(BF16)
    later_mask2 = jnp.concatenate([later_mask, later_mask], axis=0)

    def step(j, masked):
        ks = pl.multiple_of(j * t, t)
        for ci, hs in enumerate(heads):
            k = k_ref[pl.ds(ks, t), hs]
            v = v_ref[pl.ds(ks, t), hs]
            nz = lax.dot_general(q_neg[ci], k, _NT, preferred_element_type=F32) * c2
            log_keep = jnp.minimum(nz, 0.0) - jnp.log2(1.0 + jnp.exp2(-jnp.abs(nz)))
            if masked:
                log_keep = jnp.where(strict, log_keep, 0.0)
            hi = pltpu.bitcast(pltpu.bitcast(log_keep, jnp.uint32) & jnp.uint32(0xFFFF0000), F32)
            hi_lo = jnp.concatenate([hi.astype(BF16), (log_keep - hi).astype(BF16)], axis=1)
            later = jnp.dot(hi_lo, later_mask2, preferred_element_type=F32)
            carry = carry_refs[ci][...]
            tot = later + jnp.concatenate([carry] * (t // LANES), axis=1)
            a = jnp.exp2(log_keep - nz + tot)
            if masked:
                a = jnp.where(strict, a, 0.0)
            acc_refs[ci][...] += jnp.dot(a.astype(BF16), v, preferred_element_type=F32)
            carry_refs[ci][...] = carry + jnp.sum(log_keep, axis=1, keepdims=True)

    def largest_carry():
        return jnp.max(functools.reduce(jnp.maximum, [ref[...] for ref in carry_refs]))

    def first_two_tiles():
        ks = pl.multiple_of((qi - 1) * t, t)
        col = lax.broadcasted_iota(jnp.int32, (t, 2 * t), 1)
        row = lax.broadcasted_iota(jnp.int32, (t, 2 * t), 0)
        visible = col - t < row
        for ci, hs in enumerate(heads):
            k = k_ref[pl.ds(ks, 2 * t), hs]
            v = v_ref[pl.ds(ks, 2 * t), hs]
            nz = lax.dot_general(q_neg[ci], k, _NT, preferred_element_type=F32) * c2
            log_keep = jnp.minimum(nz, 0.0) - jnp.log2(1.0 + jnp.exp2(-jnp.abs(nz)))
            log_keep = jnp.where(visible, log_keep, 0.0)
            hi = pltpu.bitcast(pltpu.bitcast(log_keep, jnp.uint32) & jnp.uint32(0xFFFF0000), F32)
            lo = log_keep - hi
            hi_lo = jnp.concatenate(
                [jnp.concatenate([hi[:, :t].astype(BF16), lo[:, :t].astype(BF16)], axis=1),
                 jnp.concatenate([hi[:, t:].astype(BF16), lo[:, t:].astype(BF16)], axis=1)], axis=0)
            later = jnp.dot(hi_lo, later_mask2, preferred_element_type=F32)
            sum_old = jnp.sum(log_keep[:, :t], axis=1, keepdims=True)
            sum_diag = jnp.sum(log_keep[:, t:], axis=1, keepdims=True)
            tot = jnp.concatenate([later[:t] + sum_diag, later[t:]], axis=1)
            a = jnp.where(visible, jnp.exp2(log_keep - nz + tot), 0.0)
            acc_refs[ci][...] = jnp.dot(a.astype(BF16), v, preferred_element_type=F32)
            carry_refs[ci][...] = jnp.broadcast_to(sum_old + sum_diag, carry_refs[ci].shape)

    @pl.when(qi == 0)
    def _():
        step(qi, True)

    @pl.when(qi > 0)
    def _():
        first_two_tiles()

    def cond(state):
        j, top = state
        return jnp.logical_and(j >= 0, top >= EXP2_UNDERFLOW)

    def body(state):
        j, _ = state
        step(j, False)
        return j - 1, largest_carry()

    lax.while_loop(cond, body, (qi - 2, largest_carry()))
    for ci, hs in enumerate(heads):
        o_ref[:, hs] = acc_refs[ci][...].astype(o_ref.dtype)


def sb_attention(qkv, heads, name):
    s = qkv.shape[0]
    dh = SB_HEAD_DIM
    t = min(256, s)
    hps = SB_HEADS_PER_STEP
    groups = heads // hps
    kern = functools.partial(_sb_kernel, t=t, hps=hps, c2=dh ** -0.5 / math.log(2.0))
    return pl.pallas_call(
        kern,
        grid=(groups, s // t),
        in_specs=[pl.BlockSpec((t, hps * dh), lambda h, i: (i, h)),
                  pl.BlockSpec((s, hps * dh), lambda h, i: (0, groups + h)),
                  pl.BlockSpec((s, hps * dh), lambda h, i: (0, 2 * groups + h))],
        out_specs=pl.BlockSpec((t, hps * dh), lambda h, i: (i, h)),
        out_shape=jax.ShapeDtypeStruct((s, heads * dh), BF16),
        scratch_shapes=[pltpu.VMEM((t, dh), F32)] * hps + [pltpu.VMEM((t, LANES), F32)] * hps,
        compiler_params=_params("arbitrary", "arbitrary"),
        name=name,
    )(qkv, qkv, qkv)


def _rope_lanes(x, cos, sin_up, sin_dn):
    half = MLA_ROPE_DIM // 2
    return (x * cos + pltpu.roll(x, half, 1) * sin_up + pltpu.roll(x, LANES - half, 1) * sin_dn)


def _mla_prep_kernel(a_ref, gq_ref, gkv_ref, cos_ref, sup_ref, sdn_ref, cq_ref, ckv_ref, kr_ref):
    a = a_ref[...]
    rq, rkv = MLA_Q_RANK, MLA_KV_RANK
    cq = a[:, :rq]
    cq_ref[...] = (cq * lax.rsqrt(jnp.mean(cq * cq, axis=-1, keepdims=True) + RMS_EPS)
                   * gq_ref[...]).astype(BF16)
    ckv = a[:, rq:rq + rkv]
    ckv_ref[...] = (ckv * lax.rsqrt(jnp.mean(ckv * ckv, axis=-1, keepdims=True) + RMS_EPS)
                    * gkv_ref[...]).astype(BF16)
    kr = a[:, rq + rkv:rq + rkv + LANES]
    kr_ref[...] = _rope_lanes(kr, cos_ref[...], sup_ref[...], sdn_ref[...]).astype(BF16)


def mla_prep(a, gq, gkv, cos, sin_up, sin_dn, name):
    s, n = a.shape
    tm = 512
    row = lambda w: pl.BlockSpec((tm, w), lambda i: (i, 0))
    vec = lambda w: pl.BlockSpec((1, w), lambda i: (0, 0))
    return pl.pallas_call(
        _mla_prep_kernel,
        grid=(s // tm,),
        in_specs=[row(n), vec(MLA_Q_RANK), vec(MLA_KV_RANK), row(LANES), row(LANES), row(LANES)],
        out_specs=[row(MLA_Q_RANK), row(MLA_KV_RANK), row(LANES)],
        out_shape=[jax.ShapeDtypeStruct((s, MLA_Q_RANK), BF16),
                   jax.ShapeDtypeStruct((s, MLA_KV_RANK), BF16),
                   jax.ShapeDtypeStruct((s, LANES), BF16)],
        compiler_params=_params("arbitrary"),
        name=name,
    )(a, gq.reshape(1, -1), gkv.reshape(1, -1), cos, sin_up, sin_dn)


def _mla_kernel(q_ref, kv_ref, kr_ref, cos_ref, sup_ref, sdn_ref, o_ref,
                m_ref, l_ref, acc_ref, *, tq, tk, hps, c2):
    qi = pl.program_id(1)
    wide = 2 * LANES
    cos, sin_up, sin_dn = cos_ref[...], sup_ref[...], sdn_ref[...]
    qc = []
    for ci in range(hps):
        q = q_ref[:, ci * wide:(ci + 1) * wide]
        q_rope = _rope_lanes(q[:, LANES:], cos, sin_up, sin_dn)
        qc.append(jnp.concatenate([q[:, :LANES].astype(BF16), q_rope.astype(BF16)], axis=1))
    m_ref[...] = jnp.full_like(m_ref, NEG_BIG)
    l_ref[...] = jnp.zeros_like(l_ref)
    acc_ref[...] = jnp.zeros_like(acc_ref)
    ahead = (lax.broadcasted_iota(jnp.int32, (tq, tk), 0) - lax.broadcasted_iota(jnp.int32, (tq, tk), 1))

    def step(j, diag):
        ks = pl.multiple_of(j * tk, tk)
        kr = kr_ref[pl.ds(ks, tk), :]
        for ci in range(hps):
            kn = kv_ref[pl.ds(ks, tk), ci * wide:ci * wide + LANES]
            v = kv_ref[pl.ds(ks, tk), ci * wide + LANES:(ci + 1) * wide]
            s = lax.dot_general(qc[ci], jnp.concatenate([kn, kr], axis=1), _NT,
                                preferred_element_type=F32) * c2
            if diag is not None:
                s = jnp.where(ahead >= diag * tk, s, NEG_BIG)
            m_prev = m_ref[ci]
            m_new = jnp.maximum(m_prev, jnp.max(s, axis=1, keepdims=True))
            alpha = jnp.exp2(m_prev - m_new)
            p = jnp.exp2(s - jnp.concatenate([m_new] * (tk // LANES), axis=1))
            l_ref[ci] = alpha * l_ref[ci] + jnp.sum(p, axis=1, keepdims=True)
            acc_ref[ci] = alpha * acc_ref[ci] + jnp.dot(p.astype(BF16), v, preferred_element_type=F32)
            m_ref[ci] = m_new

    def body(j, _):
        step(j, None)
        return 0

    per = tq // tk
    lax.fori_loop(0, qi * per, body, 0)
    for dj in range(per):
        step(qi * per + dj, dj)
    for ci in range(hps):
        o_ref[:, ci * LANES:(ci + 1) * LANES] = (acc_ref[ci] / l_ref[ci]).astype(o_ref.dtype)


def mla_attention_core(q, kv, k_rope, cos, sin_up, sin_dn, heads, name):
    assert MLA_V_DIM == LANES and MLA_NOPE_DIM == LANES
    s = q.shape[0]
    tq = min(MLA_QUERY_TILE, s)
    tk = min(MLA_KEY_TILE, s)
    hps = MLA_HEADS_PER_STEP
    kern = functools.partial(_mla_kernel, tq=tq, tk=tk, hps=hps,
                             c2=(MLA_NOPE_DIM + MLA_ROPE_DIM) ** -0.5 / math.log(2.0))
    tab = pl.BlockSpec((tq, LANES), lambda h, i: (i, 0))
    return pl.pallas_call(
        kern,
        grid=(heads // hps, s // tq),
        in_specs=[pl.BlockSpec((tq, hps * 2 * LANES), lambda h, i: (i, h)),
                  pl.BlockSpec((s, hps * 2 * LANES), lambda h, i: (0, h)),
                  pl.BlockSpec((s, LANES), lambda h, i: (0, 0)),
                  tab, tab, tab],
        out_specs=pl.BlockSpec((tq, hps * LANES), lambda h, i: (i, h)),
        out_shape=jax.ShapeDtypeStruct((s, heads * MLA_V_DIM), BF16),
        scratch_shapes=[pltpu.VMEM((hps, tq, LANES), F32), pltpu.VMEM((hps, tq, LANES), F32),
                        pltpu.VMEM((hps, tq, LANES), F32)],
        compiler_params=_params("arbitrary", "arbitrary"),
        name=name,
    )(q, kv, k_rope, cos, sin_up, sin_dn)


def _t5_bucket(rel):
    n = jnp.maximum(rel, 0)
    max_exact = REL_BUCKETS // 2
    nf = jnp.maximum(n, 1).astype(F32)
    large = max_exact + (jnp.log(nf / max_exact) / math.log(REL_MAX_DIST / max_exact)
                         * (REL_BUCKETS - max_exact)).astype(jnp.int32)
    large = jnp.minimum(large, REL_BUCKETS - 1)
    return jnp.where(n < max_exact, n, large)


def _dil_kernel(tab_ref, q_ref, kp_ref, kc_ref, vp_ref, vc_ref, pq_ref, pkp_ref, pkc_ref,
                o_ref, lse_ref, *, heads, scale):
    a = pl.program_id(1)
    w = DIL_WINDOW_KEYS
    dh = DIL_HEAD_DIM
    rel = pq_ref[...] - jnp.concatenate([pkp_ref[...], pkc_ref[...]], axis=1)
    back = (lax.broadcasted_iota(jnp.int32, (w, 2 * w), 0) + w
            - lax.broadcasted_iota(jnp.int32, (w, 2 * w), 1))
    col = lax.broadcasted_iota(jnp.int32, (w, 2 * w), 1)
    first_col = jnp.where(a > 0, 0, w)
    valid = (back >= 0) & (back <= w) & (col >= first_col)
    bucket = _t5_bucket(rel)
    bucket_chunks = [bucket[:, c * LANES:(c + 1) * LANES] for c in range(2 * w // LANES)]
    for h in range(heads):
        hs = slice(h * dh, (h + 1) * dh)
        table = jnp.broadcast_to(tab_ref[h:h + 1, :], (w, LANES))
        bias = jnp.concatenate([jnp.take_along_axis(table, chunk, axis=1) for chunk in bucket_chunks], axis=1)
        k = jnp.concatenate([kp_ref[:, hs], kc_ref[:, hs]], axis=0)
        v = jnp.concatenate([vp_ref[:, hs], vc_ref[:, hs]], axis=0)
        s = lax.dot_general(q_ref[:, hs], k, _NT, preferred_element_type=F32) * scale + bias
        s = jnp.where(valid, s, NEG_BIG)
        m = jnp.max(s, axis=1, keepdims=True)
        p = jnp.exp(s - m)
        l = jnp.sum(p, axis=1, keepdims=True)
        o_ref[:, hs] = jnp.dot(p.astype(BF16), v, preferred_element_type=F32) / l
        lse_ref[:, hs] = jnp.broadcast_to(m + jnp.log(l), (w, dh))


def dilated_group(qkv, pos, tab, g, name):
    s, n = qkv.shape
    d = DIL_DILATIONS[g]
    heads, dh, w = DIL_HEADS, DIL_HEAD_DIM, DIL_WINDOW_KEYS
    hw = heads * dh
    sub = s // d
    nb = sub // w
    pos_col = pos.reshape(sub, d).T.reshape(d, sub, 1)
    pos_row = pos.reshape(sub, d).T.reshape(d, 1, sub)
    prev = lambda i: jnp.maximum(i - 1, 0)
    kern = functools.partial(_dil_kernel, heads=heads, scale=dh ** -0.5)
    return pl.pallas_call(
        kern,
        grid=(d, nb),
        in_specs=[pl.BlockSpec((heads, LANES), lambda r, i: (0, 0)),
                  pl.BlockSpec((w, hw), lambda r, i: (r * nb + i, 0)),
                  pl.BlockSpec((w, hw), lambda r, i: (r * nb + prev(i), 1)),
                  pl.BlockSpec((w, hw), lambda r, i: (r * nb + i, 1)),
                  pl.BlockSpec((w, hw), lambda r, i: (r * nb + prev(i), 2)),
                  pl.BlockSpec((w, hw), lambda r, i: (r * nb + i, 2)),
                  pl.BlockSpec((None, w, 1), lambda r, i: (r, i, 0)),
                  pl.BlockSpec((None, 1, w), lambda r, i: (r, 0, prev(i))),
                  pl.BlockSpec((None, 1, w), lambda r, i: (r, 0, i))],
        out_specs=[pl.BlockSpec((w, hw), lambda r, i: (r * nb + i, 0)),
                   pl.BlockSpec((w, hw), lambda r, i: (r * nb + i, 0))],
        out_shape=[jax.ShapeDtypeStruct((s, hw), F32), jax.ShapeDtypeStruct((s, hw), F32)],
        compiler_params=_params("arbitrary", "arbitrary"),
        name=name,
    )(tab, qkv, qkv, qkv, qkv, qkv, pos_col, pos_row, pos_row)


def _dil_mix_kernel(*refs, dilations):
    n = len(dilations)
    out_ref = refs[2 * n]
    buffers = list(refs[2 * n + 1:])
    outs, lses = [], []
    for g, d in enumerate(dilations):
        o_ref, l_ref = refs[2 * g], refs[2 * g + 1]
        if d == 1:
            outs.append(o_ref[0])
            lses.append(l_ref[0])
            continue
        o_buf, l_buf = buffers.pop(0), buffers.pop(0)
        slabs, rows = o_buf.shape[0], o_buf.shape[1] // d
        for r in range(d):
            for c in range(slabs):
                lanes = slice(c * LANES, (c + 1) * LANES)
                o_buf.at[c][pl.ds(r, rows, stride=d), :] = o_ref[r, :, lanes]
                l_buf.at[c][pl.ds(r, rows, stride=d), :] = l_ref[r, :, lanes]
        outs.append(jnp.concatenate([o_buf[c] for c in range(slabs)], axis=1))
        lses.append(jnp.concatenate([l_buf[c] for c in range(slabs)], axis=1))
    m = functools.reduce(jnp.maximum, lses)
    es = [jnp.exp(l - m) for l in lses]
    num = functools.reduce(lambda a, b: a + b, [e * o for e, o in zip(es, outs)])
    out_ref[...] = (num / functools.reduce(lambda a, b: a + b, es)).astype(out_ref.dtype)


def dilated_mix(parts, name):
    s, hw = parts[0][0].shape
    tm = 256
    flat, specs, scratch = [], [], []
    for (o, lse), d in zip(parts, DIL_DILATIONS):
        for x in (o, lse):
            flat.append(x.reshape(d, s // d, hw))
            specs.append(pl.BlockSpec((d, tm // d, hw), lambda i: (0, i, 0)))
        if d > 1:
            scratch += [pltpu.VMEM((hw // LANES, tm, LANES), F32)] * 2
    return pl.pallas_call(
        functools.partial(_dil_mix_kernel, dilations=DIL_DILATIONS),
        grid=(s // tm,),
        in_specs=specs,
        out_specs=pl.BlockSpec((tm, hw), lambda i: (i, 0)),
        out_shape=jax.ShapeDtypeStruct((s, hw), BF16),
        scratch_shapes=scratch,
        compiler_params=_params("arbitrary"),
        name=name,
    )(*flat)


def _route_tile(x, w_ref, b_ref, route_ref, count_ref, run_ref):
    ng, ne = MOE_GROUPS, MOE_EXPERTS_PER_GROUP
    x_hi = x.astype(BF16)
    x_lo = (x - x_hi.astype(F32)).astype(BF16)
    logits = jnp.dot(jnp.concatenate([x_hi, x_lo, x_hi], axis=1), w_ref[...],
                     preferred_element_type=F32) + b_ref[...]
    lane = lax.broadcasted_iota(jnp.int32, logits.shape, 1).astype(F32)
    big = float(1 << 20)
    is_group = (lane >= ng * ne) & (lane < ng * ne + ng)
    g_logit = jnp.where(is_group, logits, NEG_BIG)
    g_max = jnp.max(g_logit, axis=1, keepdims=True)
    g_top = jnp.min(jnp.where(is_group & (g_logit == g_max), lane, big), axis=1, keepdims=True) - ng * ne
    g_gate = 1.0 / jnp.sum(jnp.where(is_group, jnp.exp(g_logit - g_max), 0.0), axis=1, keepdims=True)
    in_group = (lane >= g_top * ne) & (lane < (g_top + 1) * ne)
    e_logit = jnp.where(in_group, logits, NEG_BIG)
    v1 = jnp.max(e_logit, axis=1, keepdims=True)
    i1 = jnp.min(jnp.where(in_group & (e_logit == v1), lane, big), axis=1, keepdims=True)
    rest = in_group & (lane != i1)
    e_rest = jnp.where(rest, logits, NEG_BIG)
    v2 = jnp.max(e_rest, axis=1, keepdims=True)
    i2 = jnp.min(jnp.where(rest & (e_rest == v2), lane, big), axis=1, keepdims=True)
    e21 = jnp.exp(v2 - v1)
    w1 = g_gate / (1.0 + e21)
    w2 = g_gate * e21 / (1.0 + e21)
    gates = jnp.where(lane == i1, w1, jnp.where(lane == i2, w2, 0.0))

    @pl.when(pl.program_id(0) == 0)
    def _():
        run_ref[...] = jnp.zeros_like(run_ref)

    tm = x.shape[0]
    member = jnp.where((lane == i1) | (lane == i2), 1.0, 0.0)
    earlier = (lax.broadcasted_iota(jnp.int32, (tm, tm), 1)
               < lax.broadcasted_iota(jnp.int32, (tm, tm), 0))
    before = jnp.dot(jnp.where(earlier, 1.0, 0.0).astype(BF16), member.astype(BF16),
                     preferred_element_type=F32)
    run = run_ref[...]
    seen = before + run
    rank1 = jnp.sum(jnp.where(lane == i1, seen, 0.0), axis=1, keepdims=True)
    rank2 = jnp.sum(jnp.where(lane == i2, seen, 0.0), axis=1, keepdims=True)
    run_ref[...] = run + jnp.sum(member, axis=0, keepdims=True)
    count_ref[...] = jnp.broadcast_to(run_ref[...], count_ref.shape)
    route = gates
    for slot, value in enumerate((rank1, rank2, i1, i2)):
        route = jnp.where(lane == ROUTE_INFO_LANE + slot, value, route)
    route_ref[...] = route


def _ln_route_kernel(h_ref, mix_ref, g_ref, b_ref, w_ref, br_ref, of_ref, route_ref, count_ref, run_ref):
    _ln_kernel(h_ref, mix_ref, g_ref, b_ref, of_ref, None)
    _route_tile(of_ref[...], w_ref, br_ref, route_ref, count_ref, run_ref)


def residual_layer_norm_routed(h, mix, g, b, w_router, b_router, name):
    s, d = h.shape
    tm = ROW_TILE
    row = pl.BlockSpec((tm, d), lambda i: (i, 0))
    vec = pl.BlockSpec((1, d), lambda i: (0, 0))
    return pl.pallas_call(
        _ln_route_kernel,
        grid=(s // tm,),
        in_specs=[row, row, vec, vec,
                  pl.BlockSpec((3 * d, LANES), lambda i: (0, 0)),
                  pl.BlockSpec((1, LANES), lambda i: (0, 0))],
        out_specs=[row,
                   pl.BlockSpec((tm, LANES), lambda i: (i, 0)),
                   pl.BlockSpec((8, LANES), lambda i: (0, 0))],
        out_shape=[jax.ShapeDtypeStruct((s, d), F32),
                   jax.ShapeDtypeStruct((s, LANES), F32), jax.ShapeDtypeStruct((8, LANES), F32)],
        scratch_shapes=[pltpu.VMEM((1, LANES), F32)],
        compiler_params=_params("arbitrary"),
        name=name,
    )(h, mix, g.reshape(1, d), b.reshape(1, d), w_router, b_router)


def _pack_bf16_pair(lo, hi):
    lo_bits = pltpu.bitcast(lo.astype(BF16).astype(F32), jnp.uint32) >> 16
    hi_bits = pltpu.bitcast(hi.astype(BF16).astype(F32), jnp.uint32) & jnp.uint32(0xFFFF0000)
    return lo_bits | hi_bits


def _unpack_bf16_pair(packed):
    lo = pltpu.bitcast(packed << 16, F32).astype(BF16)
    hi = pltpu.bitcast(packed & jnp.uint32(0xFFFF0000), F32).astype(BF16)
    return lo, hi


def _dispatch_kernel(pos_ref, h_ref, route_ref, xs_in, xs_ref, row_ref, sem):
    del xs_in
    tm, d = h_ref.shape
    base = pl.program_id(0) * tm
    row_ref[:, :d // 2] = _pack_bf16_pair(h_ref[:, :d // 2], h_ref[:, d // 2:])
    row_ref[:, d // 2:] = pltpu.bitcast(route_ref[...], jnp.uint32)

    def row_copies(r):
        return [pltpu.make_async_copy(row_ref.at[pl.ds(r, 1), :],
                                      xs_ref.at[pl.ds(pos_ref[k, base + r], 1), :], sem.at[0])
                for k in range(MOE_TOP_K)]

    def start(r, carry):
        for cp in row_copies(r):
            cp.start()
        return carry

    def wait(r, carry):
        for cp in row_copies(r):
            cp.wait()
        return carry

    lax.fori_loop(0, tm, start, 0, unroll=DMA_LOOP_UNROLL)
    lax.fori_loop(0, tm, wait, 0, unroll=DMA_LOOP_UNROLL)


def moe_dispatch(h, route, pos, rows, name):
    s, d = h.shape
    tm = ROW_TILE
    width = d // 2 + LANES
    grid_spec = pltpu.PrefetchScalarGridSpec(
        num_scalar_prefetch=1,
        grid=(s // tm,),
        in_specs=[pl.BlockSpec((tm, d), lambda i, pos: (i, 0)),
                  pl.BlockSpec((tm, LANES), lambda i, pos: (i, 0)),
                  pl.BlockSpec(memory_space=pl.ANY)],
        out_specs=pl.BlockSpec(memory_space=pl.ANY),
        scratch_shapes=[pltpu.VMEM((tm, width), jnp.uint32), pltpu.SemaphoreType.DMA((1,))],
    )
    return pl.pallas_call(
        _dispatch_kernel,
        grid_spec=grid_spec,
        out_shape=jax.ShapeDtypeStruct((rows, width), jnp.uint32),
        input_output_aliases={3: 0},
        compiler_params=_params("arbitrary"),
        name=name,
    )(pos, h, route, jnp.zeros((rows, width), jnp.uint32))


def _expert_kernel(te_ref, x_ref, route_ref, wg_ref, wu_ref, wd_ref, o_ref, wg_bf, wu_bf, wd_bf):
    t = pl.program_id(0)
    expert = te_ref[t]

    @pl.when(jnp.logical_or(t == 0, expert != te_ref[jnp.maximum(t - 1, 0)]))
    def _():
        wg_bf[...] = wg_ref[...].astype(BF16)
        wu_bf[...] = wu_ref[...].astype(BF16)
        wd_bf[...] = wd_ref[...].astype(BF16)

    x = jnp.concatenate(_unpack_bf16_pair(x_ref[...]), axis=1)
    gate = jnp.dot(x, wg_bf[...], preferred_element_type=F32)
    up = jnp.dot(x, wu_bf[...], preferred_element_type=F32)
    lane = lax.broadcasted_iota(jnp.int32, route_ref.shape, 1)
    route = pltpu.bitcast(route_ref[...], F32)
    comb = jnp.sum(jnp.where(lane == expert, route, 0.0), axis=1, keepdims=True)
    hidden = (gate * jax.nn.sigmoid(gate) * up * comb).astype(BF16)
    o_ref[...] = jnp.dot(hidden, wd_bf[...], preferred_element_type=F32)


def moe_experts(xs, tile_expert, w_gate, w_up, w_down, layer, name):
    rows = xs.shape[0]
    d, f = w_gate.shape[-2:]
    tm = MOE_TILE
    up_spec = pl.BlockSpec((None, None, d, f), lambda t, te: (layer, te[t], 0, 0))
    grid_spec = pltpu.PrefetchScalarGridSpec(
        num_scalar_prefetch=1,
        grid=(rows // tm,),
        in_specs=[pl.BlockSpec((tm, d // 2), lambda t, te: (t, 0)),
                  pl.BlockSpec((tm, LANES), lambda t, te: (t, d // 2 // LANES)),
                  up_spec, up_spec,
                  pl.BlockSpec((None, None, f, d), lambda t, te: (layer, te[t], 0, 0))],
        out_specs=pl.BlockSpec((tm, d), lambda t, te: (t, 0)),
        scratch_shapes=[pltpu.VMEM((d, f), BF16), pltpu.VMEM((d, f), BF16), pltpu.VMEM((f, d), BF16)],
    )
    return pl.pallas_call(
        _expert_kernel,
        grid_spec=grid_spec,
        out_shape=jax.ShapeDtypeStruct((rows, d), F32),
        compiler_params=_params("arbitrary"),
        name=name,
    )(tile_expert, xs, xs, w_gate, w_up, w_down)


def _ln_gather_kernel(pos_ref, h_ref, ys_ref, g_ref, b_ref, of_ref, ob_ref, *rest, dilations):
    strided_refs, (y_buf, sem, *stage) = rest[:len(dilations)], rest[len(dilations):]
    tm = h_ref.shape[0]
    i = pl.program_id(0)

    def row_copies(tile, r):
        return [pltpu.make_async_copy(ys_ref.at[pl.ds(pos_ref[k, tile * tm + r], 1), :],
                                      y_buf.at[tile % 2, k, pl.ds(r, 1), :], sem.at[tile % 2])
                for k in range(MOE_TOP_K)]

    def start_tile(tile):
        def start(r, carry):
            for cp in row_copies(tile, r):
                cp.start()
            return carry
        lax.fori_loop(0, tm, start, 0, unroll=DMA_LOOP_UNROLL)

    def wait_tile(tile):
        def wait(r, carry):
            for cp in row_copies(tile, r):
                cp.wait()
            return carry
        lax.fori_loop(0, tm, wait, 0, unroll=DMA_LOOP_UNROLL)

    @pl.when(i == 0)
    def _():
        start_tile(i)

    @pl.when(i + 1 < pl.num_programs(0))
    def _():
        start_tile(i + 1)

    wait_tile(i)
    ffn_ref = y_buf.at[i % 2, 0]
    ffn_ref[...] = functools.reduce(lambda a, b: a + b, [y_buf[i % 2, k] for k in range(MOE_TOP_K)])
    _ln_kernel(h_ref, ffn_ref, g_ref, b_ref, of_ref, ob_ref, *strided_refs, dilations=dilations,
               stage_ref=stage[0] if stage else None)


def residual_layer_norm_gathered(h, ys, pos, g, b, name, dilations=()):
    s, d = h.shape
    tm = ROW_TILE
    row = pl.BlockSpec((tm, d), lambda i, pos: (i, 0))
    vec = pl.BlockSpec((1, d), lambda i, pos: (0, 0))
    grid_spec = pltpu.PrefetchScalarGridSpec(
        num_scalar_prefetch=1,
        grid=(s // tm,),
        in_specs=[row, pl.BlockSpec(memory_space=pl.ANY), vec, vec],
        out_specs=[row, row] + [pl.BlockSpec((dil, tm // dil, d), lambda i, pos: (0, i, 0))
                                for dil in dilations],
        scratch_shapes=[pltpu.VMEM((2, MOE_TOP_K, tm, d), F32), pltpu.SemaphoreType.DMA((2,))]
        + ([pltpu.VMEM((d // LANES, tm, LANES), F32)] if dilations else []),
    )
    return pl.pallas_call(
        functools.partial(_ln_gather_kernel, dilations=tuple(dilations)),
        grid_spec=grid_spec,
        out_shape=[jax.ShapeDtypeStruct((s, d), F32), jax.ShapeDtypeStruct((s, d), BF16)]
        + [jax.ShapeDtypeStruct((dil, s // dil, d), BF16) for dil in dilations],
        compiler_params=_params("arbitrary"),
        name=name,
    )(pos, h, ys, g.reshape(1, d), b.reshape(1, d))


def router_weights(w_gr, b_gr, w_er, b_er):
    d = w_gr.shape[0]
    ng, ne = MOE_GROUPS, MOE_EXPERTS_PER_GROUP
    w_router = jnp.concatenate(
        [jnp.transpose(w_er, (1, 0, 2)).reshape(d, ng * ne), w_gr,
         jnp.zeros((d, LANES - ng * ne - ng), F32)], axis=1)
    b_router = jnp.concatenate(
        [b_er.reshape(ng * ne), b_gr, jnp.zeros((LANES - ng * ne - ng,), F32)]).reshape(1, LANES)
    w_hi = w_router.astype(BF16)
    w_lo = (w_router - w_hi.astype(F32)).astype(BF16)
    return jnp.concatenate([w_hi, w_hi, w_lo], axis=0), b_router


def hierarchical_moe(h_f32, route, counts, w_gate, w_up, w_down, layer, tag):
    s, d = h_f32.shape
    n_experts = MOE_GROUPS * MOE_EXPERTS_PER_GROUP
    tm = MOE_TILE
    counts = counts[0, :n_experts].astype(jnp.int32)
    padded = (counts + tm - 1) // tm * tm
    ends = jnp.cumsum(padded)
    starts = ends - padded
    info = route[:, ROUTE_INFO_LANE:ROUTE_INFO_LANE + 2 * MOE_TOP_K].astype(jnp.int32)
    pos = jnp.stack([starts[info[:, MOE_TOP_K + k]] + info[:, k] for k in range(MOE_TOP_K)])
    rows = MOE_TOP_K * s + n_experts * tm
    tile_start = jnp.arange(rows // tm, dtype=jnp.int32) * tm
    tile_expert = jnp.minimum(jnp.sum(tile_start[:, None] >= ends[None, :], axis=1),
                              n_experts - 1).astype(jnp.int32)
    xs = moe_dispatch(h_f32, route, pos, rows, f"moe_dispatch_{tag}")
    ys = moe_experts(xs, tile_expert, w_gate, w_up, w_down, layer, f"moe_experts_{tag}")
    return ys, pos


def stick_breaking_mixer(h_bf16, w_qkv, w_o, j, tag):
    qkv = matmul(h_bf16, w_qkv, BF16, f"sb_qkv_{tag}", lead=(j,))
    o = sb_attention(qkv, SB_HEADS, f"sb_attn_{tag}")
    return matmul(o, w_o, F32, f"sb_out_{tag}", lead=(j,))


def _rope_tables(pos):
    half = MLA_ROPE_DIM // 2
    inv_freq = ROPE_THETA ** (-jnp.arange(half, dtype=F32) / half)
    ang = pos.astype(F32)[:, None] * inv_freq
    cos, sin = jnp.cos(ang), jnp.sin(ang)
    zeros = jnp.zeros((pos.shape[0], LANES - 2 * half), F32)
    z_half = jnp.zeros_like(sin)
    cos_t = jnp.concatenate([cos, cos, zeros], axis=1)
    sin_up = jnp.concatenate([z_half, sin, zeros], axis=1)
    sin_dn = jnp.concatenate([-sin, z_half, zeros], axis=1)
    return cos_t, sin_up, sin_dn


def mla_mixer(h_bf16, pos, w_q_a, q_a_norm, w_q_b, w_kv_a, kv_a_norm, w_kv_b, w_o, j, tag):
    d = h_bf16.shape[1]
    heads = MLA_HEADS
    nope, rope = MLA_NOPE_DIM, MLA_ROPE_DIM
    used = MLA_Q_RANK + MLA_KV_RANK + rope
    width = -(-(MLA_Q_RANK + MLA_KV_RANK + LANES) // 512) * 512
    w_a = jnp.concatenate([w_q_a, w_kv_a, jnp.zeros((d, width - used), F32)], axis=1)
    a = matmul(h_bf16, w_a, F32, f"mla_a_{tag}")
    cos_t, sin_up, sin_dn = _rope_tables(pos)
    cq, ckv, k_rope = mla_prep(a, q_a_norm, kv_a_norm, cos_t, sin_up, sin_dn, f"mla_prep_{tag}")
    w_qb = jnp.pad(w_q_b.reshape(MLA_Q_RANK, heads, nope + rope),
                   ((0, 0), (0, 0), (0, 2 * LANES - nope - rope))).reshape(MLA_Q_RANK, heads * 2 * LANES)
    q = matmul(cq, w_qb, F32, f"mla_qb_{tag}")
    kv = matmul(ckv, w_kv_b, BF16, f"mla_kvb_{tag}", lead=(j,))
    o = mla_attention_core(q, kv, k_rope, cos_t, sin_up, sin_dn, heads, f"mla_attn_{tag}")
    return matmul(o, w_o, F32, f"mla_out_{tag}", lead=(j,))


def dilated_mixer(h_by_dilation, pos, rel_bias, w_qkv, w_o, j, tag):
    groups = DIL_GROUPS
    hw = DIL_HEADS * DIL_HEAD_DIM
    tabs = rel_bias.reshape(REL_BUCKETS, groups, DIL_HEADS)
    parts = []
    for g, d in enumerate(DIL_DILATIONS):
        def col_block(jt, tn, g=g):
            per = hw // tn
            return ((jt // per) * groups + g) * per + jt % per
        qkv = matmul(h_by_dilation[d], w_qkv, BF16, f"dil_qkv_{tag}_g{g}", lead=(j,),
                     n_out=3 * hw, col_block=col_block)
        tab = jnp.pad(tabs[:, g, :].T, ((0, 0), (0, LANES - REL_BUCKETS)))
        parts.append(dilated_group(qkv, pos, tab, g, f"dil_attn_{tag}_g{g}"))
    o = dilated_mix(parts, f"dil_mix_{tag}")
    return matmul(o, w_o, F32, f"dil_out_{tag}", lead=(j,))


def kernel(x, positions, rel_bias, sb_w_qkv, sb_w_o, mla_w_q_a, mla_q_a_norm, mla_w_q_b, mla_w_kv_a,
           mla_kv_a_norm, mla_w_kv_b, mla_w_o, dil_w_qkv, dil_w_o, ln_gain, ln_bias,
           moe_w_group_router, moe_b_group_router, moe_w_expert_router, moe_b_expert_router,
           moe_w_gate, moe_w_up, moe_w_down):
    batch, seq, d = x.shape
    outs = []
    for b in range(batch):
        h = x[b]
        h_bf16 = h.astype(BF16)
        pos = positions[b]
        strided = ()
        assert N_MIXERS > 2 and DEPTH > 0
        for i in range(DEPTH):
            kind, j = i % N_MIXERS, i // N_MIXERS
            tag = f"l{i}"
            if kind == 0:
                mix = stick_breaking_mixer(h_bf16, sb_w_qkv, sb_w_o, j, tag)
            elif kind == 1:
                mix = mla_mixer(h_bf16, pos, mla_w_q_a[j], mla_q_a_norm[j], mla_w_q_b[j], mla_w_kv_a[j],
                                mla_kv_a_norm[j], mla_w_kv_b, mla_w_o, j, tag)
            else:
                h_by_dilation = {1: h_bf16}
                h_by_dilation.update({dil: hs.reshape(seq, d) for dil, hs in zip(next_dilations, strided)})
                mix = dilated_mixer(h_by_dilation, pos, rel_bias, dil_w_qkv, dil_w_o, j, tag)
            w_router, b_router = router_weights(moe_w_group_router[i], moe_b_group_router[i],
                                                moe_w_expert_router[i], moe_b_expert_router[i])
            h, route, counts = residual_layer_norm_routed(
                h, mix, ln_gain[i, 0], ln_bias[i, 0], w_router, b_router, f"ln_mix_{tag}")
            ys, slot = hierarchical_moe(h, route, counts, moe_w_gate, moe_w_up, moe_w_down, i, tag)
            feeds_dilated = i + 1 < DEPTH and (i + 1) % N_MIXERS == 2
            next_dilations = tuple(dil for dil in DIL_DILATIONS if dil > 1) if feeds_dilated else ()
            h, h_bf16, *strided = residual_layer_norm_gathered(h, ys, slot, ln_gain[i, 1], ln_bias[i, 1],
                                                               f"ln_ffn_{tag}", dilations=next_dilations)
        outs.append(h)
    return jnp.stack(outs)
```

```python
import functools
import math

import jax
import jax.numpy as jnp
from jax import lax
from jax.experimental import pallas as pl
from jax.experimental.pallas import tpu as pltpu

DEPTH = 4
N_MIXERS = 3
LN_EPS = 1e-5
RMS_EPS = 1e-6

SB_HEADS = 32
SB_HEAD_DIM = 128

MLA_HEADS = 32
MLA_Q_RANK = 1024
MLA_KV_RANK = 512
MLA_NOPE_DIM = 128
MLA_ROPE_DIM = 64
MLA_V_DIM = 128
ROPE_THETA = 10000.0

DIL_DILATIONS = (1, 4, 16)
DIL_GROUPS = 3
DIL_HEADS = 16
DIL_HEAD_DIM = 128
DIL_WINDOW_KEYS = 128

REL_BUCKETS = 32
REL_MAX_DIST = 2048

MOE_GROUPS = 4
MOE_EXPERTS_PER_GROUP = 8
MOE_TOP_K = 2
EXPERT_HIDDEN = 256

DN_ALPHA = (2 * DEPTH) ** 0.25

LANES = 128
VMEM_LIMIT_BYTES = 56 * 1024 * 1024
MATMUL_VMEM_BUDGET = 40 * 1024 * 1024
NEG_BIG = -1e30
EXP2_UNDERFLOW = -151.0
SB_HEADS_PER_STEP = 2
MLA_HEADS_PER_STEP = 2
MLA_QUERY_TILE = 1024
MLA_KEY_TILE = 512
ROW_TILE = 256
DMA_LOOP_UNROLL = 8
MOE_TILE = 256
ROUTE_INFO_LANE = MOE_GROUPS * MOE_EXPERTS_PER_GROUP

BF16 = jnp.bfloat16
F32 = jnp.float32

_NT = (((1,), (1,)), ((), ()))


def _params(*sem):
    return pltpu.CompilerParams(dimension_semantics=sem, vmem_limit_bytes=VMEM_LIMIT_BYTES)


def _mm_kernel(x_ref, w_ref, o_ref, wbf_ref):
    @pl.when(pl.program_id(1) == 0)
    def _():
        wbf_ref[...] = w_ref[...].astype(BF16)

    o_ref[...] = jnp.dot(x_ref[...], wbf_ref[...], preferred_element_type=F32).astype(o_ref.dtype)


def _mm_tiles(m, k, n, out_bytes):
    for tm, tn in ((1024, 512), (512, 512), (512, 256), (256, 256), (256, 128), (128, 128)):
        if m % tm or n % tn:
            continue
        need = 2 * tm * k * 2 + 2 * k * tn * 4 + k * tn * 2 + 2 * tm * tn * out_bytes
        if need <= MATMUL_VMEM_BUDGET:
            return tm, tn
    raise ValueError(f"no matmul tiling for {(m, k, n)}")


def matmul(x, w, out_dtype, name, lead=(), n_out=None, col_block=None):
    m, k = x.shape
    n = w.shape[-1] if n_out is None else n_out
    tm, tn = _mm_tiles(m, k, n, jnp.dtype(out_dtype).itemsize)
    wcol = (lambda j: j) if col_block is None else (lambda j: col_block(j, tn))
    return pl.pallas_call(
        _mm_kernel,
        grid=(n // tn, m // tm),
        in_specs=[pl.BlockSpec((tm, k), lambda j, i: (i, 0)),
                  pl.BlockSpec((None,) * len(lead) + (k, tn), lambda j, i: tuple(lead) + (0, wcol(j)))],
        out_specs=pl.BlockSpec((tm, tn), lambda j, i: (i, j)),
        out_shape=jax.ShapeDtypeStruct((m, n), out_dtype),
        scratch_shapes=[pltpu.VMEM((k, tn), BF16)],
        compiler_params=_params("arbitrary", "arbitrary"),
        name=name,
    )(x, w)


def _ln_kernel(h_ref, mix_ref, g_ref, b_ref, of_ref, ob_ref, *strided_refs, dilations=(), stage_ref=None):
    x = DN_ALPHA * h_ref[...] + mix_ref[...]
    mu = jnp.mean(x, axis=-1, keepdims=True)
    xc = x - mu
    var = jnp.mean(xc * xc, axis=-1, keepdims=True)
    y = xc * lax.rsqrt(var + LN_EPS) * g_ref[...] + b_ref[...]
    of_ref[...] = y
    if ob_ref is not None:
        ob_ref[...] = y.astype(BF16)
    if not dilations:
        return
    rows, width = y.shape
    slabs = [slice(c * LANES, (c + 1) * LANES) for c in range(width // LANES)]
    for c, lanes in enumerate(slabs):
        stage_ref[c] = y[:, lanes]
    for ref, d in zip(strided_refs, dilations):
        for r in range(d):
            for c, lanes in enumerate(slabs):
                ref[r, :, lanes] = stage_ref.at[c][pl.ds(r, rows // d, stride=d), :].astype(BF16)


def _sb_kernel(q_ref, k_ref, v_ref, o_ref, *scratch, t, hps, c2):
    qi = pl.program_id(1)
    dh = SB_HEAD_DIM
    heads = [slice(c * dh, (c + 1) * dh) for c in range(hps)]
    q_neg = [-q_ref[:, hs] for hs in heads]
    acc_refs, carry_refs = scratch[:hps], scratch[hps:]
    for ref in scratch:
        ref[...] = jnp.zeros_like(ref)
    r = lax.broadcasted_iota(jnp.int32, (t, t), 0)
    c = lax.broadcasted_iota(jnp.int32, (t, t), 1)
    strict = c < r
    later_mask = jnp.where(r > c, 1.0, 0.0).astype(BF16)
    later_mask2 = jnp.concatenate([later_mask, later_mask], axis=0)

    def step(j, masked):
        ks = pl.multiple_of(j * t, t)
        for ci, hs in enumerate(heads):
            k = k_ref[pl.ds(ks, t), hs]
            v = v_ref[pl.ds(ks, t), hs]
            nz = lax.dot_general(q_neg[ci], k, _NT, preferred_element_type=F32) * c2
            log_keep = jnp.minimum(nz, 0.0) - jnp.log2(1.0 + jnp.exp2(-jnp.abs(nz)))
            if masked:
                log_keep = jnp.where(strict, log_keep, 0.0)
            hi = pltpu.bitcast(pltpu.bitcast(log_keep, jnp.uint32) & jnp.uint32(0xFFFF0000), F32)
            hi_lo = jnp.concatenate([hi.astype(BF16), (log_keep - hi).astype(BF16)], axis=1)
            later = jnp.dot(hi_lo, later_mask2, preferred_element_type=F32)
            carry = carry_refs[ci][...]
            tot = later + jnp.concatenate([carry] * (t // LANES), axis=1)
            a = jnp.exp2(log_keep - nz + tot)
            if masked:
                a = jnp.where(strict, a, 0.0)
            acc_refs[ci][...] += jnp.dot(a.astype(BF16), v, preferred_element_type=F32)
            carry_refs[ci][...] = carry + jnp.sum(log_keep, axis=1, keepdims=True)

    def largest_carry():
        return jnp.max(functools.reduce(jnp.maximum, [ref[...] for ref in carry_refs]))

    def first_two_tiles():
        ks = pl.multiple_of((qi - 1) * t, t)
        col = lax.broadcasted_iota(jnp.int32, (t, 2 * t), 1)
        row = lax.broadcasted_iota(jnp.int32, (t, 2 * t), 0)
        visible = col - t < row
        for ci, hs in enumerate(heads):
            k = k_ref[pl.ds(ks, 2 * t), hs]
            v = v_ref[pl.ds(ks, 2 * t), hs]
            nz = lax.dot_general(q_neg[ci], k, _NT, preferred_element_type=F32) * c2
            log_keep = jnp.minimum(nz, 0.0) - jnp.log2(1.0 + jnp.exp2(-jnp.abs(nz)))
            log_keep = jnp.where(visible, log_keep, 0.0)
            hi = pltpu.bitcast(pltpu.bitcast(log_keep, jnp.uint32) & jnp.uint32(0xFFFF0000), F32)
            lo = log_keep - hi
            hi_lo = jnp.concatenate(
                [jnp.concatenate([hi[:, :t].astype(BF16), lo[:, :t].astype(BF16)], axis=1),
                 jnp.concatenate([hi[:, t:].astype(BF16), lo[:, t:].astype(BF16)], axis=1)], axis=0)
            later = jnp.dot(hi_lo, later_mask2, preferred_element_type=F32)
            sum_old = jnp.sum(log_keep[:, :t], axis=1, keepdims=True)
            sum_diag = jnp.sum(log_keep[:, t:], axis=1, keepdims=True)
            tot = jnp.concatenate([later[:t] + sum_diag, later[t:]], axis=1)
            a = jnp.where(visible, jnp.exp2(log_keep - nz + tot), 0.0)
            acc_refs[ci][...] = jnp.dot(a.astype(BF16), v, preferred_element_type=F32)
            carry_refs[ci][...] = jnp.broadcast_to(sum_old + sum_diag, carry_refs[ci].shape)

    @pl.when(qi == 0)
    def _():
        step(qi, True)

    @pl.when(qi > 0)
    def _():
        first_two_tiles()

    def cond(state):
        j, top = state
        return jnp.logical_and(j >= 0, top >= EXP2_UNDERFLOW)

    def body(state):
        j, _ = state
        step(j, False)
        return j - 1, largest_carry()

    lax.while_loop(cond, body, (qi - 2, largest_carry()))
    for ci, hs in enumerate(heads):
        o_ref[:, hs] = acc_refs[ci][...].astype(o_ref.dtype)


def sb_attention(qkv, heads, name):
    s = qkv.shape[0]
    dh = SB_HEAD_DIM
    t = min(256, s)
    hps = SB_HEADS_PER_STEP
    groups = heads // hps
    kern = functools.partial(_sb_kernel, t=t, hps=hps, c2=dh ** -0.5 / math.log(2.0))
    return pl.pallas_call(
        kern,
        grid=(groups, s // t),
        in_specs=[pl.BlockSpec((t, hps * dh), lambda h, i: (i, h)),
                  pl.BlockSpec((s, hps * dh), lambda h, i: (0, groups + h)),
                  pl.BlockSpec((s, hps * dh), lambda h, i: (0, 2 * groups + h))],
        out_specs=pl.BlockSpec((t, hps * dh), lambda h, i: (i, h)),
        out_shape=jax.ShapeDtypeStruct((s, heads * dh), BF16),
        scratch_shapes=[pltpu.VMEM((t, dh), F32)] * hps + [pltpu.VMEM((t, LANES), F32)] * hps,
        compiler_params=_params("arbitrary", "arbitrary"),
        name=name,
    )(qkv, qkv, qkv)


def _rope_lanes(x, cos, sin_up, sin_dn):
    half = MLA_ROPE_DIM // 2
    return (x * cos + pltpu.roll(x, half, 1) * sin_up + pltpu.roll(x, LANES - half, 1) * sin_dn)


def _mla_prep_kernel(a_ref, gq_ref, gkv_ref, cos_ref, sup_ref, sdn_ref, cq_ref, ckv_ref, kr_ref):
    a = a_ref[...]
    rq, rkv = MLA_Q_RANK, MLA_KV_RANK
    cq = a[:, :rq]
    cq_ref[...] = (cq * lax.rsqrt(jnp.mean(cq * cq, axis=-1, keepdims=True) + RMS_EPS)
                   * gq_ref[...]).astype(BF16)
    ckv = a[:, rq:rq + rkv]
    ckv_ref[...] = (ckv * lax.rsqrt(jnp.mean(ckv * ckv, axis=-1, keepdims=True) + RMS_EPS)
                    * gkv_ref[...]).astype(BF16)
    kr = a[:, rq + rkv:rq + rkv + LANES]
    kr_ref[...] = _rope_lanes(kr, cos_ref[...], sup_ref[...], sdn_ref[...]).astype(BF16)


def mla_prep(a, gq, gkv, cos, sin_up, sin_dn, name):
    s, n = a.shape
    tm = 512
    row = lambda w: pl.BlockSpec((tm, w), lambda i: (i, 0))
    vec = lambda w: pl.BlockSpec((1, w), lambda i: (0, 0))
    return pl.pallas_call(
        _mla_prep_kernel,
        grid=(s // tm,),
        in_specs=[row(n), vec(MLA_Q_RANK), vec(MLA_KV_RANK), row(LANES), row(LANES), row(LANES)],
        out_specs=[row(MLA_Q_RANK), row(MLA_KV_RANK), row(LANES)],
        out_shape=[jax.ShapeDtypeStruct((s, MLA_Q_RANK), BF16),
                   jax.ShapeDtypeStruct((s, MLA_KV_RANK), BF16),
                   jax.ShapeDtypeStruct((s, LANES), BF16)],
        compiler_params=_params("arbitrary"),
        name=name,
    )(a, gq.reshape(1, -1), gkv.reshape(1, -1), cos, sin_up, sin_dn)


def _mla_kernel(q_ref, kv_ref, kr_ref, cos_ref, sup_ref, sdn_ref, o_ref,
                m_ref, l_ref, acc_ref, *, tq, tk, hps, c2):
    qi = pl.program_id(1)
    wide = 2 * LANES
    cos, sin_up, sin_dn = cos_ref[...], sup_ref[...], sdn_ref[...]
    qc = []
    for ci in range(hps):
        q = q_ref[:, ci * wide:(ci + 1) * wide]
        q_rope = _rope_lanes(q[:, LANES:], cos, sin_up, sin_dn)
        qc.append(jnp.concatenate([q[:, :LANES].astype(BF16), q_rope.astype(BF16)], axis=1))
    m_ref[...] = jnp.full_like(m_ref, NEG_BIG)
    l_ref[...] = jnp.zeros_like(l_ref)
    acc_ref[...] = jnp.zeros_like(acc_ref)
    ahead = (lax.broadcasted_iota(jnp.int32, (tq, tk), 0) - lax.broadcasted_iota(jnp.int32, (tq, tk), 1))

    def step(j, diag):
        ks = pl.multiple_of(j * tk, tk)
        kr = kr_ref[pl.ds(ks, tk), :]
        for ci in range(hps):
            kn = kv_ref[pl.ds(ks, tk), ci * wide:ci * wide + LANES]
            v = kv_ref[pl.ds(ks, tk), ci * wide + LANES:(ci + 1) * wide]
            s = lax.dot_general(qc[ci], jnp.concatenate([kn, kr], axis=1), _NT,
                                preferred_element_type=F32) * c2
            if diag is not None:
                s = jnp.where(ahead >= diag * tk, s, NEG_BIG)
            m_prev = m_ref[ci]
            m_new = jnp.maximum(m_prev, jnp.max(s, axis=1, keepdims=True))
            alpha = jnp.exp2(m_prev - m_new)
            p = jnp.exp2(s - jnp.concatenate([m_new] * (tk // LANES), axis=1))
            l_ref[ci] = alpha * l_ref[ci] + jnp.sum(p, axis=1, keepdims=True)
            acc_ref[ci] = alpha * acc_ref[ci] + jnp.dot(p.astype(BF16), v, preferred_element_type=F32)
            m_ref[ci] = m_new

    def body(j, _):
        step(j, None)
        return 0

    per = tq // tk
    lax.fori_loop(0, qi * per, body, 0)
    for dj in range(per):
        step(qi * per + dj, dj)
    for ci in range(hps):
        o_ref[:, ci * LANES:(ci + 1) * LANES] = (acc_ref[ci] / l_ref[ci]).astype(o_ref.dtype)


def mla_attention_core(q, kv, k_rope, cos, sin_up, sin_dn, heads, name):
    assert MLA_V_DIM == LANES and MLA_NOPE_DIM == LANES
    s = q.shape[0]
    tq = min(MLA_QUERY_TILE, s)
    tk = min(MLA_KEY_TILE, s)
    hps = MLA_HEADS_PER_STEP
    kern = functools.partial(_mla_kernel, tq=tq, tk=tk, hps=hps,
                             c2=(MLA_NOPE_DIM + MLA_ROPE_DIM) ** -0.5 / math.log(2.0))
    tab = pl.BlockSpec((tq, LANES), lambda h, i: (i, 0))
    return pl.pallas_call(
        kern,
        grid=(heads // hps, s // tq),
        in_specs=[pl.BlockSpec((tq, hps * 2 * LANES), lambda h, i: (i, h)),
                  pl.BlockSpec((s, hps * 2 * LANES), lambda h, i: (0, h)),
                  pl.BlockSpec((s, LANES), lambda h, i: (0, 0)),
                  tab, tab, tab],
        out_specs=pl.BlockSpec((tq, hps * LANES), lambda h, i: (i, h)),
        out_shape=jax.ShapeDtypeStruct((s, heads * MLA_V_DIM), BF16),
        scratch_shapes=[pltpu.VMEM((hps, tq, LANES), F32), pltpu.VMEM((hps, tq, LANES), F32),
                        pltpu.VMEM((hps, tq, LANES), F32)],
        compiler_params=_params("arbitrary", "arbitrary"),
        name=name,
    )(q, kv, k_rope, cos, sin_up, sin_dn)


def _t5_bucket(rel):
    n = jnp.maximum(rel, 0)
    max_exact = REL_BUCKETS // 2
    nf = jnp.maximum(n, 1).astype(F32)
    large = max_exact + (jnp.log(nf / max_exact) / math.log(REL_MAX_DIST / max_exact)
                         * (REL_BUCKETS - max_exact)).astype(jnp.int32)
    large = jnp.minimum(large, REL_BUCKETS - 1)
    return jnp.where(n < max_exact, n, large)


def _dil_kernel(tab_ref, q_ref, kp_ref, kc_ref, vp_ref, vc_ref, pq_ref, pkp_ref, pkc_ref,
                o_ref, lse_ref, *, heads, scale):
    a = pl.program_id(1)
    w = DIL_WINDOW_KEYS
    dh = DIL_HEAD_DIM
    rel = pq_ref[...] - jnp.concatenate([pkp_ref[...], pkc_ref[...]], axis=1)
    back = (lax.broadcasted_iota(jnp.int32, (w, 2 * w), 0) + w
            - lax.broadcasted_iota(jnp.int32, (w, 2 * w), 1))
    col = lax.broadcasted_iota(jnp.int32, (w, 2 * w), 1)
    first_col = jnp.where(a > 0, 0, w)
    valid = (back >= 0) & (back <= w) & (col >= first_col)
    bucket = _t5_bucket(rel)
    bucket_chunks = [bucket[:, c * LANES:(c + 1) * LANES] for c in range(2 * w // LANES)]
    for h in range(heads):
        hs = slice(h * dh, (h + 1) * dh)
        table = jnp.broadcast_to(tab_ref[h:h + 1, :], (w, LANES))
        bias = jnp.concatenate([jnp.take_along_axis(table, chunk, axis=1) for chunk in bucket_chunks], axis=1)
        k = jnp.concatenate([kp_ref[:, hs], kc_ref[:, hs]], axis=0)
        v = jnp.concatenate([vp_ref[:, hs], vc_ref[:, hs]], axis=0)
        s = lax.dot_general(q_ref[:, hs], k, _NT, preferred_element_type=F32) * scale + bias
        s = jnp.where(valid, s, NEG_BIG)
        m = jnp.max(s, axis=1, keepdims=True)
        p = jnp.exp(s - m)
        l = jnp.sum(p, axis=1, keepdims=True)
        o_ref[:, hs] = jnp.dot(p.astype(BF16), v, preferred_element_type=F32) / l
        lse_ref[:, hs] = jnp.broadcast_to(m + jnp.log(l), (w, dh))


def dilated_group(qkv, pos, tab, g, name):
    s, n = qkv.shape
    d = DIL_DILATIONS[g]
    heads, dh, w = DIL_HEADS, DIL_HEAD_DIM, DIL_WINDOW_KEYS
    hw = heads * dh
    sub = s // d
    nb = sub // w
    pos_col = pos.reshape(sub, d).T.reshape(d, sub, 1)
    pos_row = pos.reshape(sub, d).T.reshape(d, 1, sub)
    prev = lambda i: jnp.maximum(i - 1, 0)
    kern = functools.partial(_dil_kernel, heads=heads, scale=dh ** -0.5)
    return pl.pallas_call(
        kern,
        grid=(d, nb),
        in_specs=[pl.BlockSpec((heads, LANES), lambda r, i: (0, 0)),
                  pl.BlockSpec((w, hw), lambda r, i: (r * nb + i, 0)),
                  pl.BlockSpec((w, hw), lambda r, i: (r * nb + prev(i), 1)),
                  pl.BlockSpec((w, hw), lambda r, i: (r * nb + i, 1)),
                  pl.BlockSpec((w, hw), lambda r, i: (r * nb + prev(i), 2)),
                  pl.BlockSpec((w, hw), lambda r, i: (r * nb + i, 2)),
                  pl.BlockSpec((None, w, 1), lambda r, i: (r, i, 0)),
                  pl.BlockSpec((None, 1, w), lambda r, i: (r, 0, prev(i))),
                  pl.BlockSpec((None, 1, w), lambda r, i: (r, 0, i))],
        out_specs=[pl.BlockSpec((w, hw), lambda r, i: (r * nb + i, 0)),
                   pl.BlockSpec((w, hw), lambda r, i: (r * nb + i, 0))],
        out_shape=[jax.ShapeDtypeStruct((s, hw), F32), jax.ShapeDtypeStruct((s, hw), F32)],
        compiler_params=_params("arbitrary", "arbitrary"),
        name=name,
    )(tab, qkv, qkv, qkv, qkv, qkv, pos_col, pos_row, pos_row)


def _dil_mix_kernel(*refs, dilations):
    n = len(dilations)
    out_ref = refs[2 * n]
    buffers = list(refs[2 * n + 1:])
    outs, lses = [], []
    for g, d in enumerate(dilations):
        o_ref, l_ref = refs[2 * g], refs[2 * g + 1]
        if d == 1:
            outs.append(o_ref[0])
            lses.append(l_ref[0])
            continue
        o_buf, l_buf = buffers.pop(0), buffers.pop(0)
        slabs, rows = o_buf.shape[0], o_buf.shape[1] // d
        for r in range(d):
            for c in range(slabs):
                lanes = slice(c * LANES, (c + 1) * LANES)
                o_buf.at[c][pl.ds(r, rows, stride=d), :] = o_ref[r, :, lanes]
                l_buf.at[c][pl.ds(r, rows, stride=d), :] = l_ref[r, :, lanes]
        outs.append(jnp.concatenate([o_buf[c] for c in range(slabs)], axis=1))
        lses.append(jnp.concatenate([l_buf[c] for c in range(slabs)], axis=1))
    m = functools.reduce(jnp.maximum, lses)
    es = [jnp.exp(l - m) for l in lses]
    num = functools.reduce(lambda a, b: a + b, [e * o for e, o in zip(es, outs)])
    out_ref[...] = (num / functools.reduce(lambda a, b: a + b, es)).astype(out_ref.dtype)


def dilated_mix(parts, name):
    s, hw = parts[0][0].shape
    tm = 256
    flat, specs, scratch = [], [], []
    for (o, lse), d in zip(parts, DIL_DILATIONS):
        for x in (o, lse):
            flat.append(x.reshape(d, s // d, hw))
            specs.append(pl.BlockSpec((d, tm // d, hw), lambda i: (0, i, 0)))
        if d > 1:
            scratch += [pltpu.VMEM((hw // LANES, tm, LANES), F32)] * 2
    return pl.pallas_call(
        functools.partial(_dil_mix_kernel, dilations=DIL_DILATIONS),
        grid=(s // tm,),
        in_specs=specs,
        out_specs=pl.BlockSpec((tm, hw), lambda i: (i, 0)),
        out_shape=jax.ShapeDtypeStruct((s, hw), BF16),
        scratch_shapes=scratch,
        compiler_params=_params("arbitrary"),
        name=name,
    )(*flat)


def _route_tile(x, w_ref, b_ref, route_ref, count_ref, run_ref):
    ng, ne = MOE_GROUPS, MOE_EXPERTS_PER_GROUP
    x_hi = x.astype(BF16)
    x_lo = (x - x_hi.astype(F32)).astype(BF16)
    logits = jnp.dot(jnp.concatenate([x_hi, x_lo, x_hi], axis=1), w_ref[...],
                     preferred_element_type=F32) + b_ref[...]
    lane = lax.broadcasted_iota(jnp.int32, logits.shape, 1).astype(F32)
    big = float(1 << 20)
    is_group = (lane >= ng * ne) & (lane < ng * ne + ng)
    g_logit = jnp.where(is_group, logits, NEG_BIG)
    g_max = jnp.max(g_logit, axis=1, keepdims=True)
    g_top = jnp.min(jnp.where(is_group & (g_logit == g_max), lane, big), axis=1, keepdims=True) - ng * ne
    g_gate = 1.0 / jnp.sum(jnp.where(is_group, jnp.exp(g_logit - g_max), 0.0), axis=1, keepdims=True)
    in_group = (lane >= g_top * ne) & (lane < (g_top + 1) * ne)
    e_logit = jnp.where(in_group, logits, NEG_BIG)
    v1 = jnp.max(e_logit, axis=1, keepdims=True)
    i1 = jnp.min(jnp.where(in_group & (e_logit == v1), lane, big), axis=1, keepdims=True)
    rest = in_group & (lane != i1)
    e_rest = jnp.where(rest, logits, NEG_BIG)
    v2 = jnp.max(e_rest, axis=1, keepdims=True)
    i2 = jnp.min(jnp.where(rest & (e_rest == v2), lane, big), axis=1, keepdims=True)
    e21 = jnp.exp(v2 - v1)
    w1 = g_gate / (1.0 + e21)
    w2 = g_gate * e21 / (1.0 + e21)
    gates = jnp.where(lane == i1, w1, jnp.where(lane == i2, w2, 0.0))

    @pl.when(pl.program_id(0) == 0)
    def _():
        run_ref[...] = jnp.zeros_like(run_ref)

    tm = x.shape[0]
    member = jnp.where((lane == i1) | (lane == i2), 1.0, 0.0)
    earlier = (lax.broadcasted_iota(jnp.int32, (tm, tm), 1)
               < lax.broadcasted_iota(jnp.int32, (tm, tm), 0))
    before = jnp.dot(jnp.where(earlier, 1.0, 0.0).astype(BF16), member.astype(BF16),
                     preferred_element_type=F32)
    run = run_ref[...]
    seen = before + run
    rank1 = jnp.sum(jnp.where(lane == i1, seen, 0.0), axis=1, keepdims=True)
    rank2 = jnp.sum(jnp.where(lane == i2, seen, 0.0), axis=1, keepdims=True)
    run_ref[...] = run + jnp.sum(member, axis=0, keepdims=True)
    count_ref[...] = jnp.broadcast_to(run_ref[...], count_ref.shape)
    route = gates
    for slot, value in enumerate((rank1, rank2, i1, i2)):
        route = jnp.where(lane == ROUTE_INFO_LANE + slot, value, route)
    route_ref[...] = route


def _ln_route_kernel(h_ref, mix_ref, g_ref, b_ref, w_ref, br_ref, of_ref, route_ref, count_ref, run_ref):
    _ln_kernel(h_ref, mix_ref, g_ref, b_ref, of_ref, None)
    _route_tile(of_ref[...], w_ref, br_ref, route_ref, count_ref, run_ref)


def residual_layer_norm_routed(h, mix, g, b, w_router, b_router, name):
    s, d = h.shape
    tm = ROW_TILE
    row = pl.BlockSpec((tm, d), lambda i: (i, 0))
    vec = pl.BlockSpec((1, d), lambda i: (0, 0))
    return pl.pallas_call(
        _ln_route_kernel,
        grid=(s // tm,),
        in_specs=[row, row, vec, vec,
                  pl.BlockSpec((3 * d, LANES), lambda i: (0, 0)),
                  pl.BlockSpec((1, LANES), lambda i: (0, 0))],
        out_specs=[row,
                   pl.BlockSpec((tm, LANES), lambda i: (i, 0)),
                   pl.BlockSpec((8, LANES), lambda i: (0, 0))],
        out_shape=[jax.ShapeDtypeStruct((s, d), F32),
                   jax.ShapeDtypeStruct((s, LANES), F32), jax.ShapeDtypeStruct((8, LANES), F32)],
        scratch_shapes=[pltpu.VMEM((1, LANES), F32)],
        compiler_params=_params("arbitrary"),
        name=name,
    )(h, mix, g.reshape(1, d), b.reshape(1, d), w_router, b_router)


def _pack_bf16_pair(lo, hi):
    lo_bits = pltpu.bitcast(lo.astype(BF16).astype(F32), jnp.uint32) >> 16
    hi_bits = pltpu.bitcast(hi.astype(BF16).astype(F32), jnp.uint32) & jnp.uint32(0xFFFF0000)
    return lo_bits | hi_bits


def _unpack_bf16_pair(packed):
    lo = pltpu.bitcast(packed << 16, F32).astype(BF16)
    hi = pltpu.bitcast(packed & jnp.uint32(0xFFFF0000), F32).astype(BF16)
    return lo, hi


def _dispatch_kernel(pos_ref, h_ref, route_ref, xs_in, xs_ref, row_ref, sem):
    del xs_in
    tm, d = h_ref.shape
    base = pl.program_id(0) * tm
    row_ref[:, :d // 2] = _pack_bf16_pair(h_ref[:, :d // 2], h_ref[:, d // 2:])
    row_ref[:, d // 2:] = pltpu.bitcast(route_ref[...], jnp.uint32)

    def row_copies(r):
        return [pltpu.make_async_copy(row_ref.at[pl.ds(r, 1), :],
                                      xs_ref.at[pl.ds(pos_ref[k, base + r], 1), :], sem.at[0])
                for k in range(MOE_TOP_K)]

    def start(r, carry):
        for cp in row_copies(r):
            cp.start()
        return carry

    def wait(r, carry):
        for cp in row_copies(r):
            cp.wait()
        return carry

    lax.fori_loop(0, tm, start, 0, unroll=DMA_LOOP_UNROLL)
    lax.fori_loop(0, tm, wait, 0, unroll=DMA_LOOP_UNROLL)


def moe_dispatch(h, route, pos, rows, name):
    s, d = h.shape
    tm = ROW_TILE
    width = d // 2 + LANES
    grid_spec = pltpu.PrefetchScalarGridSpec(
        num_scalar_prefetch=1,
        grid=(s // tm,),
        in_specs=[pl.BlockSpec((tm, d), lambda i, pos: (i, 0)),
                  pl.BlockSpec((tm, LANES), lambda i, pos: (i, 0)),
                  pl.BlockSpec(memory_space=pl.ANY)],
        out_specs=pl.BlockSpec(memory_space=pl.ANY),
        scratch_shapes=[pltpu.VMEM((tm, width), jnp.uint32), pltpu.SemaphoreType.DMA((1,))],
    )
    return pl.pallas_call(
        _dispatch_kernel,
        grid_spec=grid_spec,
        out_shape=jax.ShapeDtypeStruct((rows, width), jnp.uint32),
        input_output_aliases={3: 0},
        compiler_params=_params("arbitrary"),
        name=name,
    )(pos, h, route, jnp.zeros((rows, width), jnp.uint32))


def _expert_kernel(te_ref, x_ref, route_ref, wg_ref, wu_ref, wd_ref, o_ref, wg_bf, wu_bf, wd_bf):
    t = pl.program_id(0)
    expert = te_ref[t]
    in_use = t < te_ref[pl.num_programs(0)]

    @pl.when(jnp.logical_not(in_use))
    def _():
        o_ref[...] = jnp.zeros_like(o_ref)

    @pl.when(in_use)
    def _():
        @pl.when(jnp.logical_or(t == 0, expert != te_ref[jnp.maximum(t - 1, 0)]))
        def _():
            wg_bf[...] = wg_ref[...].astype(BF16)
            wu_bf[...] = wu_ref[...].astype(BF16)
            wd_bf[...] = wd_ref[...].astype(BF16)

        x = jnp.concatenate(_unpack_bf16_pair(x_ref[...]), axis=1)
        gate = jnp.dot(x, wg_bf[...], preferred_element_type=F32)
        up = jnp.dot(x, wu_bf[...], preferred_element_type=F32)
        lane = lax.broadcasted_iota(jnp.int32, route_ref.shape, 1)
        route = pltpu.bitcast(route_ref[...], F32)
        comb = jnp.sum(jnp.where(lane == expert, route, 0.0), axis=1, keepdims=True)
        hidden = (gate * jax.nn.sigmoid(gate) * up * comb).astype(BF16)
        o_ref[...] = jnp.dot(hidden, wd_bf[...], preferred_element_type=F32)


def _used_tile(t, te, n_tiles):
    return jnp.minimum(t, te[n_tiles] - 1)


def moe_experts(xs, tile_expert, w_gate, w_up, w_down, layer, name):
    rows = xs.shape[0]
    d, f = w_gate.shape[-2:]
    tm = MOE_TILE
    n = rows // tm
    up_spec = pl.BlockSpec((None, None, d, f), lambda t, te: (layer, te[_used_tile(t, te, n)], 0, 0))
    grid_spec = pltpu.PrefetchScalarGridSpec(
        num_scalar_prefetch=1,
        grid=(n,),
        in_specs=[pl.BlockSpec((tm, d // 2), lambda t, te: (_used_tile(t, te, n), 0)),
                  pl.BlockSpec((tm, LANES), lambda t, te: (_used_tile(t, te, n), d // 2 // LANES)),
                  up_spec, up_spec,
                  pl.BlockSpec((None, None, f, d), lambda t, te: (layer, te[_used_tile(t, te, n)], 0, 0))],
        out_specs=pl.BlockSpec((tm, d), lambda t, te: (t, 0)),
        scratch_shapes=[pltpu.VMEM((d, f), BF16), pltpu.VMEM((d, f), BF16), pltpu.VMEM((f, d), BF16)],
    )
    return pl.pallas_call(
        _expert_kernel,
        grid_spec=grid_spec,
        out_shape=jax.ShapeDtypeStruct((rows, d), F32),
        compiler_params=_params("arbitrary"),
        name=name,
    )(tile_expert, xs, xs, w_gate, w_up, w_down)


def _ln_gather_kernel(pos_ref, h_ref, ys_ref, g_ref, b_ref, of_ref, ob_ref, *rest, dilations):
    strided_refs, (y_buf, sem, *stage) = rest[:len(dilations)], rest[len(dilations):]
    tm = h_ref.shape[0]
    i = pl.program_id(0)

    def row_copies(tile, r):
        return [pltpu.make_async_copy(ys_ref.at[pl.ds(pos_ref[k, tile * tm + r], 1), :],
                                      y_buf.at[tile % 2, k, pl.ds(r, 1), :], sem.at[tile % 2])
                for k in range(MOE_TOP_K)]

    def start_tile(tile):
        def start(r, carry):
            for cp in row_copies(tile, r):
                cp.start()
            return carry
        lax.fori_loop(0, tm, start, 0, unroll=DMA_LOOP_UNROLL)

    def wait_tile(tile):
        def wait(r, carry):
            for cp in row_copies(tile, r):
                cp.wait()
            return carry
        lax.fori_loop(0, tm, wait, 0, unroll=DMA_LOOP_UNROLL)

    @pl.when(i == 0)
    def _():
        start_tile(i)

    @pl.when(i + 1 < pl.num_programs(0))
    def _():
        start_tile(i + 1)

    wait_tile(i)
    ffn_ref = y_buf.at[i % 2, 0]
    ffn_ref[...] = functools.reduce(lambda a, b: a + b, [y_buf[i % 2, k] for k in range(MOE_TOP_K)])
    _ln_kernel(h_ref, ffn_ref, g_ref, b_ref, of_ref, ob_ref, *strided_refs, dilations=dilations,
               stage_ref=stage[0] if stage else None)


def residual_layer_norm_gathered(h, ys, pos, g, b, name, dilations=()):
    s, d = h.shape
    tm = ROW_TILE
    row = pl.BlockSpec((tm, d), lambda i, pos: (i, 0))
    vec = pl.BlockSpec((1, d), lambda i, pos: (0, 0))
    grid_spec = pltpu.PrefetchScalarGridSpec(
        num_scalar_prefetch=1,
        grid=(s // tm,),
        in_specs=[row, pl.BlockSpec(memory_space=pl.ANY), vec, vec],
        out_specs=[row, row] + [pl.BlockSpec((dil, tm // dil, d), lambda i, pos: (0, i, 0))
                                for dil in dilations],
        scratch_shapes=[pltpu.VMEM((2, MOE_TOP_K, tm, d), F32), pltpu.SemaphoreType.DMA((2,))]
        + ([pltpu.VMEM((d // LANES, tm, LANES), F32)] if dilations else []),
    )
    return pl.pallas_call(
        functools.partial(_ln_gather_kernel, dilations=tuple(dilations)),
        grid_spec=grid_spec,
        out_shape=[jax.ShapeDtypeStruct((s, d), F32), jax.ShapeDtypeStruct((s, d), BF16)]
        + [jax.ShapeDtypeStruct((dil, s // dil, d), BF16) for dil in dilations],
        compiler_params=_params("arbitrary"),
        name=name,
    )(pos, h, ys, g.reshape(1, d), b.reshape(1, d))


def router_weights(w_gr, b_gr, w_er, b_er):
    d = w_gr.shape[0]
    ng, ne = MOE_GROUPS, MOE_EXPERTS_PER_GROUP
    w_router = jnp.concatenate(
        [jnp.transpose(w_er, (1, 0, 2)).reshape(d, ng * ne), w_gr,
         jnp.zeros((d, LANES - ng * ne - ng), F32)], axis=1)
    b_router = jnp.concatenate(
        [b_er.reshape(ng * ne), b_gr, jnp.zeros((LANES - ng * ne - ng,), F32)]).reshape(1, LANES)
    w_hi = w_router.astype(BF16)
    w_lo = (w_router - w_hi.astype(F32)).astype(BF16)
    return jnp.concatenate([w_hi, w_hi, w_lo], axis=0), b_router


def hierarchical_moe(h_f32, route, counts, w_gate, w_up, w_down, layer, tag):
    s, d = h_f32.shape
    n_experts = MOE_GROUPS * MOE_EXPERTS_PER_GROUP
    tm = MOE_TILE
    counts = counts[0, :n_experts].astype(jnp.int32)
    padded = (counts + tm - 1) // tm * tm
    ends = jnp.cumsum(padded)
    starts = ends - padded
    info = route[:, ROUTE_INFO_LANE:ROUTE_INFO_LANE + 2 * MOE_TOP_K].astype(jnp.int32)
    pos = jnp.stack([starts[info[:, MOE_TOP_K + k]] + info[:, k] for k in range(MOE_TOP_K)])
    rows = MOE_TOP_K * s + n_experts * tm
    tile_start = jnp.arange(rows // tm, dtype=jnp.int32) * tm
    tile_expert = jnp.minimum(jnp.sum(tile_start[:, None] >= ends[None, :], axis=1),
                              n_experts - 1).astype(jnp.int32)
    tile_expert = jnp.concatenate([tile_expert, ends[-1:] // tm])
    xs = moe_dispatch(h_f32, route, pos, rows, f"moe_dispatch_{tag}")
    ys = moe_experts(xs, tile_expert, w_gate, w_up, w_down, layer, f"moe_experts_{tag}")
    return ys, pos


def stick_breaking_mixer(h_bf16, w_qkv, w_o, j, tag):
    qkv = matmul(h_bf16, w_qkv, BF16, f"sb_qkv_{tag}", lead=(j,))
    o = sb_attention(qkv, SB_HEADS, f"sb_attn_{tag}")
    return matmul(o, w_o, F32, f"sb_out_{tag}", lead=(j,))


def _rope_tables(pos):
    half = MLA_ROPE_DIM // 2
    inv_freq = ROPE_THETA ** (-jnp.arange(half, dtype=F32) / half)
    ang = pos.astype(F32)[:, None] * inv_freq
    cos, sin = jnp.cos(ang), jnp.sin(ang)
    zeros = jnp.zeros((pos.shape[0], LANES - 2 * half), F32)
    z_half = jnp.zeros_like(sin)
    cos_t = jnp.concatenate([cos, cos, zeros], axis=1)
    sin_up = jnp.concatenate([z_half, sin, zeros], axis=1)
    sin_dn = jnp.concatenate([-sin, z_half, zeros], axis=1)
    return cos_t, sin_up, sin_dn


def mla_mixer(h_bf16, pos, w_q_a, q_a_norm, w_q_b, w_kv_a, kv_a_norm, w_kv_b, w_o, j, tag):
    d = h_bf16.shape[1]
    heads = MLA_HEADS
    nope, rope = MLA_NOPE_DIM, MLA_ROPE_DIM
    used = MLA_Q_RANK + MLA_KV_RANK + rope
    width = -(-(MLA_Q_RANK + MLA_KV_RANK + LANES) // 512) * 512
    w_a = jnp.concatenate([w_q_a, w_kv_a, jnp.zeros((d, width - used), F32)], axis=1)
    a = matmul(h_bf16, w_a, F32, f"mla_a_{tag}")
    cos_t, sin_up, sin_dn = _rope_tables(pos)
    cq, ckv, k_rope = mla_prep(a, q_a_norm, kv_a_norm, cos_t, sin_up, sin_dn, f"mla_prep_{tag}")
    w_qb = jnp.pad(w_q_b.reshape(MLA_Q_RANK, heads, nope + rope),
                   ((0, 0), (0, 0), (0, 2 * LANES - nope - rope))).reshape(MLA_Q_RANK, heads * 2 * LANES)
    q = matmul(cq, w_qb, F32, f"mla_qb_{tag}")
    kv = matmul(ckv, w_kv_b, BF16, f"mla_kvb_{tag}", lead=(j,))
    o = mla_attention_core(q, kv, k_rope, cos_t, sin_up, sin_dn, heads, f"mla_attn_{tag}")
    return matmul(o, w_o, F32, f"mla_out_{tag}", lead=(j,))


def dilated_mixer(h_by_dilation, pos, rel_bias, w_qkv, w_o, j, tag):
    groups = DIL_GROUPS
    hw = DIL_HEADS * DIL_HEAD_DIM
    tabs = rel_bias.reshape(REL_BUCKETS, groups, DIL_HEADS)
    parts = []
    for g, d in enumerate(DIL_DILATIONS):
        def col_block(jt, tn, g=g):
            per = hw // tn
            return ((jt // per) * groups + g) * per + jt % per
        qkv = matmul(h_by_dilation[d], w_qkv, BF16, f"dil_qkv_{tag}_g{g}", lead=(j,),
                     n_out=3 * hw, col_block=col_block)
        tab = jnp.pad(tabs[:, g, :].T, ((0, 0), (0, LANES - REL_BUCKETS)))
        parts.append(dilated_group(qkv, pos, tab, g, f"dil_attn_{tag}_g{g}"))
    o = dilated_mix(parts, f"dil_mix_{tag}")
    return matmul(o, w_o, F32, f"dil_out_{tag}", lead=(j,))


def kernel(x, positions, rel_bias, sb_w_qkv, sb_w_o, mla_w_q_a, mla_q_a_norm, mla_w_q_b, mla_w_kv_a,
           mla_kv_a_norm, mla_w_kv_b, mla_w_o, dil_w_qkv, dil_w_o, ln_gain, ln_bias,
           moe_w_group_router, moe_b_group_router, moe_w_expert_router, moe_b_expert_router,
           moe_w_gate, moe_w_up, moe_w_down):
    batch, seq, d = x.shape
    outs = []
    for b in range(batch):
        h = x[b]
        h_bf16 = h.astype(BF16)
        pos = positions[b]
        strided = ()
        assert N_MIXERS > 2 and DEPTH > 0
        for i in range(DEPTH):
            kind, j = i % N_MIXERS, i // N_MIXERS
            tag = f"l{i}"
            if kind == 0:
                mix = stick_breaking_mixer(h_bf16, sb_w_qkv, sb_w_o, j, tag)
            elif kind == 1:
                mix = mla_mixer(h_bf16, pos, mla_w_q_a[j], mla_q_a_norm[j], mla_w_q_b[j], mla_w_kv_a[j],
                                mla_kv_a_norm[j], mla_w_kv_b, mla_w_o, j, tag)
            else:
                h_by_dilation = {1: h_bf16}
                h_by_dilation.update({dil: hs.reshape(seq, d) for dil, hs in zip(next_dilations, strided)})
                mix = dilated_mixer(h_by_dilation, pos, rel_bias, dil_w_qkv, dil_w_o, j, tag)
            w_router, b_router = router_weights(moe_w_group_router[i], moe_b_group_router[i],
                                                moe_w_expert_router[i], moe_b_expert_router[i])
            h, route, counts = residual_layer_norm_routed(
                h, mix, ln_gain[i, 0], ln_bias[i, 0], w_router, b_router, f"ln_mix_{tag}")
            ys, slot = hierarchical_moe(h, route, counts, moe_w_gate, moe_w_up, moe_w_down, i, tag)
            feeds_dilated = i + 1 < DEPTH and (i + 1) % N_MIXERS == 2
            next_dilations = tuple(dil for dil in DIL_DILATIONS if dil > 1) if feeds_dilated else ()
            h, h_bf16, *strided = residual_layer_norm_gathered(h, ys, slot, ln_gain[i, 1], ln_bias[i, 1],
                                                               f"ln_ffn_{tag}", dilations=next_dilations)
        outs.append(h)
    return jnp.stack(outs)
```
